```python
import jax, jax.numpy as jnp
from jax import lax
import numpy as np

D_MODEL = 1024
BATCH = 8
SEQ = 2048
DEPTH = 2
DEC_BATCH = 32
DEC_SEQ = 8
PAST_LEN = 8192
PAGE_SIZE = 128

N_LAYERS_A = (DEPTH + 1) // 2
N_LAYERS_B = DEPTH // 2
CHUNK = 128
D_A = 2 * D_MODEL
A_GROUPS = 8
A_GROUP_DIM = D_A // A_GROUPS
B_HEADS = 16
B_HEAD_DIM = D_MODEL // B_HEADS
D_B = B_HEADS * B_HEAD_DIM
Q_BLOCK = 128
D_FF = 7 * D_MODEL // 2
N_EXPERTS = 8
TOP_K = 2
FORGET_BIAS = 4.0
EPS = 1e-6

kernel_name = "fox_gmlp_hybrid_decode_step"


def rmsnorm(x, g):
    xf = x.astype(jnp.float32)
    y = xf * lax.rsqrt(jnp.mean(xf * xf, axis=-1, keepdims=True) + EPS)
    return (y * g.astype(jnp.float32)).astype(x.dtype)


def layernorm(x, g, b):
    xf = x.astype(jnp.float32)
    mu = jnp.mean(xf, axis=-1, keepdims=True)
    var = jnp.mean(jnp.square(xf - mu), axis=-1, keepdims=True)
    y = (xf - mu) * lax.rsqrt(var + EPS)
    return (y * g.astype(jnp.float32) + b.astype(jnp.float32)).astype(x.dtype)


def adaln(c, w, b):
    mod = jax.nn.silu(c) @ w + b
    return jnp.split(mod[:, None, :], 6, axis=-1)


def modulate(h, shift, scale):
    return h * (1 + scale) + shift


def gmlp_mixer(h, w_in, v_g, v_b, w_s, b_s, w_out):
    bsz, t, _ = h.shape
    L = min(t, CHUNK)
    n_chunks = t // L
    z = jax.nn.gelu(h @ w_in, approximate=False)
    u, v = jnp.split(z, 2, axis=-1)
    v = layernorm(v, v_g, v_b)
    vh = v.reshape(bsz, n_chunks, L, A_GROUPS, A_GROUP_DIM)
    causal = jnp.tril(jnp.ones((L, L), dtype=bool))
    ws = jnp.where(causal[None], w_s[:, :L, :L], 0)
    mixed = jnp.einsum('gts,bcsgd->bctgd', ws, vh) + b_s[:, :L].T[None, None, :, :, None]
    out = u * mixed.reshape(bsz, t, D_A)
    return out @ w_out, v


def fox_project(h, w_in, b_f):
    bsz, t, _ = h.shape
    proj = h @ w_in
    q = proj[..., :D_B].reshape(bsz, t, B_HEADS, B_HEAD_DIM)
    k = proj[..., D_B:2 * D_B].reshape(bsz, t, B_HEADS, B_HEAD_DIM)
    v = proj[..., 2 * D_B:3 * D_B].reshape(bsz, t, B_HEADS, B_HEAD_DIM)
    logf = jax.nn.log_sigmoid((proj[..., 3 * D_B:] + b_f).astype(jnp.float32))
    return q, k, v, logf


def fox_prompt(h, w_in, b_f, w_out):
    q, k, v, logf = fox_project(h, w_in, b_f)
    bsz, t = h.shape[0], h.shape[1]
    cum = lax.cumsum(logf, axis=1)
    n_blocks = t // Q_BLOCK
    q_blocks = q.reshape(bsz, n_blocks, Q_BLOCK, B_HEADS, B_HEAD_DIM).transpose(1, 0, 2, 3, 4)
    cum_blocks = cum.reshape(bsz, n_blocks, Q_BLOCK, B_HEADS).transpose(1, 0, 2, 3)
    starts = jnp.arange(n_blocks, dtype=jnp.int32) * Q_BLOCK
    key_pos = jnp.arange(t, dtype=jnp.int32)
    cum_k = cum.transpose(0, 2, 1)
    scale = B_HEAD_DIM ** -0.5

    def attend(args):
        qb, cb, start = args
        s = jnp.einsum('bqhd,bkhd->bhqk', qb, k).astype(jnp.float32) * scale
        s = s + cb.transpose(0, 2, 1)[..., None] - cum_k[:, :, None, :]
        qpos = start + jnp.arange(Q_BLOCK, dtype=jnp.int32)
        s = jnp.where(key_pos[None, :] <= qpos[:, None], s, -jnp.inf)
        p = jax.nn.softmax(s, axis=-1).astype(v.dtype)
        return jnp.einsum('bhqk,bkhd->bqhd', p, v)

    o = lax.map(attend, (q_blocks, cum_blocks, starts))
    o = o.transpose(1, 0, 2, 3, 4).reshape(bsz, t, D_B)
    return o @ w_out, k, v, logf


def fox_sample(h, cache_k, cache_v, cache_logf, page_table, w_in, b_f, w_out):
    q, k, v, logf = fox_project(h, w_in, b_f)
    bsz, t = h.shape[0], h.shape[1]
    past = page_table.shape[1] * PAGE_SIZE
    pk = cache_k[page_table].reshape(bsz, past, B_HEADS, B_HEAD_DIM)
    pv = cache_v[page_table].reshape(bsz, past, B_HEADS, B_HEAD_DIM)
    plf = cache_logf[page_table].reshape(bsz, past, B_HEADS).astype(jnp.float32)
    rev = lax.cumsum(plf, axis=1, reverse=True)
    suf = jnp.concatenate([rev[:, 1:], jnp.zeros_like(rev[:, :1])], axis=1)
    cn = lax.cumsum(logf, axis=1)
    cn_t = cn.transpose(0, 2, 1)
    scale = B_HEAD_DIM ** -0.5
    s_past = jnp.einsum('bqhd,bkhd->bhqk', q, pk).astype(jnp.float32) * scale
    s_past = s_past + cn_t[..., None] + suf.transpose(0, 2, 1)[:, :, None, :]
    s_new = jnp.einsum('bqhd,bkhd->bhqk', q, k).astype(jnp.float32) * scale
    s_new = s_new + cn_t[..., None] - cn_t[:, :, None, :]
    causal = jnp.tril(jnp.ones((t, t), dtype=bool))
    s_new = jnp.where(causal, s_new, -jnp.inf)
    p = jax.nn.softmax(jnp.concatenate([s_past, s_new], axis=-1), axis=-1).astype(v.dtype)
    o = jnp.einsum('bhqk,bkhd->bqhd', p[..., :past], pv) + jnp.einsum('bhqk,bkhd->bqhd', p[..., past:], v)
    return o.reshape(bsz, t, D_B) @ w_out, k, v, logf


def swiglu(h, w_gu, w_down):
    g, u = jnp.split(h @ w_gu, 2, axis=-1)
    return (jax.nn.silu(g) * u) @ w_down


def moe_ffn(h, w_r, b_r, w_gu, w_down):
    logits = (h @ w_r).astype(jnp.float32) + b_r.astype(jnp.float32)
    top_vals, top_idx = lax.top_k(logits, TOP_K)
    gates = jax.nn.softmax(top_vals, axis=-1)
    combine = jnp.sum(jax.nn.one_hot(top_idx, N_EXPERTS, dtype=jnp.float32) * gates[..., None], axis=-2)
    combine = combine.astype(h.dtype)
    y = jnp.zeros_like(h)
    for e in range(N_EXPERTS):
        y = y + combine[..., e:e + 1] * swiglu(h, w_gu[e], w_down[e])
    return y


def setup_inputs(seed: int = 0) -> dict:
    key = jax.random.key(seed)
    ks = jax.random.split(key, 32)
    n_pages = PAST_LEN // PAGE_SIZE
    n_used = DEC_BATCH * n_pages
    n_phys = n_used + (n_used + 3) // 4
    nrm = jax.random.normal
    f32 = jnp.float32
    page_table = jax.random.permutation(ks[0], n_phys)[:n_used].reshape(DEC_BATCH, n_pages).astype(jnp.int32)
    return {
        "x_prompt": nrm(ks[1], (BATCH, SEQ, D_MODEL), f32),
        "x_sample": nrm(ks[2], (DEC_BATCH, DEC_SEQ, D_MODEL), f32),
        "c_prompt": nrm(ks[3], (BATCH, D_MODEL), f32),
        "c_sample": nrm(ks[4], (DEC_BATCH, D_MODEL), f32),
        "cache_k": nrm(ks[5], (N_LAYERS_B, n_phys, PAGE_SIZE, B_HEADS, B_HEAD_DIM), f32),
        "cache_v": nrm(ks[6], (N_LAYERS_B, n_phys, PAGE_SIZE, B_HEADS, B_HEAD_DIM), f32),
        "cache_logf": jax.nn.log_sigmoid(FORGET_BIAS + nrm(ks[7], (N_LAYERS_B, n_phys, PAGE_SIZE, B_HEADS), f32)),
        "page_table": page_table,
        "norm_mix_g": 1.0 + 0.05 * nrm(ks[8], (DEPTH, D_MODEL), f32),
        "norm_ffn_g": 1.0 + 0.05 * nrm(ks[9], (DEPTH, D_MODEL), f32),
        "final_norm_g": 1.0 + 0.05 * nrm(ks[10], (D_MODEL,), f32),
        "ada_w": 0.5 * D_MODEL ** -0.5 * nrm(ks[11], (DEPTH, D_MODEL, 6 * D_MODEL), f32),
        "ada_b": 0.02 * nrm(ks[12], (DEPTH, 6 * D_MODEL), f32),
        "gmlp_w_in": D_MODEL ** -0.5 * nrm(ks[13], (N_LAYERS_A, D_MODEL, 2 * D_A), f32),
        "gmlp_v_g": 1.0 + 0.05 * nrm(ks[14], (N_LAYERS_A, D_A), f32),
        "gmlp_v_b": 0.02 * nrm(ks[15], (N_LAYERS_A, D_A), f32),
        "gmlp_w_s": CHUNK ** -0.5 * nrm(ks[16], (N_LAYERS_A, A_GROUPS, CHUNK, CHUNK), f32),
        "gmlp_b_s": 1.0 + 0.1 * nrm(ks[17], (N_LAYERS_A, A_GROUPS, CHUNK), f32),
        "gmlp_w_out": D_A ** -0.5 * nrm(ks[18], (N_LAYERS_A, D_A, D_MODEL), f32),
        "fox_w_in": D_MODEL ** -0.5 * nrm(ks[19], (N_LAYERS_B, D_MODEL, 3 * D_B + B_HEADS), f32),
        "fox_b_f": FORGET_BIAS + 0.5 * nrm(ks[20], (N_LAYERS_B, B_HEADS), f32),
        "fox_w_out": D_B ** -0.5 * nrm(ks[21], (N_LAYERS_B, D_B, D_MODEL), f32),
        "ffn_w_gu": D_MODEL ** -0.5 * nrm(ks[22], (N_LAYERS_A, D_MODEL, 2 * D_FF), f32),
        "ffn_w_down": D_FF ** -0.5 * nrm(ks[23], (N_LAYERS_A, D_FF, D_MODEL), f32),
        "moe_w_r": D_MODEL ** -0.5 * nrm(ks[24], (N_LAYERS_B, D_MODEL, N_EXPERTS), f32),
        "moe_b_r": 0.01 * nrm(ks[25], (N_LAYERS_B, N_EXPERTS), f32),
        "moe_w_gu": D_MODEL ** -0.5 * nrm(ks[26], (N_LAYERS_B, N_EXPERTS, D_MODEL, 2 * D_FF), f32),
        "moe_w_down": D_FF ** -0.5 * nrm(ks[27], (N_LAYERS_B, N_EXPERTS, D_FF, D_MODEL), f32),
    }


def reference(x_prompt, x_sample, c_prompt, c_sample, cache_k, cache_v, cache_logf, page_table,
              norm_mix_g, norm_ffn_g, final_norm_g, ada_w, ada_b,
              gmlp_w_in, gmlp_v_g, gmlp_v_b, gmlp_w_s, gmlp_b_s, gmlp_w_out,
              fox_w_in, fox_b_f, fox_w_out,
              ffn_w_gu, ffn_w_down,
              moe_w_r, moe_b_r, moe_w_gu, moe_w_down):
    xp, xs = x_prompt, x_sample
    a_v_s = []
    k_p, v_p, lf_p, k_s, v_s, lf_s = [], [], [], [], [], []
    for i in range(DEPTH):
        li = i // 2
        shp1, scp1, gtp1, shp2, scp2, gtp2 = adaln(c_prompt, ada_w[i], ada_b[i])
        shs1, scs1, gts1, shs2, scs2, gts2 = adaln(c_sample, ada_w[i], ada_b[i])
        hp = modulate(rmsnorm(xp, norm_mix_g[i]), shp1, scp1)
        hs = modulate(rmsnorm(xs, norm_mix_g[i]), shs1, scs1)
        if i % 2 == 0:
            mp, _ = gmlp_mixer(hp, gmlp_w_in[li], gmlp_v_g[li], gmlp_v_b[li], gmlp_w_s[li], gmlp_b_s[li], gmlp_w_out[li])
            ms, vrows = gmlp_mixer(hs, gmlp_w_in[li], gmlp_v_g[li], gmlp_v_b[li], gmlp_w_s[li], gmlp_b_s[li], gmlp_w_out[li])
            a_v_s.append(vrows)
        else:
            mp, kp_, vp_, lfp_ = fox_prompt(hp, fox_w_in[li], fox_b_f[li], fox_w_out[li])
            ms, ks_, vs_, lfs_ = fox_sample(hs, cache_k[li], cache_v[li], cache_logf[li], page_table,
                                           fox_w_in[li], fox_b_f[li], fox_w_out[li])
            k_p.append(kp_); v_p.append(vp_); lf_p.append(lfp_)
            k_s.append(ks_); v_s.append(vs_); lf_s.append(lfs_)
        xp = xp + gtp1 * mp
        xs = xs + gts1 * ms
        hp = modulate(rmsnorm(xp, norm_ffn_g[i]), shp2, scp2)
        hs = modulate(rmsnorm(xs, norm_ffn_g[i]), shs2, scs2)
        if i % 2 == 0:
            fp = swiglu(hp, ffn_w_gu[li], ffn_w_down[li])
            fs = swiglu(hs, ffn_w_gu[li], ffn_w_down[li])
        else:
            fp = moe_ffn(hp, moe_w_r[li], moe_b_r[li], moe_w_gu[li], moe_w_down[li])
            fs = moe_ffn(hs, moe_w_r[li], moe_b_r[li], moe_w_gu[li], moe_w_down[li])
        xp = xp + gtp2 * fp
        xs = xs + gts2 * fs
    y_prompt = rmsnorm(xp, final_norm_g)
    y_sample = rmsnorm(xs, final_norm_g)
    state_a_v_sample = jnp.stack(a_v_s)
    k_prompt = jnp.stack(k_p)
    v_prompt = jnp.stack(v_p)
    logf_prompt = jnp.stack(lf_p)
    k_sample = jnp.stack(k_s)
    v_sample = jnp.stack(v_s)
    logf_sample = jnp.stack(lf_s)
    return (y_prompt, y_sample, state_a_v_sample, k_prompt, v_prompt, logf_prompt, k_sample, v_sample, logf_sample)
```

```python
import functools

import jax
import jax.numpy as jnp
from jax import lax
from jax.experimental import pallas as pl
from jax.experimental.pallas import tpu as pltpu

F32 = jnp.float32
BF16 = jnp.bfloat16

N_HEADS = 16
GMLP_GROUPS = 8
GMLP_CHUNK = 128
TOP_K = 2
EPS = 1e-6
NEG = -1e30

LANES = 128
SUBLANES = 8
VMEM_LIMIT = 56 * 1024 * 1024

TM_GMLP = 512
TM_FFN = 1024
TF_FFN = 512
TM_PROJ = 512
TQ_ATTN = 256
TM_POST = 256
SUB = 256
TS_MOE = 2048
PAGES_PER_STEP = 8


def _cparams(*sem):
    return pltpu.CompilerParams(dimension_semantics=sem, vmem_limit_bytes=VMEM_LIMIT)


def _dot(a, b):
    return jnp.dot(a, b, preferred_element_type=F32)


def _dot_nt(a, b):
    return lax.dot_general(a, b, (((1,), (1,)), ((), ())), preferred_element_type=F32)


def _norm_mod(x, g, shift, scale):
    ms = jnp.mean(x * x, axis=-1, keepdims=True)
    y = x * lax.rsqrt(ms + EPS) * g
    return y * (1.0 + scale) + shift


def _gelu(x):
    return 0.5 * x * (1.0 + lax.erf(x * (2.0 ** -0.5)))


def _silu(x):
    return x * jax.nn.sigmoid(x)


def _ones_where(cond):
    return jnp.where(cond, 1.0, 0.0).astype(BF16)


def _shift_div(x, c):
    assert c & (c - 1) == 0
    return lax.shift_right_logical(x, c.bit_length() - 1)


def _split3(x):
    hi = x.astype(BF16)
    r = x - hi.astype(F32)
    mid = r.astype(BF16)
    lo = (r - mid.astype(F32)).astype(BF16)
    return hi, mid, lo


def _ada_kernel(c_ref, w_ref, b_ref, o_ref):
    s = _silu(c_ref[...]).astype(BF16)
    o_ref[...] = _dot(s, w_ref[...].astype(BF16)) + b_ref[...]


def _ada_call(c_all, ada_w, ada_b):
    n_layers, d, d6 = ada_w.shape
    r = c_all.shape[0]
    tn = min(d6, 1536)
    return pl.pallas_call(
        _ada_kernel,
        grid=(n_layers, d6 // tn),
        in_specs=[
            pl.BlockSpec((r, d), lambda l, j: (0, 0)),
            pl.BlockSpec((None, d, tn), lambda l, j: (l, 0, j)),
            pl.BlockSpec((None, 1, tn), lambda l, j: (l, 0, j)),
        ],
        out_specs=pl.BlockSpec((None, r, tn), lambda l, j: (l, 0, j)),
        out_shape=jax.ShapeDtypeStruct((n_layers, r, d6), F32),
        compiler_params=_cparams("arbitrary", "arbitrary"),
        name="ada",
    )(c_all, ada_w, ada_b.reshape(n_layers, 1, d6))


def _mod_spec(mod, tiles_per_seq):
    _, rows, d = mod.shape
    return pl.BlockSpec((None, rows, d), lambda i, *_: (i // tiles_per_seq, 0, 0))


def _gmlp_kernel(x_ref, g_ref, sh_ref, sc_ref, gt_ref, wu_ref, wv_ref, vg_ref, vb_ref, ws_ref, bs_ref,
                 wo_ref, *rest, lc, period, groups, emit_v):
    if emit_v:
        o_ref, v_ref, vn_scr, out_scr = rest
    else:
        o_ref, vn_scr, out_scr = rest
    x = x_ref[...]
    tm = x.shape[0]
    h = _norm_mod(x, g_ref[...], sh_ref[...], sc_ref[...]).astype(BF16)
    v = _gelu(_dot(h, wv_ref[...]))
    mu = jnp.mean(v, axis=-1, keepdims=True)
    vc = v - mu
    var = jnp.mean(vc * vc, axis=-1, keepdims=True)
    vn = vc * lax.rsqrt(var + EPS) * vg_ref[...] + vb_ref[...]
    if emit_v:
        v_ref[...] = vn
    vn_scr[...] = vn.astype(BF16)
    gd = vn.shape[1] // groups
    r = lax.broadcasted_iota(jnp.int32, (lc, lc), 0)
    c = lax.broadcasted_iota(jnp.int32, (lc, lc), 1)
    mask = c <= r
    if period < lc:
        blk = ~(period - 1)
        mask = mask & ((r & blk) == (c & blk))
    for g in range(groups):
        u = _gelu(_dot(h, wu_ref[:, g * gd:(g + 1) * gd]))
        wsm = jnp.where(mask, ws_ref[g], 0.0).astype(BF16)
        bcol = bs_ref[:, g:g + 1]
        for ci in range(tm // lc):
            rows = slice(ci * lc, (ci + 1) * lc)
            mixed = _dot(wsm, vn_scr[rows, g * gd:(g + 1) * gd]) + bcol
            out_scr[rows, g * gd:(g + 1) * gd] = (u[rows] * mixed).astype(BF16)
    o_ref[...] = x + gt_ref[...] * _dot(out_scr[...], wo_ref[...])


def _gmlp_call(x, g, shift, scale, gate, wu, wv, vg, vb, ws, bs_t, wo, *, tm, tiles_per_seq, period, emit_v):
    n, d = x.shape
    da = wu.shape[1]
    groups, lc, _ = ws.shape
    const2 = lambda i: (0, 0)
    in_specs = [
        pl.BlockSpec((tm, d), lambda i: (i, 0)),
        pl.BlockSpec((1, d), const2),
        _mod_spec(shift, tiles_per_seq), _mod_spec(scale, tiles_per_seq), _mod_spec(gate, tiles_per_seq),
        pl.BlockSpec((d, da), const2), pl.BlockSpec((d, da), const2),
        pl.BlockSpec((1, da), const2), pl.BlockSpec((1, da), const2),
        pl.BlockSpec((groups, lc, lc), lambda i: (0, 0, 0)),
        pl.BlockSpec((lc, groups), const2),
        pl.BlockSpec((da, d), const2),
    ]
    out_specs = [pl.BlockSpec((tm, d), lambda i: (i, 0))]
    out_shape = [jax.ShapeDtypeStruct((n, d), F32)]
    if emit_v:
        out_specs.append(pl.BlockSpec((tm, da), lambda i: (i, 0)))
        out_shape.append(jax.ShapeDtypeStruct((n, da), F32))
    return pl.pallas_call(
        functools.partial(_gmlp_kernel, lc=lc, period=period, groups=groups, emit_v=emit_v),
        grid=(n // tm,),
        in_specs=in_specs, out_specs=out_specs, out_shape=out_shape,
        scratch_shapes=[pltpu.VMEM((tm, da), BF16), pltpu.VMEM((tm, da), BF16)],
        compiler_params=_cparams("arbitrary"),
        name="gmlp",
    )(x, g, shift, scale, gate, wu, wv, vg, vb, ws, bs_t, wo)


def _ffn_kernel(x_ref, g_ref, sh_ref, sc_ref, gt_ref, wg_ref, wu_ref, wd_ref, o_ref, h_scr, acc_scr):
    j = pl.program_id(1)

    @pl.when(j == 0)
    def _():
        h_scr[...] = _norm_mod(x_ref[...], g_ref[...], sh_ref[...], sc_ref[...]).astype(BF16)
        acc_scr[...] = jnp.zeros_like(acc_scr)

    h = h_scr[...]
    a = _silu(_dot(h, wg_ref[...].astype(BF16))) * _dot(h, wu_ref[...].astype(BF16))
    acc_scr[...] += _dot(a.astype(BF16), wd_ref[...].astype(BF16))

    @pl.when(j == pl.num_programs(1) - 1)
    def _():
        o_ref[...] = x_ref[...] + gt_ref[...] * acc_scr[...]


def _ffn_call(x, g, shift, scale, gate, w_gu, w_down, *, tm, tiles_per_seq):
    n, d = x.shape
    f = w_down.shape[0]
    tf = min(TF_FFN, f)
    nf = f // tf
    const2 = lambda i, j: (0, 0)
    return pl.pallas_call(
        _ffn_kernel,
        grid=(n // tm, nf),
        in_specs=[
            pl.BlockSpec((tm, d), lambda i, j: (i, 0)),
            pl.BlockSpec((1, d), const2),
            _mod_spec(shift, tiles_per_seq), _mod_spec(scale, tiles_per_seq), _mod_spec(gate, tiles_per_seq),
            pl.BlockSpec((d, tf), lambda i, j: (0, j)),
            pl.BlockSpec((d, tf), lambda i, j: (0, nf + j)),
            pl.BlockSpec((tf, d), lambda i, j: (j, 0)),
        ],
        out_specs=pl.BlockSpec((tm, d), lambda i, j: (i, 0)),
        out_shape=jax.ShapeDtypeStruct((n, d), F32),
        scratch_shapes=[pltpu.VMEM((tm, d), BF16), pltpu.VMEM((tm, d), F32)],
        compiler_params=_cparams("arbitrary", "arbitrary"),
        name="ffn",
    )(x, g, shift, scale, gate, w_gu, w_gu, w_down)


def _log_sigmoid(x):
    return jnp.minimum(x, 0.0) - jnp.log(1.0 + jnp.exp(-jnp.abs(x)))


def _foxproj_kernel(x_ref, g_ref, sh_ref, sc_ref, w_ref, wf_ref, bf_ref, *rest, transposed, period,
                    tiles_per_seq, q_scale):
    if transposed:
        q_ref, k_ref, v_ref, kb_ref, vb_ref, lf_ref, cum_ref, carry_scr = rest
    else:
        q_ref, k_ref, v_ref, lf_ref, cum_ref, carry_scr = rest
    i = pl.program_id(0)
    x = x_ref[...]
    tm, d = x.shape
    h = _norm_mod(x, g_ref[...], sh_ref[...], sc_ref[...]).astype(BF16)
    q = _dot_nt(h, w_ref[0:d, :]) * q_scale
    q_ref[...] = q.astype(q_ref.dtype)
    if transposed:
        kt = _dot_nt(w_ref[d:2 * d, :], h)
        vt = _dot_nt(w_ref[2 * d:3 * d, :], h)
        k_ref[...] = kt
        v_ref[...] = vt
        kb_ref[...] = kt.astype(BF16)
        vb_ref[...] = vt.astype(BF16)
    else:
        k_ref[...] = _dot_nt(h, w_ref[d:2 * d, :])
        v_ref[...] = _dot_nt(h, w_ref[2 * d:3 * d, :])
    logf = _log_sigmoid(_dot_nt(wf_ref[...], h) + bf_ref[...])
    lf_ref[...] = logf
    s = lax.broadcasted_iota(jnp.int32, (tm, tm), 0)
    t = lax.broadcasted_iota(jnp.int32, (tm, tm), 1)
    upper = s <= t
    if period < tm:
        blk = ~(period - 1)
        upper = upper & ((s & blk) == (t & blk))
    upper = _ones_where(upper)
    hi, mid, lo = _split3(logf)
    cum = _dot(hi, upper) + _dot(mid, upper) + _dot(lo, upper)
    if tiles_per_seq > 1:
        @pl.when(i % tiles_per_seq == 0)
        def _():
            carry_scr[...] = jnp.zeros_like(carry_scr)
        cum = cum + carry_scr[:, 0:1]
        carry_scr[...] = jnp.broadcast_to(cum[:, tm - 1:tm], carry_scr.shape)
    cum_ref[...] = cum


def _foxproj_call(x, g, shift, scale, w_t, wf_t, b_f, *, tm, tiles_per_seq, transposed, period):
    n, d = x.shape
    heads = wf_t.shape[0]
    n_seq = n // (tm * tiles_per_seq)
    t_len = tm * tiles_per_seq
    const2 = lambda i: (0, 0)
    row = lambda i: (i, 0)
    seq_t = lambda i: (i // tiles_per_seq, 0, i % tiles_per_seq)
    in_specs = [
        pl.BlockSpec((tm, d), row),
        pl.BlockSpec((1, d), const2),
        _mod_spec(shift, tiles_per_seq), _mod_spec(scale, tiles_per_seq),
        pl.BlockSpec((3 * d, d), const2),
        pl.BlockSpec((heads, d), const2),
        pl.BlockSpec((heads, 1), const2),
    ]
    lf_spec = pl.BlockSpec((None, heads, tm), seq_t)
    lf_shape = jax.ShapeDtypeStruct((n_seq, heads, t_len), F32)
    if transposed:
        kv_spec = pl.BlockSpec((None, d, tm), seq_t)
        out_specs = [pl.BlockSpec((tm, d), row), kv_spec, kv_spec, kv_spec, kv_spec, lf_spec, lf_spec]
        out_shape = [jax.ShapeDtypeStruct((n, d), BF16),
                     jax.ShapeDtypeStruct((n_seq, d, t_len), F32), jax.ShapeDtypeStruct((n_seq, d, t_len), F32),
                     jax.ShapeDtypeStruct((n_seq, d, t_len), BF16), jax.ShapeDtypeStruct((n_seq, d, t_len), BF16),
                     lf_shape, lf_shape]
    else:
        out_specs = [pl.BlockSpec((tm, d), row)] * 3 + [lf_spec, lf_spec]
        out_shape = [jax.ShapeDtypeStruct((n, d), F32)] * 3 + [lf_shape, lf_shape]
    return pl.pallas_call(
        functools.partial(_foxproj_kernel, transposed=transposed, period=period, tiles_per_seq=tiles_per_seq,
                          q_scale=float(d // heads) ** -0.5),
        grid=(n // tm,),
        in_specs=in_specs, out_specs=out_specs, out_shape=out_shape,
        scratch_shapes=[pltpu.VMEM((heads, LANES), F32)],
        compiler_params=_cparams("arbitrary"),
        name="foxproj",
    )(x, g, shift, scale, w_t, wf_t, b_f)


def _attn_kernel(q_ref, kt_ref, vt_ref, cum_ref, o_ref, cq_scr, m_scr, l_scr, acc_scr, *, tq, hd):
    t_len = q_ref.shape[0]
    nq = t_len // tq
    lane = lax.broadcasted_iota(jnp.int32, (tq, LANES), 1)
    r = lax.broadcasted_iota(jnp.int32, (tq, tq), 0)
    c = lax.broadcasted_iota(jnp.int32, (tq, tq), 1)
    eye = r == c
    causal = c <= r
    m_scr[...] = jnp.full_like(m_scr, NEG)
    l_scr[...] = jnp.zeros_like(l_scr)
    acc_scr[...] = jnp.zeros_like(acc_scr)
    for hh in range(2):
        for qi in range(nq):
            row = cum_ref[hh:hh + 1, qi * tq:(qi + 1) * tq]
            cq_scr[hh, qi * tq:(qi + 1) * tq, :] = jnp.sum(jnp.where(eye, row, 0.0), axis=1, keepdims=True)

    def update(hh, qi, ki, masked):
        rows = pl.ds(pl.multiple_of(qi * tq, tq), tq)
        q = q_ref[rows, :]
        qh = jnp.where((lane >= hh * hd) & (lane < (hh + 1) * hd), q, jnp.zeros_like(q))
        s = _dot(qh, kt_ref[:, ki * tq:(ki + 1) * tq])
        s = s + cq_scr[hh, rows, :] - cum_ref[hh:hh + 1, ki * tq:(ki + 1) * tq]
        if masked:
            s = jnp.where(causal, s, NEG)
        m_old = m_scr[hh, rows, :]
        m_new = jnp.maximum(m_old, jnp.max(s, axis=1, keepdims=True))
        alpha = jnp.exp(m_old - m_new)
        p = jnp.exp(s - m_new)
        l_scr[hh, rows, :] = alpha * l_scr[hh, rows, :] + jnp.sum(p, axis=1, keepdims=True)
        pv = _dot_nt(p.astype(BF16), vt_ref[:, ki * tq:(ki + 1) * tq])
        acc_scr[hh, rows, :] = alpha * acc_scr[hh, rows, :] + pv
        m_scr[hh, rows, :] = m_new

    for ki in range(nq):
        for hh in range(2):
            update(hh, ki, ki, True)
            if ki + 1 < nq:
                def body(qi, carry, hh=hh, ki=ki):
                    update(hh, qi, ki, False)
                    return carry
                lax.fori_loop(ki + 1, nq, body, 0)

    for qi in range(nq):
        rows = slice(qi * tq, (qi + 1) * tq)
        o0 = acc_scr[0, rows, :] / l_scr[0, rows, :]
        o1 = acc_scr[1, rows, :] / l_scr[1, rows, :]
        o_ref[rows, :] = jnp.where(lane < hd, o0, o1).astype(o_ref.dtype)


def _attn_call(q, kt, vt, cum4, *, t_len, heads):
    n, d = q.shape
    n_seq = n // t_len
    hd = d // heads
    assert 2 * hd == LANES
    tq = min(TQ_ATTN, t_len)
    kv_spec = pl.BlockSpec((None, LANES, t_len), lambda b, p: (b, p, 0))
    return pl.pallas_call(
        functools.partial(_attn_kernel, tq=tq, hd=hd),
        grid=(n_seq, heads // 2),
        in_specs=[
            pl.BlockSpec((t_len, LANES), lambda b, p: (b, p)),
            kv_spec, kv_spec,
            pl.BlockSpec((None, None, 2, t_len), lambda b, p: (b, p, 0, 0)),
        ],
        out_specs=pl.BlockSpec((t_len, LANES), lambda b, p: (b, p)),
        out_shape=jax.ShapeDtypeStruct((n, d), BF16),
        scratch_shapes=[pltpu.VMEM((2, t_len, 1), F32), pltpu.VMEM((2, t_len, 1), F32),
                        pltpu.VMEM((2, t_len, 1), F32), pltpu.VMEM((2, t_len, LANES), F32)],
        compiler_params=_cparams("arbitrary", "arbitrary"),
        name="attn_prompt",
    )(q, kt, vt, cum4)


def _attn_sample_kernel(pt_ref, q_ref, cn_ref, kn_ref, vn_ref, *rest, pages, hd, page_size):
    k_refs = rest[0:pages]
    v_refs = rest[pages:2 * pages]
    lf_refs = rest[2 * pages:3 * pages]
    o_ref, qbd_scr, cn_scr, m_scr, l_scr, acc_scr, suf_scr = rest[3 * pages:]
    j = pl.program_id(1)
    tn, d = q_ref.shape
    heads = d // hd
    rows = heads * tn
    row = lax.broadcasted_iota(jnp.int32, (rows, LANES), 0)
    lane = lax.broadcasted_iota(jnp.int32, (rows, LANES), 1)
    q_of_row = row & (tn - 1)

    def rep(a):
        return jnp.broadcast_to(a[:, None, :], (heads, tn, a.shape[-1])).reshape(rows, a.shape[-1])

    def update(s, v_bf, nt):
        m_old = m_scr[...]
        m_new = jnp.maximum(m_old, jnp.max(s, axis=1, keepdims=True))
        alpha = jnp.exp(m_old - m_new)
        p = jnp.exp(s - m_new)
        l_scr[...] = alpha * l_scr[...] + jnp.sum(p, axis=1, keepdims=True)
        pb = p.astype(BF16)
        pv = _dot_nt(pb, v_bf) if nt else _dot(pb, v_bf)
        acc_scr[...] = alpha * acc_scr[...] + pv
        m_scr[...] = m_new

    @pl.when(j == 0)
    def _():
        q = q_ref[...]
        qrep = jnp.broadcast_to(q[None, :, :], (heads, tn, d)).reshape(rows, d)
        rr = lax.broadcasted_iota(jnp.int32, (rows, d), 0)
        cc = lax.broadcasted_iota(jnp.int32, (rows, d), 1)
        qbd_scr[...] = jnp.where(_shift_div(rr, tn) == _shift_div(cc, hd), qrep, 0.0).astype(BF16)
        cn_scr[...] = jnp.sum(jnp.where(lane == q_of_row, rep(cn_ref[...]), 0.0), axis=1, keepdims=True)
        m_scr[...] = jnp.full_like(m_scr, NEG)
        l_scr[...] = jnp.zeros_like(l_scr)
        acc_scr[...] = jnp.zeros_like(acc_scr)
        suf_scr[...] = jnp.zeros_like(suf_scr)

    s_idx = lax.broadcasted_iota(jnp.int32, (page_size, page_size), 0)
    t_idx = lax.broadcasted_iota(jnp.int32, (page_size, page_size), 1)
    later = _ones_where(s_idx > t_idx)
    for i in reversed(range(pages)):
        lf = lf_refs[i][...]
        hi, mid, lo = _split3(lf)
        suf = _dot(hi, later) + _dot(mid, later) + _dot(lo, later) + suf_scr[:, 0:1]
        suf_scr[...] = jnp.broadcast_to(suf[:, 0:1] + lf[:, 0:1], suf_scr.shape)
        s = _dot(qbd_scr[...], k_refs[i][...].astype(BF16)) + cn_scr[...] + rep(suf)
        update(s, v_refs[i][...].astype(BF16), True)

    @pl.when(j == pl.num_programs(1) - 1)
    def _():
        pad = jnp.zeros((LANES - tn, d), BF16)
        k_new = jnp.concatenate([kn_ref[...].astype(BF16), pad], axis=0)
        v_new = jnp.concatenate([vn_ref[...].astype(BF16), pad], axis=0)
        s = _dot_nt(qbd_scr[...], k_new) + cn_scr[...] - rep(cn_ref[...])
        s = jnp.where(lane <= q_of_row, s, NEG)
        update(s, v_new, False)
        o = acc_scr[...] / l_scr[...]
        rr = lax.broadcasted_iota(jnp.int32, (rows, d), 0)
        cc = lax.broadcasted_iota(jnp.int32, (rows, d), 1)
        o = jnp.where(_shift_div(rr, tn) == _shift_div(cc, hd), o, 0.0)
        o_ref[...] = jnp.sum(o.reshape(heads, tn, d), axis=0)


def _attn_sample_call(page_table, q, cn_pad, k_new, v_new, kc_t, vc_t, lfc_t, *, tn, heads):
    n, d = q.shape
    n_seq, n_pages = page_table.shape
    page_size = kc_t.shape[-1]
    pages = min(PAGES_PER_STEP, n_pages)
    n_steps = n_pages // pages
    hd = d // heads

    def page_map(i):
        def index(b, j, pt):
            return (pt[b * n_pages + (n_steps - 1 - j) * pages + i], 0, 0)
        return index

    row = lambda b, j, pt: (b, 0)
    in_specs = [
        pl.BlockSpec((tn, d), row),
        pl.BlockSpec((None, heads, LANES), lambda b, j, pt: (b, 0, 0)),
        pl.BlockSpec((tn, d), row), pl.BlockSpec((tn, d), row),
    ]
    in_specs += [pl.BlockSpec((None, d, page_size), page_map(i)) for i in range(pages)]
    in_specs += [pl.BlockSpec((None, d, page_size), page_map(i)) for i in range(pages)]
    in_specs += [pl.BlockSpec((None, heads, page_size), page_map(i)) for i in range(pages)]
    rows = heads * tn
    return pl.pallas_call(
        functools.partial(_attn_sample_kernel, pages=pages, hd=hd, page_size=page_size),
        grid_spec=pltpu.PrefetchScalarGridSpec(
            num_scalar_prefetch=1,
            grid=(n_seq, n_steps),
            in_specs=in_specs,
            out_specs=pl.BlockSpec((tn, d), row),
            scratch_shapes=[pltpu.VMEM((rows, d), BF16), pltpu.VMEM((rows, 1), F32), pltpu.VMEM((rows, 1), F32),
                            pltpu.VMEM((rows, 1), F32), pltpu.VMEM((rows, d), F32),
                            pltpu.VMEM((heads, LANES), F32)],
        ),
        out_shape=jax.ShapeDtypeStruct((n, d), F32),
        compiler_params=_cparams("arbitrary", "arbitrary"),
        name="attn_sample",
    )(page_table.reshape(-1), q, cn_pad, k_new, v_new, *([kc_t] * pages), *([vc_t] * pages), *([lfc_t] * pages))


def _post_kernel(o_ref, wo_ref, x_ref, gt_ref, g_ref, sh_ref, sc_ref, wr_ref, br_ref, cin_ref,
                 x3_ref, h_ref, meta_ref, blk_ref, cout_ref, carry_scr, *, n_experts):
    i = pl.program_id(0)
    tm = x_ref.shape[0]

    @pl.when(i == 0)
    def _():
        carry_scr[...] = jnp.broadcast_to(cin_ref[...], carry_scr.shape)

    x3 = x_ref[...] + gt_ref[...] * _dot(o_ref[...].astype(BF16), wo_ref[...])
    x3_ref[...] = x3
    h = _norm_mod(x3, g_ref[...], sh_ref[...], sc_ref[...])
    h_hi = h.astype(BF16)
    h_ref[...] = h_hi
    h_lo = (h - h_hi.astype(F32)).astype(BF16)
    wr = wr_ref[...]
    w_hi = wr.astype(BF16)
    w_lo = (wr - w_hi.astype(F32)).astype(BF16)
    logits = _dot(h_hi, w_hi) + _dot(h_hi, w_lo) + _dot(h_lo, w_hi) + br_ref[...]
    lane_i = lax.broadcasted_iota(jnp.int32, (tm, LANES), 1)
    lane = lane_i.astype(F32)
    logits = jnp.where(lane_i < n_experts, logits, NEG)
    l1 = jnp.max(logits, axis=1, keepdims=True)
    i1 = jnp.min(jnp.where(logits == l1, lane, float(LANES)), axis=1, keepdims=True)
    rest = jnp.where(lane == i1, NEG, logits)
    l2 = jnp.max(rest, axis=1, keepdims=True)
    i2 = jnp.min(jnp.where(rest == l2, lane, float(LANES)), axis=1, keepdims=True)
    e = jnp.exp(l2 - l1)
    g1 = 1.0 / (1.0 + e)
    g2 = e / (1.0 + e)
    onehot = jnp.where((lane == i1) | (lane == i2), 1.0, 0.0)
    r = lax.broadcasted_iota(jnp.int32, (tm, tm), 0)
    c = lax.broadcasted_iota(jnp.int32, (tm, tm), 1)
    before = _ones_where(c < r)
    carry = carry_scr[0:1, :]
    blk_ref[...] = carry
    prefix = _dot(before, onehot.astype(BF16)) + carry
    r1 = jnp.sum(jnp.where(lane == i1, prefix, 0.0), axis=1, keepdims=True)
    r2 = jnp.sum(jnp.where(lane == i2, prefix, 0.0), axis=1, keepdims=True)
    cols = (i1, i2, g1, g2, r1, r2)
    meta = jnp.zeros((tm, LANES), F32)
    for k, col in enumerate(cols):
        meta = jnp.where(lane_i == k, col, meta)
    meta_ref[...] = meta
    carry = carry + jnp.sum(onehot, axis=0, keepdims=True)
    carry_scr[...] = jnp.broadcast_to(carry, carry_scr.shape)
    cout_ref[...] = carry


def _post_call(o, wo, x, gate, g, shift, scale, wr_pad, br_pad, counts_in, *, tm, tiles_per_seq, n_experts):
    n, d = x.shape
    nb = n // tm
    const2 = lambda i: (0, 0)
    row = lambda i: (i, 0)
    return pl.pallas_call(
        functools.partial(_post_kernel, n_experts=n_experts),
        grid=(nb,),
        in_specs=[
            pl.BlockSpec((tm, d), row),
            pl.BlockSpec((d, d), const2),
            pl.BlockSpec((tm, d), row),
            _mod_spec(gate, tiles_per_seq),
            pl.BlockSpec((1, d), const2),
            _mod_spec(shift, tiles_per_seq), _mod_spec(scale, tiles_per_seq),
            pl.BlockSpec((d, LANES), const2),
            pl.BlockSpec((1, LANES), const2),
            pl.BlockSpec((1, LANES), const2),
        ],
        out_specs=[
            pl.BlockSpec((tm, d), row), pl.BlockSpec((tm, d), row), pl.BlockSpec((tm, LANES), row),
            pl.BlockSpec((None, 1, LANES), lambda i: (i, 0, 0)),
            pl.BlockSpec((1, LANES), const2),
        ],
        out_shape=[
            jax.ShapeDtypeStruct((n, d), F32), jax.ShapeDtypeStruct((n, d), BF16),
            jax.ShapeDtypeStruct((n, LANES), F32),
            jax.ShapeDtypeStruct((nb, 1, LANES), F32),
            jax.ShapeDtypeStruct((1, LANES), F32),
        ],
        scratch_shapes=[pltpu.VMEM((SUBLANES, LANES), F32)],
        compiler_params=_cparams("arbitrary"),
        name="post_attn_router",
    )(o, wo, x, gate, g, shift, scale, wr_pad, br_pad, counts_in)


def _dispatch_kernel(lo_ref, hi_ref, pos_ref, gsrc_ref, hp_ref, hs_ref, xs_ref, gs_ref, acc_scr, gacc_scr,
                     *, n_prompt_blocks, tb):
    j = pl.program_id(0)
    sub = xs_ref.shape[0]
    lo = lo_ref[j]
    hi = hi_ref[j]
    acc_scr[...] = jnp.zeros_like(acc_scr)
    gacc_scr[...] = jnp.zeros_like(gacc_scr)
    dest = j * sub + lax.broadcasted_iota(jnp.int32, (sub, tb), 0)

    def body(b, carry):
        pos = pos_ref[b]
        oh0 = _ones_where(pos[0:1, :] == dest)
        oh1 = _ones_where(pos[1:2, :] == dest)
        oh = oh0 + oh1

        @pl.when(b < n_prompt_blocks)
        def _():
            rows = pl.ds(pl.multiple_of(b * tb, tb), tb)
            acc_scr[...] += _dot(oh, hp_ref[rows, :])

        @pl.when(b >= n_prompt_blocks)
        def _():
            rows = pl.ds(pl.multiple_of((b - n_prompt_blocks) * tb, tb), tb)
            acc_scr[...] += _dot(oh, hs_ref[rows, :])

        gsrc = gsrc_ref[b]
        gacc_scr[...] += _dot(oh0, gsrc[0]) + _dot(oh1, gsrc[1])
        return carry

    lax.fori_loop(lo, hi, body, 0)
    xs_ref[...] = acc_scr[...].astype(BF16)
    gs_ref[...] = jnp.broadcast_to(jnp.sum(gacc_scr[...], axis=1, keepdims=True), gs_ref.shape)


def _dispatch_call(tile_lo, tile_hi, pos3, gsrc, h_p, h_s, *, n_sorted, tb):
    n_p, d = h_p.shape
    n_s = h_s.shape[0]
    nb = pos3.shape[0]
    whole = lambda nd: (lambda j, lo, hi: (0,) * nd)
    once = pl.Buffered(1)
    return pl.pallas_call(
        functools.partial(_dispatch_kernel, n_prompt_blocks=n_p // tb, tb=tb),
        grid_spec=pltpu.PrefetchScalarGridSpec(
            num_scalar_prefetch=2,
            grid=(n_sorted // SUB,),
            in_specs=[
                pl.BlockSpec((nb, 2, tb), whole(3), pipeline_mode=once),
                pl.BlockSpec((nb, 2, tb, LANES), whole(4), pipeline_mode=once),
                pl.BlockSpec((n_p, d), whole(2), pipeline_mode=once),
                pl.BlockSpec((n_s, d), whole(2), pipeline_mode=once),
            ],
            out_specs=[pl.BlockSpec((SUB, d), lambda j, lo, hi: (j, 0)),
                       pl.BlockSpec((SUB, LANES), lambda j, lo, hi: (j, 0))],
            scratch_shapes=[pltpu.VMEM((SUB, d), F32), pltpu.VMEM((SUB, LANES), F32)],
        ),
        out_shape=[jax.ShapeDtypeStruct((n_sorted, d), BF16), jax.ShapeDtypeStruct((n_sorted, LANES), F32)],
        compiler_params=_cparams("arbitrary"),
        name="moe_dispatch",
    )(tile_lo, tile_hi, pos3, gsrc, h_p, h_s)


def _moe_kernel(ex_ref, nsub_ref, x_ref, gs_ref, wg_ref, wu_ref, wd_ref, y_ref, acc_scr, wgb_scr, wub_scr, wdb_scr):
    s = pl.program_id(0)
    j = pl.program_id(1)
    nsub = nsub_ref[s]
    last = j == pl.num_programs(1) - 1

    @pl.when(nsub > 0)
    def _():
        wgb_scr[...] = wg_ref[...].astype(BF16)
        wub_scr[...] = wu_ref[...].astype(BF16)
        wdb_scr[...] = wd_ref[...].astype(BF16)

    def body(t, carry):
        rows = pl.ds(pl.multiple_of(t * SUB, SUB), SUB)
        x = x_ref[rows, :]
        a = _silu(_dot(x, wgb_scr[...])) * _dot(x, wub_scr[...])
        part = _dot(a.astype(BF16), wdb_scr[...])

        @pl.when(j == 0)
        def _():
            acc_scr[rows, :] = part

        @pl.when(j > 0)
        def _():
            acc_scr[rows, :] += part

        @pl.when(last)
        def _():
            y_ref[rows, :] = (acc_scr[rows, :] * gs_ref[rows, 0:1]).astype(BF16)
        return carry

    lax.fori_loop(0, nsub, body, 0)

    @pl.when(last)
    def _():
        def zero(t, carry):
            rows = pl.ds(pl.multiple_of(t * SUB, SUB), SUB)
            y_ref[rows, :] = jnp.zeros((SUB, y_ref.shape[1]), BF16)
            return carry
        lax.fori_loop(nsub, y_ref.shape[0] // SUB, zero, 0)


def _moe_call(sup_expert, sup_nsub, xs, gs, w_gu, w_down, *, ts):
    n_sorted, d = xs.shape
    f = w_down.shape[1]
    tf = min(TF_FFN, f)
    nf = f // tf
    n_super = n_sorted // ts

    def jj(s, j, nsub):
        return jnp.where(nsub[s] > 0, j, nf - 1)

    once = pl.Buffered(1)
    return pl.pallas_call(
        _moe_kernel,
        grid_spec=pltpu.PrefetchScalarGridSpec(
            num_scalar_prefetch=2,
            grid=(n_super, nf),
            in_specs=[
                pl.BlockSpec((ts, d), lambda s, j, ex, ns: (s, 0), pipeline_mode=once),
                pl.BlockSpec((ts, LANES), lambda s, j, ex, ns: (s, 0), pipeline_mode=once),
                pl.BlockSpec((None, d, tf), lambda s, j, ex, ns: (ex[s], 0, jj(s, j, ns))),
                pl.BlockSpec((None, d, tf), lambda s, j, ex, ns: (ex[s], 0, nf + jj(s, j, ns))),
                pl.BlockSpec((None, tf, d), lambda s, j, ex, ns: (ex[s], jj(s, j, ns), 0)),
            ],
            out_specs=pl.BlockSpec((ts, d), lambda s, j, ex, ns: (s, 0)),
            scratch_shapes=[pltpu.VMEM((ts, d), F32), pltpu.VMEM((d, tf), BF16), pltpu.VMEM((d, tf), BF16),
                            pltpu.VMEM((tf, d), BF16)],
        ),
        out_shape=jax.ShapeDtypeStruct((n_sorted, d), BF16),
        compiler_params=_cparams("arbitrary", "arbitrary"),
        name="moe_ffn",
    )(sup_expert, sup_nsub, xs, gs, w_gu, w_gu, w_down)


def _combine_kernel(tile_ref, need2_ref, pos_ref, xp_ref, xs_ref, gtp_ref, gts_ref, gf_ref, *rest,
                    n_experts, n_prompt_blocks):
    y_refs = rest[0:2 * n_experts]
    op_ref, os_ref, acc_scr = rest[2 * n_experts:]
    b = pl.program_id(0)
    tb = acc_scr.shape[0]
    pos = pos_ref[...]
    col = lax.broadcasted_iota(jnp.int32, (tb, SUB), 1)
    acc_scr[...] = jnp.zeros_like(acc_scr)
    for e in range(n_experts):
        base = tile_ref[b * n_experts + e] * SUB
        p0 = jnp.where(pos[:, 2:3] == e, pos[:, 0:1], -1)
        p1 = jnp.where(pos[:, 3:4] == e, pos[:, 1:2], -1)
        for w in range(2):
            def add(e=e, w=w, base=base, p0=p0, p1=p1):
                src = base + w * SUB + col
                oh = _ones_where((p0 == src) | (p1 == src))
                acc_scr[...] += _dot(oh, y_refs[2 * e + w][...])
            if w == 0:
                add()
            else:
                pl.when(need2_ref[b * n_experts + e] > 0)(add)

    def finish(x, gate, o_ref):
        x4 = x + gate * acc_scr[...]
        ms = jnp.mean(x4 * x4, axis=-1, keepdims=True)
        o_ref[...] = x4 * lax.rsqrt(ms + EPS) * gf_ref[...]

    @pl.when(b < n_prompt_blocks)
    def _():
        finish(xp_ref[...], gtp_ref[...], op_ref)

    @pl.when(b >= n_prompt_blocks)
    def _():
        finish(xs_ref[...], gts_ref[...], os_ref)


def _combine_call(win_tile, win_need2, pos_rows, x_p, x_s, gate_p, gate_s, g_final, ys, *, tb, blocks_per_seq,
                  n_experts):
    n_p, d = x_p.shape
    n_s = x_s.shape[0]
    npb = n_p // tb
    nb = npb + n_s // tb

    def pb(b):
        return jnp.minimum(b, npb - 1)

    def sb(b):
        return jnp.maximum(b - npb, 0)

    def win_map(e, w):
        return lambda b, tile, need: (tile[b * n_experts + e] + w, 0)

    in_specs = [
        pl.BlockSpec((tb, LANES), lambda b, tile, need: (b, 0)),
        pl.BlockSpec((tb, d), lambda b, tile, need: (pb(b), 0)),
        pl.BlockSpec((tb, d), lambda b, tile, need: (sb(b), 0)),
        pl.BlockSpec((None, 1, d), lambda b, tile, need: (pb(b) // blocks_per_seq, 0, 0)),
        pl.BlockSpec((None, tb, d), lambda b, tile, need: (sb(b), 0, 0)),
        pl.BlockSpec((1, d), lambda b, tile, need: (0, 0)),
    ]
    in_specs += [pl.BlockSpec((SUB, d), win_map(e, w)) for e in range(n_experts) for w in range(2)]
    return pl.pallas_call(
        functools.partial(_combine_kernel, n_experts=n_experts, n_prompt_blocks=npb),
        grid_spec=pltpu.PrefetchScalarGridSpec(
            num_scalar_prefetch=2,
            grid=(nb,),
            in_specs=in_specs,
            out_specs=[pl.BlockSpec((tb, d), lambda b, tile, need: (pb(b), 0)),
                       pl.BlockSpec((tb, d), lambda b, tile, need: (sb(b), 0))],
            scratch_shapes=[pltpu.VMEM((tb, d), F32)],
        ),
        out_shape=[jax.ShapeDtypeStruct((n_p, d), F32), jax.ShapeDtypeStruct((n_s, d), F32)],
        compiler_params=_cparams("arbitrary"),
        name="moe_combine",
    )(win_tile, win_need2, pos_rows, x_p, x_s, gate_p, gate_s, g_final, *([ys] * (2 * n_experts)))


def _routing_tables(meta, blk_cnt, counts, *, n_experts, ts, tb, n_super):
    n = meta.shape[0]
    nb = n // tb
    idx = meta[:, 0:TOP_K].astype(jnp.int32)
    gates = meta[:, 2:2 + TOP_K]
    rank = meta[:, 4:4 + TOP_K].astype(jnp.int32)
    cnt = counts[0, :n_experts].astype(jnp.int32)
    n_sup_e = (cnt + ts - 1) // ts
    sup_start = jnp.cumsum(n_sup_e) - n_sup_e
    start = sup_start * ts
    pos = start[idx] + rank

    s_ids = jnp.arange(n_super, dtype=jnp.int32)
    used = jnp.sum(n_sup_e)
    sup_e = jnp.clip(jnp.searchsorted(sup_start + n_sup_e, s_ids, side="right"), 0, n_experts - 1).astype(jnp.int32)
    last_e = jnp.max(jnp.where(cnt > 0, jnp.arange(n_experts, dtype=jnp.int32), 0))
    sup_e = jnp.where(s_ids < used, sup_e, last_e)
    rows_in = jnp.clip(cnt[sup_e] - (s_ids - sup_start[sup_e]) * ts, 0, ts)
    sup_nsub = jnp.where(s_ids < used, (rows_in + SUB - 1) // SUB, 0).astype(jnp.int32)

    blk = blk_cnt[:, 0, :n_experts].astype(jnp.int32)
    blk_end = jnp.concatenate([blk[1:], cnt[None, :]], axis=0)
    n_tiles = n_super * (ts // SUB)
    t_ids = jnp.arange(n_tiles, dtype=jnp.int32)
    t_e = sup_e[t_ids // (ts // SUB)]
    t_r0 = t_ids * SUB - start[t_e]
    t_active = ((t_ids // (ts // SUB)) < used) & (t_r0 < cnt[t_e]) & (t_r0 >= 0)
    t_r1 = jnp.minimum(t_r0 + SUB, cnt[t_e])
    be = blk_end[:, t_e]
    bs = blk[:, t_e]
    overlap = (be > t_r0[None, :]) & (bs < t_r1[None, :])
    b_ids = jnp.arange(nb, dtype=jnp.int32)[:, None]
    lo = jnp.min(jnp.where(overlap, b_ids, nb), axis=0)
    hi = jnp.max(jnp.where(overlap, b_ids + 1, 0), axis=0)
    tile_lo = jnp.where(t_active, lo, 0).astype(jnp.int32)
    tile_hi = jnp.where(t_active, hi, 0).astype(jnp.int32)

    first = start[None, :] + blk
    n_be = blk_end - blk
    win_tile = (first // SUB).astype(jnp.int32)
    win_need2 = ((first + n_be) > (win_tile + 1) * SUB).astype(jnp.int32)

    pos3 = pos.reshape(nb, tb, TOP_K).transpose(0, 2, 1)
    pos_rows = jnp.zeros((n, LANES), jnp.int32).at[:, 0:TOP_K].set(pos).at[:, TOP_K:2 * TOP_K].set(idx)
    g_hi = gates.astype(BF16)
    g_r = gates - g_hi.astype(F32)
    g_mid = g_r.astype(BF16)
    g_lo = (g_r - g_mid.astype(F32)).astype(BF16)
    terms = jnp.stack([g_hi, g_mid, g_lo], axis=-1)
    gsrc = jnp.zeros((n, TOP_K, LANES), BF16).at[:, :, 0:3].set(terms)
    gsrc = gsrc.reshape(nb, tb, TOP_K, LANES).transpose(0, 2, 1, 3)
    return dict(pos3=pos3, pos_rows=pos_rows, gsrc=gsrc, sup_e=sup_e, sup_nsub=sup_nsub, tile_lo=tile_lo,
                tile_hi=tile_hi, win_tile=win_tile.reshape(-1), win_need2=win_need2.reshape(-1))


def kernel(x_prompt, x_sample, c_prompt, c_sample, cache_k, cache_v, cache_logf, page_table, norm_mix_g, norm_ffn_g, final_norm_g, ada_w, ada_b, gmlp_w_in, gmlp_v_g, gmlp_v_b, gmlp_w_s, gmlp_b_s, gmlp_w_out, fox_w_in, fox_b_f, fox_w_out, ffn_w_gu, ffn_w_down, moe_w_r, moe_b_r, moe_w_gu, moe_w_down):
    n_seq_p, t_len, d = x_prompt.shape
    n_seq_s, t_new, _ = x_sample.shape
    n_p = n_seq_p * t_len
    n_s = n_seq_s * t_new
    heads = N_HEADS
    hd = d // heads
    n_experts = moe_w_r.shape[-1]
    da = gmlp_w_out.shape[1]

    xp = x_prompt.reshape(n_p, d)
    xs = x_sample.reshape(n_s, d)

    mod = _ada_call(jnp.concatenate([c_prompt, c_sample], axis=0), ada_w, ada_b)

    def mods(layer):
        mp = [mod[layer, :n_seq_p, c * d:(c + 1) * d].reshape(n_seq_p, 1, d) for c in range(6)]
        ms = [jnp.repeat(mod[layer, n_seq_p:, c * d:(c + 1) * d], t_new, axis=0).reshape(1, n_s, d)
              for c in range(6)]
        return mp, ms

    row = lambda a: a.reshape(1, -1)

    mp, ms = mods(0)
    w_in = gmlp_w_in[0].astype(BF16)
    wu, wv = w_in[:, :da], w_in[:, da:]
    wo = gmlp_w_out[0].astype(BF16)
    vg, vb = row(gmlp_v_g[0]), row(gmlp_v_b[0])
    lc = min(GMLP_CHUNK, t_len)
    tm = min(TM_GMLP, t_len)
    xp = _gmlp_call(xp, row(norm_mix_g[0]), mp[0], mp[1], mp[2], wu, wv, vg, vb,
                    gmlp_w_s[0][:, :lc, :lc], gmlp_b_s[0][:, :lc].T, wo,
                    tm=tm, tiles_per_seq=t_len // tm, period=lc, emit_v=False)[0]
    reps = n_s // t_new
    ws_s = jnp.tile(gmlp_w_s[0][:, :t_new, :t_new], (1, reps, reps))
    bs_s = jnp.tile(gmlp_b_s[0][:, :t_new], (1, reps)).T
    xs, v_rows = _gmlp_call(xs, row(norm_mix_g[0]), ms[0], ms[1], ms[2], wu, wv, vg, vb, ws_s, bs_s, wo,
                            tm=n_s, tiles_per_seq=1, period=t_new, emit_v=True)
    tm = min(TM_FFN, t_len)
    xp = _ffn_call(xp, row(norm_ffn_g[0]), mp[3], mp[4], mp[5], ffn_w_gu[0], ffn_w_down[0],
                   tm=tm, tiles_per_seq=t_len // tm)
    xs = _ffn_call(xs, row(norm_ffn_g[0]), ms[3], ms[4], ms[5], ffn_w_gu[0], ffn_w_down[0],
                   tm=n_s, tiles_per_seq=1)

    mp, ms = mods(1)
    w_t = fox_w_in[0].T.astype(BF16)
    wqkv_t, wf_t = w_t[:3 * d], w_t[3 * d:]
    b_f = fox_b_f[0].reshape(heads, 1)
    tm = min(TM_PROJ, t_len)
    q_p, kt_p, vt_p, ktb_p, vtb_p, lft_p, cumt_p = _foxproj_call(
        xp, row(norm_mix_g[1]), mp[0], mp[1], wqkv_t, wf_t, b_f,
        tm=tm, tiles_per_seq=t_len // tm, transposed=True, period=tm)
    q_s, k_s, v_s, lft_s, cumt_s = _foxproj_call(
        xs, row(norm_mix_g[1]), ms[0], ms[1], wqkv_t, wf_t, b_f,
        tm=n_s, tiles_per_seq=1, transposed=False, period=t_new)

    o_p = _attn_call(q_p, ktb_p, vtb_p, cumt_p.reshape(n_seq_p, heads // 2, 2, t_len), t_len=t_len, heads=heads)

    page_size = cache_k.shape[2]
    n_phys = cache_k.shape[1]
    kc_t = jnp.transpose(cache_k[0], (0, 2, 3, 1)).reshape(n_phys, d, page_size)
    vc_t = jnp.transpose(cache_v[0], (0, 2, 3, 1)).reshape(n_phys, d, page_size)
    lfc_t = jnp.transpose(cache_logf[0], (0, 2, 1))
    cn = cumt_s[0].reshape(heads, n_seq_s, t_new).transpose(1, 0, 2)
    cn_pad = jnp.zeros((n_seq_s, heads, LANES), F32).at[:, :, :t_new].set(cn)
    o_s = _attn_sample_call(page_table, q_s, cn_pad, k_s, v_s, kc_t, vc_t, lfc_t, tn=t_new, heads=heads)

    wo_f = fox_w_out[0].astype(BF16)
    wr_pad = jnp.zeros((d, LANES), F32).at[:, :n_experts].set(moe_w_r[0])
    br_pad = jnp.zeros((1, LANES), F32).at[0, :n_experts].set(moe_b_r[0])
    tb = min(TM_POST, t_len)
    x3_p, h_p, meta_p, blk_p, counts = _post_call(
        o_p, wo_f, xp, mp[2], row(norm_ffn_g[1]), mp[3], mp[4], wr_pad, br_pad, jnp.zeros((1, LANES), F32),
        tm=tb, tiles_per_seq=t_len // tb, n_experts=n_experts)
    x3_s, h_s, meta_s, blk_s, counts = _post_call(
        o_s, wo_f, xs, ms[2], row(norm_ffn_g[1]), ms[3], ms[4], wr_pad, br_pad, counts,
        tm=tb, tiles_per_seq=n_s // tb, n_experts=n_experts)

    n_tok = n_p + n_s
    ts = min(TS_MOE, ((TOP_K * n_tok + SUB - 1) // SUB) * SUB)
    n_super = (TOP_K * n_tok) // ts + n_experts + 1
    rt = _routing_tables(jnp.concatenate([meta_p, meta_s], axis=0), jnp.concatenate([blk_p, blk_s], axis=0),
                         counts, n_experts=n_experts, ts=ts, tb=tb, n_super=n_super)
    xs_sorted, g_sorted = _dispatch_call(rt["tile_lo"], rt["tile_hi"], rt["pos3"], rt["gsrc"], h_p, h_s,
                                         n_sorted=n_super * ts, tb=tb)
    ys_sorted = _moe_call(rt["sup_e"], rt["sup_nsub"], xs_sorted, g_sorted, moe_w_gu[0], moe_w_down[0], ts=ts)
    y_p, y_s = _combine_call(rt["win_tile"], rt["win_need2"], rt["pos_rows"], x3_p, x3_s, mp[5],
                             ms[5].reshape(n_s // tb, tb, d), row(final_norm_g), ys_sorted,
                             tb=tb, blocks_per_seq=t_len // tb, n_experts=n_experts)

    y_prompt = y_p.reshape(n_seq_p, t_len, d)
    y_sample = y_s.reshape(n_seq_s, t_new, d)
    state_a_v_sample = v_rows.reshape(1, n_seq_s, t_new, da)
    k_prompt = kt_p.reshape(1, n_seq_p, heads, hd, t_len).transpose(0, 1, 4, 2, 3)
    v_prompt = vt_p.reshape(1, n_seq_p, heads, hd, t_len).transpose(0, 1, 4, 2, 3)
    logf_prompt = lft_p.transpose(0, 2, 1)[None]
    k_sample = k_s.reshape(1, n_seq_s, t_new, heads, hd)
    v_sample = v_s.reshape(1, n_seq_s, t_new, heads, hd)
    logf_sample = lft_s[0].T.reshape(1, n_seq_s, t_new, heads)
    return (y_prompt, y_sample, state_a_v_sample, k_prompt, v_prompt, logf_prompt, k_sample, v_sample, logf_sample)
```

```python
import functools

import jax
import jax.numpy as jnp
from jax import lax
from jax.experimental import pallas as pl
from jax.experimental.pallas import tpu as pltpu

F32 = jnp.float32
BF16 = jnp.bfloat16

N_HEADS = 16
GMLP_GROUPS = 8
GMLP_CHUNK = 128
TOP_K = 2
EPS = 1e-6
NEG = -1e30

LANES = 128
SUBLANES = 8
VMEM_LIMIT = 56 * 1024 * 1024

TM_GMLP = 512
TM_FFN = 1024
TF_FFN = 512
TM_PROJ = 512
TQ_ATTN = 256
TM_POST = 256
SUB = 256
FFN_ROWS = 2 * SUB
TS_MOE = 2048
PAGES_PER_STEP = 16


def _cparams(*sem):
    return pltpu.CompilerParams(dimension_semantics=sem, vmem_limit_bytes=VMEM_LIMIT)


def _dot(a, b):
    return jnp.dot(a, b, preferred_element_type=F32)


def _dot_nt(a, b):
    return lax.dot_general(a, b, (((1,), (1,)), ((), ())), preferred_element_type=F32)


def _norm_mod(x, g, shift, scale):
    ms = jnp.mean(x * x, axis=-1, keepdims=True)
    y = x * lax.rsqrt(ms + EPS) * g
    return y * (1.0 + scale) + shift


def _gelu(x):
    return 0.5 * x * (1.0 + lax.erf(x * (2.0 ** -0.5)))


def _silu(x):
    return x * jax.nn.sigmoid(x)


def _ones_where(cond):
    return jnp.where(cond, 1.0, 0.0).astype(BF16)


def _shift_div(x, c):
    assert c & (c - 1) == 0
    return lax.shift_right_logical(x, c.bit_length() - 1)


def _split3(x):
    hi = x.astype(BF16)
    r = x - hi.astype(F32)
    mid = r.astype(BF16)
    lo = (r - mid.astype(F32)).astype(BF16)
    return hi, mid, lo


def _ada_kernel(c_ref, w_ref, b_ref, o_ref):
    s = _silu(c_ref[...]).astype(BF16)
    o_ref[...] = _dot(s, w_ref[...].astype(BF16)) + b_ref[...]


def _ada_call(c_all, ada_w, ada_b):
    n_layers, d, d6 = ada_w.shape
    r = c_all.shape[0]
    tn = min(d6, 1536)
    return pl.pallas_call(
        _ada_kernel,
        grid=(n_layers, d6 // tn),
        in_specs=[
            pl.BlockSpec((r, d), lambda l, j: (0, 0)),
            pl.BlockSpec((None, d, tn), lambda l, j: (l, 0, j)),
            pl.BlockSpec((None, 1, tn), lambda l, j: (l, 0, j)),
        ],
        out_specs=pl.BlockSpec((None, r, tn), lambda l, j: (l, 0, j)),
        out_shape=jax.ShapeDtypeStruct((n_layers, r, d6), F32),
        compiler_params=_cparams("arbitrary", "arbitrary"),
        name="ada",
    )(c_all, ada_w, ada_b.reshape(n_layers, 1, d6))


def _mod_spec(mod, tiles_per_seq):
    _, rows, d = mod.shape
    return pl.BlockSpec((None, rows, d), lambda i, *_: (i // tiles_per_seq, 0, 0))


def _gmlp_kernel(x_ref, g_ref, sh_ref, sc_ref, gt_ref, wu_ref, wv_ref, vg_ref, vb_ref, ws_ref, bs_ref,
                 wo_ref, *rest, lc, period, groups, emit_v):
    if emit_v:
        o_ref, v_ref, vn_scr, out_scr = rest
    else:
        o_ref, vn_scr, out_scr = rest
    x = x_ref[...]
    tm = x.shape[0]
    h = _norm_mod(x, g_ref[...], sh_ref[...], sc_ref[...]).astype(BF16)
    v = _gelu(_dot(h, wv_ref[...]))
    mu = jnp.mean(v, axis=-1, keepdims=True)
    vc = v - mu
    var = jnp.mean(vc * vc, axis=-1, keepdims=True)
    vn = vc * lax.rsqrt(var + EPS) * vg_ref[...] + vb_ref[...]
    if emit_v:
        v_ref[...] = vn
    vn_scr[...] = vn.astype(BF16)
    gd = vn.shape[1] // groups
    r = lax.broadcasted_iota(jnp.int32, (lc, lc), 0)
    c = lax.broadcasted_iota(jnp.int32, (lc, lc), 1)
    mask = c <= r
    if period < lc:
        blk = ~(period - 1)
        mask = mask & ((r & blk) == (c & blk))
    for g in range(groups):
        u = _gelu(_dot(h, wu_ref[:, g * gd:(g + 1) * gd]))
        wsm = jnp.where(mask, ws_ref[g], 0.0).astype(BF16)
        bcol = bs_ref[:, g:g + 1]
        for ci in range(tm // lc):
            rows = slice(ci * lc, (ci + 1) * lc)
            mixed = _dot(wsm, vn_scr[rows, g * gd:(g + 1) * gd]) + bcol
            out_scr[rows, g * gd:(g + 1) * gd] = (u[rows] * mixed).astype(BF16)
    o_ref[...] = x + gt_ref[...] * _dot(out_scr[...], wo_ref[...])


def _gmlp_call(x, g, shift, scale, gate, wu, wv, vg, vb, ws, bs_t, wo, *, tm, tiles_per_seq, period, emit_v):
    n, d = x.shape
    da = wu.shape[1]
    groups, lc, _ = ws.shape
    const2 = lambda i: (0, 0)
    in_specs = [
        pl.BlockSpec((tm, d), lambda i: (i, 0)),
        pl.BlockSpec((1, d), const2),
        _mod_spec(shift, tiles_per_seq), _mod_spec(scale, tiles_per_seq), _mod_spec(gate, tiles_per_seq),
        pl.BlockSpec((d, da), const2), pl.BlockSpec((d, da), const2),
        pl.BlockSpec((1, da), const2), pl.BlockSpec((1, da), const2),
        pl.BlockSpec((groups, lc, lc), lambda i: (0, 0, 0)),
        pl.BlockSpec((lc, groups), const2),
        pl.BlockSpec((da, d), const2),
    ]
    out_specs = [pl.BlockSpec((tm, d), lambda i: (i, 0))]
    out_shape = [jax.ShapeDtypeStruct((n, d), F32)]
    if emit_v:
        out_specs.append(pl.BlockSpec((tm, da), lambda i: (i, 0)))
        out_shape.append(jax.ShapeDtypeStruct((n, da), F32))
    return pl.pallas_call(
        functools.partial(_gmlp_kernel, lc=lc, period=period, groups=groups, emit_v=emit_v),
        grid=(n // tm,),
        in_specs=in_specs, out_specs=out_specs, out_shape=out_shape,
        scratch_shapes=[pltpu.VMEM((tm, da), BF16), pltpu.VMEM((tm, da), BF16)],
        compiler_params=_cparams("arbitrary"),
        name="gmlp",
    )(x, g, shift, scale, gate, wu, wv, vg, vb, ws, bs_t, wo)


def _ffn_kernel(x_ref, g_ref, sh_ref, sc_ref, gt_ref, wg_ref, wu_ref, wd_ref, o_ref, h_scr, acc_scr):
    j = pl.program_id(1)

    @pl.when(j == 0)
    def _():
        h_scr[...] = _norm_mod(x_ref[...], g_ref[...], sh_ref[...], sc_ref[...]).astype(BF16)
        acc_scr[...] = jnp.zeros_like(acc_scr)

    h = h_scr[...]
    a = _silu(_dot(h, wg_ref[...].astype(BF16))) * _dot(h, wu_ref[...].astype(BF16))
    acc_scr[...] += _dot(a.astype(BF16), wd_ref[...].astype(BF16))

    @pl.when(j == pl.num_programs(1) - 1)
    def _():
        o_ref[...] = x_ref[...] + gt_ref[...] * acc_scr[...]


def _ffn_call(x, g, shift, scale, gate, w_gu, w_down, *, tm, tiles_per_seq):
    n, d = x.shape
    f = w_down.shape[0]
    tf = min(TF_FFN, f)
    nf = f // tf
    const2 = lambda i, j: (0, 0)
    return pl.pallas_call(
        _ffn_kernel,
        grid=(n // tm, nf),
        in_specs=[
            pl.BlockSpec((tm, d), lambda i, j: (i, 0)),
            pl.BlockSpec((1, d), const2),
            _mod_spec(shift, tiles_per_seq), _mod_spec(scale, tiles_per_seq), _mod_spec(gate, tiles_per_seq),
            pl.BlockSpec((d, tf), lambda i, j: (0, j)),
            pl.BlockSpec((d, tf), lambda i, j: (0, nf + j)),
            pl.BlockSpec((tf, d), lambda i, j: (j, 0)),
        ],
        out_specs=pl.BlockSpec((tm, d), lambda i, j: (i, 0)),
        out_shape=jax.ShapeDtypeStruct((n, d), F32),
        scratch_shapes=[pltpu.VMEM((tm, d), BF16), pltpu.VMEM((tm, d), F32)],
        compiler_params=_cparams("arbitrary", "arbitrary"),
        name="ffn",
    )(x, g, shift, scale, gate, w_gu, w_gu, w_down)


def _log_sigmoid(x):
    return jnp.minimum(x, 0.0) - jnp.log(1.0 + jnp.exp(-jnp.abs(x)))


def _foxproj_kernel(x_ref, g_ref, sh_ref, sc_ref, w_ref, wf_ref, bf_ref, *rest, transposed, period,
                    tiles_per_seq, q_scale):
    if transposed:
        q_ref, k_ref, kb_ref, v_ref, vb_ref, lf_ref, cum_ref, carry_scr = rest
    else:
        q_ref, k_ref, v_ref, lf_ref, cum_ref, carry_scr = rest
    i = pl.program_id(0)
    x = x_ref[...]
    tm, d = x.shape
    h = _norm_mod(x, g_ref[...], sh_ref[...], sc_ref[...]).astype(BF16)
    if transposed:
        tq = q_ref.shape[-1]
        qt = (_dot_nt(w_ref[0:d, :], h) * q_scale).astype(BF16)
        for jq in range(tm // tq):
            q_ref[jq] = qt[:, jq * tq:(jq + 1) * tq]
        k_ref[...] = _dot_nt(w_ref[d:2 * d, :], h)
        kb_ref[...] = _dot_nt(h, w_ref[d:2 * d, :]).astype(BF16)
        vt = _dot_nt(w_ref[2 * d:3 * d, :], h)
        v_ref[...] = vt
        vb_ref[...] = vt.astype(BF16)
    else:
        q_ref[...] = _dot_nt(h, w_ref[0:d, :]) * q_scale
        k_ref[...] = _dot_nt(h, w_ref[d:2 * d, :])
        v_ref[...] = _dot_nt(h, w_ref[2 * d:3 * d, :])
    logf = _log_sigmoid(_dot_nt(wf_ref[...], h) + bf_ref[...])
    lf_ref[...] = logf
    s = lax.broadcasted_iota(jnp.int32, (tm, tm), 0)
    t = lax.broadcasted_iota(jnp.int32, (tm, tm), 1)
    upper = s <= t
    if period < tm:
        blk = ~(period - 1)
        upper = upper & ((s & blk) == (t & blk))
    upper = _ones_where(upper)
    hi, mid, lo = _split3(logf)
    cum = _dot(hi, upper) + _dot(mid, upper) + _dot(lo, upper)
    if tiles_per_seq > 1:
        @pl.when(i % tiles_per_seq == 0)
        def _():
            carry_scr[...] = jnp.zeros_like(carry_scr)
        cum = cum + carry_scr[:, 0:1]
        carry_scr[...] = jnp.broadcast_to(cum[:, tm - 1:tm], carry_scr.shape)
    cum_ref[...] = cum


def _foxproj_call(x, g, shift, scale, w_t, wf_t, b_f, *, tm, tiles_per_seq, transposed, period):
    n, d = x.shape
    heads = wf_t.shape[0]
    n_seq = n // (tm * tiles_per_seq)
    t_len = tm * tiles_per_seq
    const2 = lambda i: (0, 0)
    row = lambda i: (i, 0)
    seq_t = lambda i: (i // tiles_per_seq, 0, i % tiles_per_seq)
    in_specs = [
        pl.BlockSpec((tm, d), row),
        pl.BlockSpec((1, d), const2),
        _mod_spec(shift, tiles_per_seq), _mod_spec(scale, tiles_per_seq),
        pl.BlockSpec((3 * d, d), const2),
        pl.BlockSpec((heads, d), const2),
        pl.BlockSpec((heads, 1), const2),
    ]
    lf_spec = pl.BlockSpec((None, heads, tm), seq_t)
    lf_shape = jax.ShapeDtypeStruct((n_seq, heads, t_len), F32)
    if transposed:
        tq = min(TQ_ATTN, tm)
        kv_spec = pl.BlockSpec((None, d, tm), seq_t)
        q_spec = pl.BlockSpec((None, tm // tq, d, tq), lambda i: (i // tiles_per_seq, i % tiles_per_seq, 0, 0))
        out_specs = [q_spec, kv_spec, pl.BlockSpec((tm, d), row), kv_spec, kv_spec, lf_spec, lf_spec]
        out_shape = [jax.ShapeDtypeStruct((n_seq, t_len // tq, d, tq), BF16),
                     jax.ShapeDtypeStruct((n_seq, d, t_len), F32), jax.ShapeDtypeStruct((n, d), BF16),
                     jax.ShapeDtypeStruct((n_seq, d, t_len), F32), jax.ShapeDtypeStruct((n_seq, d, t_len), BF16),
                     lf_shape, lf_shape]
    else:
        out_specs = [pl.BlockSpec((tm, d), row)] * 3 + [lf_spec, lf_spec]
        out_shape = [jax.ShapeDtypeStruct((n, d), F32)] * 3 + [lf_shape, lf_shape]
    return pl.pallas_call(
        functools.partial(_foxproj_kernel, transposed=transposed, period=period, tiles_per_seq=tiles_per_seq,
                          q_scale=float(d // heads) ** -0.5),
        grid=(n // tm,),
        in_specs=in_specs, out_specs=out_specs, out_shape=out_shape,
        scratch_shapes=[pltpu.VMEM((heads, LANES), F32)],
        compiler_params=_cparams("arbitrary"),
        name="foxproj",
    )(x, g, shift, scale, w_t, wf_t, b_f)


def _attn_kernel(qt_ref, k_ref, vt_ref, cum_ref, o_ref, ck_scr, m_scr, l_scr, acc_scr, *, hd):
    nq, _, tq = qt_ref.shape
    lane = lax.broadcasted_iota(jnp.int32, (tq, LANES), 1)
    key = lax.broadcasted_iota(jnp.int32, (tq, tq), 0)
    qry = lax.broadcasted_iota(jnp.int32, (tq, tq), 1)
    eye = key == qry
    causal = key <= qry
    m_scr[...] = jnp.full_like(m_scr, NEG)
    l_scr[...] = jnp.zeros_like(l_scr)
    acc_scr[...] = jnp.zeros_like(acc_scr)

    def update(hh, qi, kh, vh, masked):
        s = _dot(kh, qt_ref[qi])
        s = s + cum_ref[hh, pl.ds(qi, 1), :] - ck_scr[...]
        if masked:
            s = jnp.where(causal, s, NEG)
        m_old = m_scr[hh, qi]
        m_new = jnp.maximum(m_old, jnp.max(s, axis=0, keepdims=True))
        alpha = jnp.exp(m_old - m_new)
        p = jnp.exp(s - m_new)
        l_scr[hh, qi] = alpha * l_scr[hh, qi] + jnp.sum(p, axis=0, keepdims=True)
        acc_scr[hh, qi] = alpha * acc_scr[hh, qi] + _dot(vh, p.astype(BF16))
        m_scr[hh, qi] = m_new

    for ki in range(nq):
        keys = slice(ki * tq, (ki + 1) * tq)
        k_blk = k_ref[keys, :]
        for hh in range(2):
            kh = jnp.where((lane >= hh * hd) & (lane < (hh + 1) * hd), k_blk, jnp.zeros_like(k_blk))
            vh = vt_ref[hh * hd:(hh + 1) * hd, keys]
            col = jnp.sum(jnp.where(eye, cum_ref[hh, ki:ki + 1, :], 0.0), axis=1, keepdims=True)
            ck_scr[...] = jnp.broadcast_to(col, ck_scr.shape)
            update(hh, ki, kh, vh, True)
            for qi in range(ki + 1, nq):
                update(hh, qi, kh, vh, False)

    for qi in range(nq):
        ot = jnp.concatenate([acc_scr[0, qi] / l_scr[0, qi], acc_scr[1, qi] / l_scr[1, qi]], axis=0)
        o_ref[qi * tq:(qi + 1) * tq, :] = ot.T.astype(o_ref.dtype)


def _attn_call(qt, k, vt, cum5, *, t_len, heads):
    n_seq, nq, d, tq = qt.shape
    hd = d // heads
    assert 2 * hd == LANES
    return pl.pallas_call(
        functools.partial(_attn_kernel, hd=hd),
        grid=(n_seq, heads // 2),
        in_specs=[
            pl.BlockSpec((None, nq, LANES, tq), lambda b, p: (b, 0, p, 0)),
            pl.BlockSpec((t_len, LANES), lambda b, p: (b, p)),
            pl.BlockSpec((None, LANES, t_len), lambda b, p: (b, p, 0)),
            pl.BlockSpec((None, None, 2, nq, tq), lambda b, p: (b, p, 0, 0, 0)),
        ],
        out_specs=pl.BlockSpec((t_len, LANES), lambda b, p: (b, p)),
        out_shape=jax.ShapeDtypeStruct((n_seq * t_len, d), BF16),
        scratch_shapes=[pltpu.VMEM((tq, tq), F32), pltpu.VMEM((2, nq, 1, tq), F32),
                        pltpu.VMEM((2, nq, 1, tq), F32), pltpu.VMEM((2, nq, hd, tq), F32)],
        compiler_params=_cparams("arbitrary", "arbitrary"),
        name="attn_prompt",
    )(qt, k, vt, cum5)


def _attn_sample_kernel(pt_ref, q_ref, cn_ref, kn_ref, vn_ref, *rest, pages, hd, page_size):
    k_refs = rest[0:pages]
    v_refs = rest[pages:2 * pages]
    lf_refs = rest[2 * pages:3 * pages]
    o_ref, qbd_scr, cn_scr, m_scr, l_scr, acc_scr, suf_scr = rest[3 * pages:]
    j = pl.program_id(1)
    tn, d = q_ref.shape
    heads = d // hd
    rows = heads * tn
    row = lax.broadcasted_iota(jnp.int32, (rows, LANES), 0)
    lane = lax.broadcasted_iota(jnp.int32, (rows, LANES), 1)
    q_of_row = row & (tn - 1)

    def rep(a):
        return jnp.broadcast_to(a[:, None, :], (heads, tn, a.shape[-1])).reshape(rows, a.shape[-1])

    @pl.when(j == 0)
    def _():
        q = q_ref[...]
        qrep = jnp.broadcast_to(q[None, :, :], (heads, tn, d)).reshape(rows, d)
        rr = lax.broadcasted_iota(jnp.int32, (rows, d), 0)
        cc = lax.broadcasted_iota(jnp.int32, (rows, d), 1)
        qbd_scr[...] = jnp.where(_shift_div(rr, tn) == _shift_div(cc, hd), qrep, 0.0).astype(BF16)
        cn_scr[...] = jnp.sum(jnp.where(lane == q_of_row, rep(cn_ref[...]), 0.0), axis=1, keepdims=True)
        m_scr[...] = jnp.full_like(m_scr, NEG)
        l_scr[...] = jnp.zeros_like(l_scr)
        acc_scr[...] = jnp.zeros_like(acc_scr)
        suf_scr[...] = jnp.zeros_like(suf_scr)

    s_idx = lax.broadcasted_iota(jnp.int32, (page_size, page_size), 0)
    t_idx = lax.broadcasted_iota(jnp.int32, (page_size, page_size), 1)
    later = _ones_where(s_idx > t_idx)
    carry = suf_scr[:, 0:1]
    scores = [None] * pages
    for i in reversed(range(pages)):
        lf = lf_refs[i][...]
        hi, mid, lo = _split3(lf)
        suf = _dot(hi, later) + _dot(mid, later) + _dot(lo, later) + carry
        carry = suf[:, 0:1] + lf[:, 0:1]
        scores[i] = _dot(qbd_scr[...], k_refs[i][...].astype(BF16)) + rep(suf)
    suf_scr[...] = jnp.broadcast_to(carry, suf_scr.shape)
    s = jnp.concatenate(scores, axis=1) + cn_scr[...]
    m_old = m_scr[...]
    m_new = jnp.maximum(m_old, jnp.max(s, axis=1, keepdims=True))
    alpha = jnp.exp(m_old - m_new)
    p = jnp.exp(s - m_new)
    l_scr[...] = alpha * l_scr[...] + jnp.sum(p, axis=1, keepdims=True)
    pb = p.astype(BF16)
    pv = _dot_nt(pb[:, 0:page_size], v_refs[0][...].astype(BF16))
    for i in range(1, pages):
        pv = pv + _dot_nt(pb[:, i * page_size:(i + 1) * page_size], v_refs[i][...].astype(BF16))
    acc_scr[...] = alpha * acc_scr[...] + pv
    m_scr[...] = m_new

    @pl.when(j == pl.num_programs(1) - 1)
    def _():
        pad = jnp.zeros((LANES - tn, d), BF16)
        k_new = jnp.concatenate([kn_ref[...].astype(BF16), pad], axis=0)
        v_new = jnp.concatenate([vn_ref[...].astype(BF16), pad], axis=0)
        s = _dot_nt(qbd_scr[...], k_new) + cn_scr[...] - rep(cn_ref[...])
        s = jnp.where(lane <= q_of_row, s, NEG)
        m_old = m_scr[...]
        m_new = jnp.maximum(m_old, jnp.max(s, axis=1, keepdims=True))
        alpha = jnp.exp(m_old - m_new)
        p = jnp.exp(s - m_new)
        l_new = alpha * l_scr[...] + jnp.sum(p, axis=1, keepdims=True)
        o = (alpha * acc_scr[...] + _dot(p.astype(BF16), v_new)) / l_new
        rr = lax.broadcasted_iota(jnp.int32, (rows, d), 0)
        cc = lax.broadcasted_iota(jnp.int32, (rows, d), 1)
        o = jnp.where(_shift_div(rr, tn) == _shift_div(cc, hd), o, 0.0)
        o_ref[...] = jnp.sum(o.reshape(heads, tn, d), axis=0)


def _attn_sample_call(page_table, q, cn_pad, k_new, v_new, kc_t, vc_t, lfc_t, *, tn, heads):
    n, d = q.shape
    n_seq, n_pages = page_table.shape
    page_size = kc_t.shape[-1]
    pages = min(PAGES_PER_STEP, n_pages)
    n_steps = n_pages // pages
    hd = d // heads

    def page_map(i):
        def index(b, j, pt):
            return (pt[b * n_pages + (n_steps - 1 - j) * pages + i], 0, 0)
        return index

    row = lambda b, j, pt: (b, 0)
    in_specs = [
        pl.BlockSpec((tn, d), row),
        pl.BlockSpec((None, heads, LANES), lambda b, j, pt: (b, 0, 0)),
        pl.BlockSpec((tn, d), row), pl.BlockSpec((tn, d), row),
    ]
    in_specs += [pl.BlockSpec((None, d, page_size), page_map(i)) for i in range(pages)]
    in_specs += [pl.BlockSpec((None, d, page_size), page_map(i)) for i in range(pages)]
    in_specs += [pl.BlockSpec((None, heads, page_size), page_map(i)) for i in range(pages)]
    rows = heads * tn
    return pl.pallas_call(
        functools.partial(_attn_sample_kernel, pages=pages, hd=hd, page_size=page_size),
        grid_spec=pltpu.PrefetchScalarGridSpec(
            num_scalar_prefetch=1,
            grid=(n_seq, n_steps),
            in_specs=in_specs,
            out_specs=pl.BlockSpec((tn, d), row),
            scratch_shapes=[pltpu.VMEM((rows, d), BF16), pltpu.VMEM((rows, 1), F32), pltpu.VMEM((rows, 1), F32),
                            pltpu.VMEM((rows, 1), F32), pltpu.VMEM((rows, d), F32),
                            pltpu.VMEM((heads, LANES), F32)],
        ),
        out_shape=jax.ShapeDtypeStruct((n, d), F32),
        compiler_params=_cparams("arbitrary", "arbitrary"),
        name="attn_sample",
    )(page_table.reshape(-1), q, cn_pad, k_new, v_new, *([kc_t] * pages), *([vc_t] * pages), *([lfc_t] * pages))


def _post_kernel(op_ref, os_ref, wo_ref, xp_ref, xs_ref, gtp_ref, gts_ref, g_ref, shp_ref, shs_ref, scp_ref,
                 scs_ref, wr_ref, br_ref, x3_ref, h_ref, meta_ref, blk_ref, cout_ref, carry_scr,
                 *, n_experts, n_prompt_blocks):
    i = pl.program_id(0)
    tm = xp_ref.shape[0]
    is_p = i < n_prompt_blocks

    @pl.when(i == 0)
    def _():
        carry_scr[...] = jnp.zeros_like(carry_scr)

    o = jnp.where(is_p, op_ref[...], os_ref[...].astype(BF16))
    x = jnp.where(is_p, xp_ref[...], xs_ref[...])
    gate = jnp.where(is_p, gtp_ref[...], gts_ref[...])
    shift = jnp.where(is_p, shp_ref[...], shs_ref[...])
    scale = jnp.where(is_p, scp_ref[...], scs_ref[...])
    x3 = x + gate * _dot(o, wo_ref[...])
    x3_ref[...] = x3
    h = _norm_mod(x3, g_ref[...], shift, scale)
    h_hi = h.astype(BF16)
    h_ref[...] = h_hi
    h_lo = (h - h_hi.astype(F32)).astype(BF16)
    wr = wr_ref[...]
    w_hi = wr.astype(BF16)
    w_lo = (wr - w_hi.astype(F32)).astype(BF16)
    logits = _dot(h_hi, w_hi) + _dot(h_hi, w_lo) + _dot(h_lo, w_hi) + br_ref[...]
    lane_i = lax.broadcasted_iota(jnp.int32, (tm, LANES), 1)
    lane = lane_i.astype(F32)
    logits = jnp.where(lane_i < n_experts, logits, NEG)
    l1 = jnp.max(logits, axis=1, keepdims=True)
    i1 = jnp.min(jnp.where(logits == l1, lane, float(LANES)), axis=1, keepdims=True)
    rest = jnp.where(lane == i1, NEG, logits)
    l2 = jnp.max(rest, axis=1, keepdims=True)
    i2 = jnp.min(jnp.where(rest == l2, lane, float(LANES)), axis=1, keepdims=True)
    e = jnp.exp(l2 - l1)
    g1 = 1.0 / (1.0 + e)
    g2 = e / (1.0 + e)
    onehot = jnp.where((lane == i1) | (lane == i2), 1.0, 0.0)
    r = lax.broadcasted_iota(jnp.int32, (tm, tm), 0)
    c = lax.broadcasted_iota(jnp.int32, (tm, tm), 1)
    before = _ones_where(c < r)
    carry = carry_scr[0:1, :]
    blk_ref[...] = carry
    prefix = _dot(before, onehot.astype(BF16)) + carry
    r1 = jnp.sum(jnp.where(lane == i1, prefix, 0.0), axis=1, keepdims=True)
    r2 = jnp.sum(jnp.where(lane == i2, prefix, 0.0), axis=1, keepdims=True)
    cols = (i1, i2, g1, g2, r1, r2)
    meta = jnp.zeros((tm, LANES), F32)
    for k, col in enumerate(cols):
        meta = jnp.where(lane_i == k, col, meta)
    meta_ref[...] = meta
    carry = carry + jnp.sum(onehot, axis=0, keepdims=True)
    carry_scr[...] = jnp.broadcast_to(carry, carry_scr.shape)
    cout_ref[...] = carry


def _post_call(o_p, o_s, wo, x_p, x_s, gate_p, gate_s, g, shift_p, shift_s, scale_p, scale_s, wr_pad, br_pad,
               *, tm, blocks_per_seq, n_experts):
    n_p, d = x_p.shape
    n_s = x_s.shape[0]
    npb = n_p // tm
    nb = npb + n_s // tm
    n = n_p + n_s
    const2 = lambda i: (0, 0)
    row = lambda i: (i, 0)
    p_row = lambda i: (jnp.minimum(i, npb - 1), 0)
    s_row = lambda i: (jnp.maximum(i - npb, 0), 0)
    p_mod = pl.BlockSpec((None, 1, d), lambda i: (jnp.minimum(i, npb - 1) // blocks_per_seq, 0, 0))
    s_mod = pl.BlockSpec((None, tm, d), lambda i: (jnp.maximum(i - npb, 0), 0, 0))
    return pl.pallas_call(
        functools.partial(_post_kernel, n_experts=n_experts, n_prompt_blocks=npb),
        grid=(nb,),
        in_specs=[
            pl.BlockSpec((tm, d), p_row), pl.BlockSpec((tm, d), s_row),
            pl.BlockSpec((d, d), const2),
            pl.BlockSpec((tm, d), p_row), pl.BlockSpec((tm, d), s_row),
            p_mod, s_mod,
            pl.BlockSpec((1, d), const2),
            p_mod, s_mod, p_mod, s_mod,
            pl.BlockSpec((d, LANES), const2),
            pl.BlockSpec((1, LANES), const2),
        ],
        out_specs=[
            pl.BlockSpec((tm, d), row), pl.BlockSpec((tm, d), row), pl.BlockSpec((tm, LANES), row),
            pl.BlockSpec((None, 1, LANES), lambda i: (i, 0, 0)),
            pl.BlockSpec((1, LANES), const2),
        ],
        out_shape=[
            jax.ShapeDtypeStruct((n, d), F32), jax.ShapeDtypeStruct((n, d), BF16),
            jax.ShapeDtypeStruct((n, LANES), F32),
            jax.ShapeDtypeStruct((nb, 1, LANES), F32),
            jax.ShapeDtypeStruct((1, LANES), F32),
        ],
        scratch_shapes=[pltpu.VMEM((SUBLANES, LANES), F32)],
        compiler_params=_cparams("arbitrary"),
        name="post_attn_router",
    )(o_p, o_s, wo, x_p, x_s, gate_p, gate_s, g, shift_p, shift_s, scale_p, scale_s, wr_pad, br_pad)


def _dispatch_kernel(lo_ref, hi_ref, pos_ref, gate_ref, h_ref, xs_ref, gs_ref, acc_scr, gacc_scr, *, tb):
    j = pl.program_id(0)
    sub = xs_ref.shape[0]
    lo = lo_ref[j]
    hi = hi_ref[j]
    acc_scr[...] = jnp.zeros_like(acc_scr)
    gacc_scr[...] = jnp.zeros_like(gacc_scr)
    dest = j * sub + lax.broadcasted_iota(jnp.int32, (sub, tb), 0)

    def body(b, carry):
        pos = pos_ref[b]
        gate = gate_ref[b]
        hit0 = pos[0:1, :] == dest
        hit1 = pos[1:2, :] == dest
        rows = pl.ds(pl.multiple_of(b * tb, tb), tb)
        acc_scr[...] += _dot(_ones_where(hit0 | hit1), h_ref[rows, :])
        gacc_scr[...] += jnp.where(hit0, gate[0:1, :], 0.0) + jnp.where(hit1, gate[1:2, :], 0.0)
        return carry

    lax.fori_loop(lo, hi, body, 0)
    xs_ref[...] = acc_scr[...].astype(BF16)
    gs_ref[...] = jnp.broadcast_to(jnp.sum(gacc_scr[...], axis=1, keepdims=True), gs_ref.shape)


def _dispatch_call(tile_lo, tile_hi, pos3, gate3, h, *, n_sorted, tb):
    n, d = h.shape
    nb = pos3.shape[0]
    whole = lambda nd: (lambda j, lo, hi: (0,) * nd)
    once = pl.Buffered(1)
    return pl.pallas_call(
        functools.partial(_dispatch_kernel, tb=tb),
        grid_spec=pltpu.PrefetchScalarGridSpec(
            num_scalar_prefetch=2,
            grid=(n_sorted // SUB,),
            in_specs=[
                pl.BlockSpec((nb, 2, tb), whole(3), pipeline_mode=once),
                pl.BlockSpec((nb, 2, tb), whole(3), pipeline_mode=once),
                pl.BlockSpec((n, d), whole(2), pipeline_mode=once),
            ],
            out_specs=[pl.BlockSpec((SUB, d), lambda j, lo, hi: (j, 0)),
                       pl.BlockSpec((SUB, LANES), lambda j, lo, hi: (j, 0))],
            scratch_shapes=[pltpu.VMEM((SUB, d), F32), pltpu.VMEM((SUB, tb), F32)],
        ),
        out_shape=[jax.ShapeDtypeStruct((n_sorted, d), BF16), jax.ShapeDtypeStruct((n_sorted, LANES), F32)],
        compiler_params=_cparams("arbitrary"),
        name="moe_dispatch",
    )(tile_lo, tile_hi, pos3, gate3, h)


def _moe_kernel(ex_ref, nsub_ref, x_ref, gs_ref, wg_ref, wu_ref, wd_ref, y_ref, acc_scr, wgb_scr, wub_scr, wdb_scr):
    s = pl.program_id(0)
    j = pl.program_id(1)
    nsub = nsub_ref[s]
    last = j == pl.num_programs(1) - 1

    @pl.when(nsub > 0)
    def _():
        wgb_scr[...] = wg_ref[...].astype(BF16)
        wub_scr[...] = wu_ref[...].astype(BF16)
        wdb_scr[...] = wd_ref[...].astype(BF16)

    n_steps = (nsub * SUB + FFN_ROWS - 1) // FFN_ROWS

    def step_rows(t):
        return pl.ds(pl.multiple_of(t * FFN_ROWS, FFN_ROWS), FFN_ROWS)

    @pl.when(j == 0)
    def _():
        def zero(t, carry):
            acc_scr[step_rows(t), :] = jnp.zeros((FFN_ROWS, acc_scr.shape[1]), F32)
            return carry
        lax.fori_loop(0, n_steps, zero, 0)

    def body(t, carry):
        rows = step_rows(t)
        x = x_ref[rows, :]
        a = _silu(_dot(x, wgb_scr[...])) * _dot(x, wub_scr[...])
        acc_scr[rows, :] += _dot(a.astype(BF16), wdb_scr[...])
        return carry

    lax.fori_loop(0, n_steps, body, 0)

    @pl.when(last)
    def _():
        def emit(t, carry):
            rows = step_rows(t)
            y_ref[rows, :] = (acc_scr[rows, :] * gs_ref[rows, 0:1]).astype(BF16)
            return carry
        lax.fori_loop(0, n_steps, emit, 0)

        def zero(t, carry):
            y_ref[step_rows(t), :] = jnp.zeros((FFN_ROWS, y_ref.shape[1]), BF16)
            return carry
        lax.fori_loop(n_steps, y_ref.shape[0] // FFN_ROWS, zero, 0)


def _moe_call(sup_expert, sup_nsub, xs, gs, w_gu, w_down, *, ts):
    n_sorted, d = xs.shape
    f = w_down.shape[1]
    tf = min(TF_FFN, f)
    nf = f // tf
    n_super = n_sorted // ts

    def jj(s, j, nsub):
        return jnp.where(nsub[s] > 0, j, nf - 1)

    once = pl.Buffered(1)
    return pl.pallas_call(
        _moe_kernel,
        grid_spec=pltpu.PrefetchScalarGridSpec(
            num_scalar_prefetch=2,
            grid=(n_super, nf),
            in_specs=[
                pl.BlockSpec((ts, d), lambda s, j, ex, ns: (s, 0), pipeline_mode=once),
                pl.BlockSpec((ts, LANES), lambda s, j, ex, ns: (s, 0), pipeline_mode=once),
                pl.BlockSpec((None, d, tf), lambda s, j, ex, ns: (ex[s], 0, jj(s, j, ns))),
                pl.BlockSpec((None, d, tf), lambda s, j, ex, ns: (ex[s], 0, nf + jj(s, j, ns))),
                pl.BlockSpec((None, tf, d), lambda s, j, ex, ns: (ex[s], jj(s, j, ns), 0)),
            ],
            out_specs=pl.BlockSpec((ts, d), lambda s, j, ex, ns: (s, 0)),
            scratch_shapes=[pltpu.VMEM((ts, d), F32), pltpu.VMEM((d, tf), BF16), pltpu.VMEM((d, tf), BF16),
                            pltpu.VMEM((tf, d), BF16)],
        ),
        out_shape=jax.ShapeDtypeStruct((n_sorted, d), BF16),
        compiler_params=_cparams("arbitrary", "arbitrary"),
        name="moe_ffn",
    )(sup_expert, sup_nsub, xs, gs, w_gu, w_gu, w_down)


def _combine_kernel(tile_ref, need2_ref, pos_ref, x_ref, gtp_ref, gts_ref, gf_ref, *rest,
                    n_experts, n_prompt_blocks):
    y_refs = rest[0:2 * n_experts]
    op_ref, os_ref, acc_scr = rest[2 * n_experts:]
    b = pl.program_id(0)
    tb = acc_scr.shape[0]
    pos = pos_ref[...]
    col = lax.broadcasted_iota(jnp.int32, (tb, SUB), 1)
    acc_scr[...] = jnp.zeros_like(acc_scr)
    for e in range(n_experts):
        base = tile_ref[b * n_experts + e] * SUB
        p0 = jnp.where(pos[:, 2:3] == e, pos[:, 0:1], -1)
        p1 = jnp.where(pos[:, 3:4] == e, pos[:, 1:2], -1)
        for w in range(2):
            def add(e=e, w=w, base=base, p0=p0, p1=p1):
                src = base + w * SUB + col
                oh = _ones_where((p0 == src) | (p1 == src))
                acc_scr[...] += _dot(oh, y_refs[2 * e + w][...])
            if w == 0:
                add()
            else:
                pl.when(need2_ref[b * n_experts + e] > 0)(add)

    is_p = b < n_prompt_blocks
    x4 = x_ref[...] + jnp.where(is_p, gtp_ref[...], gts_ref[...]) * acc_scr[...]
    ms = jnp.mean(x4 * x4, axis=-1, keepdims=True)
    y = x4 * lax.rsqrt(ms + EPS) * gf_ref[...]

    @pl.when(is_p)
    def _():
        op_ref[...] = y

    @pl.when(jnp.logical_not(is_p))
    def _():
        os_ref[...] = y


def _combine_call(win_tile, win_need2, pos_rows, x, gate_p, gate_s, g_final, ys, *, n_p, tb, blocks_per_seq,
                  n_experts):
    n, d = x.shape
    n_s = n - n_p
    npb = n_p // tb
    nb = n // tb

    def pb(b):
        return jnp.minimum(b, npb - 1)

    def sb(b):
        return jnp.maximum(b - npb, 0)

    def win_map(e, w):
        return lambda b, tile, need: (tile[b * n_experts + e] + w, 0)

    in_specs = [
        pl.BlockSpec((tb, LANES), lambda b, tile, need: (b, 0)),
        pl.BlockSpec((tb, d), lambda b, tile, need: (b, 0)),
        pl.BlockSpec((None, 1, d), lambda b, tile, need: (pb(b) // blocks_per_seq, 0, 0)),
        pl.BlockSpec((None, tb, d), lambda b, tile, need: (sb(b), 0, 0)),
        pl.BlockSpec((1, d), lambda b, tile, need: (0, 0)),
    ]
    in_specs += [pl.BlockSpec((SUB, d), win_map(e, w)) for e in range(n_experts) for w in range(2)]
    return pl.pallas_call(
        functools.partial(_combine_kernel, n_experts=n_experts, n_prompt_blocks=npb),
        grid_spec=pltpu.PrefetchScalarGridSpec(
            num_scalar_prefetch=2,
            grid=(nb,),
            in_specs=in_specs,
            out_specs=[pl.BlockSpec((tb, d), lambda b, tile, need: (pb(b), 0)),
                       pl.BlockSpec((tb, d), lambda b, tile, need: (sb(b), 0))],
            scratch_shapes=[pltpu.VMEM((tb, d), F32)],
        ),
        out_shape=[jax.ShapeDtypeStruct((n_p, d), F32), jax.ShapeDtypeStruct((n_s, d), F32)],
        compiler_params=_cparams("arbitrary"),
        name="moe_combine",
    )(win_tile, win_need2, pos_rows, x, gate_p, gate_s, g_final, *([ys] * (2 * n_experts)))


def _routing_tables(meta, blk_cnt, counts, *, n_experts, ts, tb, n_super):
    n = meta.shape[0]
    nb = n // tb
    idx = meta[:, 0:TOP_K].astype(jnp.int32)
    gates = meta[:, 2:2 + TOP_K]
    rank = meta[:, 4:4 + TOP_K].astype(jnp.int32)
    cnt = counts[0, :n_experts].astype(jnp.int32)
    n_sup_e = (cnt + ts - 1) // ts
    sup_start = jnp.cumsum(n_sup_e) - n_sup_e
    start = sup_start * ts
    pos = start[idx] + rank

    s_ids = jnp.arange(n_super, dtype=jnp.int32)
    used = jnp.sum(n_sup_e)
    sup_end = sup_start + n_sup_e
    sup_e = jnp.sum((s_ids[:, None] >= sup_end[None, :]).astype(jnp.int32), axis=1)
    sup_e = jnp.clip(sup_e, 0, n_experts - 1)
    last_e = jnp.max(jnp.where(cnt > 0, jnp.arange(n_experts, dtype=jnp.int32), 0))
    sup_e = jnp.where(s_ids < used, sup_e, last_e)
    rows_in = jnp.clip(cnt[sup_e] - (s_ids - sup_start[sup_e]) * ts, 0, ts)
    sup_nsub = jnp.where(s_ids < used, (rows_in + SUB - 1) // SUB, 0).astype(jnp.int32)

    blk = blk_cnt[:, 0, :n_experts].astype(jnp.int32)
    blk_end = jnp.concatenate([blk[1:], cnt[None, :]], axis=0)
    n_tiles = n_super * (ts // SUB)
    t_ids = jnp.arange(n_tiles, dtype=jnp.int32)
    t_e = sup_e[t_ids // (ts // SUB)]
    t_r0 = t_ids * SUB - start[t_e]
    t_active = ((t_ids // (ts // SUB)) < used) & (t_r0 < cnt[t_e]) & (t_r0 >= 0)
    t_r1 = jnp.minimum(t_r0 + SUB, cnt[t_e])
    be = blk_end[:, t_e]
    bs = blk[:, t_e]
    overlap = (be > t_r0[None, :]) & (bs < t_r1[None, :])
    b_ids = jnp.arange(nb, dtype=jnp.int32)[:, None]
    lo = jnp.min(jnp.where(overlap, b_ids, nb), axis=0)
    hi = jnp.max(jnp.where(overlap, b_ids + 1, 0), axis=0)
    tile_lo = jnp.where(t_active, lo, 0).astype(jnp.int32)
    tile_hi = jnp.where(t_active, hi, 0).astype(jnp.int32)

    first = start[None, :] + blk
    n_be = blk_end - blk
    win_tile = (first // SUB).astype(jnp.int32)
    win_need2 = ((first + n_be) > (win_tile + 1) * SUB).astype(jnp.int32)

    pos3 = pos.reshape(nb, tb, TOP_K).transpose(0, 2, 1)
    pos_rows = jnp.zeros((n, LANES), jnp.int32).at[:, 0:TOP_K].set(pos).at[:, TOP_K:2 * TOP_K].set(idx)
    gate3 = gates.reshape(nb, tb, TOP_K).transpose(0, 2, 1)
    return dict(pos3=pos3, pos_rows=pos_rows, gate3=gate3, sup_e=sup_e, sup_nsub=sup_nsub, tile_lo=tile_lo,
                tile_hi=tile_hi, win_tile=win_tile.reshape(-1), win_need2=win_need2.reshape(-1))


def kernel(x_prompt, x_sample, c_prompt, c_sample, cache_k, cache_v, cache_logf, page_table, norm_mix_g, norm_ffn_g, final_norm_g, ada_w, ada_b, gmlp_w_in, gmlp_v_g, gmlp_v_b, gmlp_w_s, gmlp_b_s, gmlp_w_out, fox_w_in, fox_b_f, fox_w_out, ffn_w_gu, ffn_w_down, moe_w_r, moe_b_r, moe_w_gu, moe_w_down):
    n_seq_p, t_len, d = x_prompt.shape
    n_seq_s, t_new, _ = x_sample.shape
    n_p = n_seq_p * t_len
    n_s = n_seq_s * t_new
    heads = N_HEADS
    hd = d // heads
    n_experts = moe_w_r.shape[-1]
    da = gmlp_w_out.shape[1]

    xp = x_prompt.reshape(n_p, d)
    xs = x_sample.reshape(n_s, d)

    mod = _ada_call(jnp.concatenate([c_prompt, c_sample], axis=0), ada_w, ada_b)

    def mods(layer):
        mp = [mod[layer, :n_seq_p, c * d:(c + 1) * d].reshape(n_seq_p, 1, d) for c in range(6)]
        ms = [jnp.repeat(mod[layer, n_seq_p:, c * d:(c + 1) * d], t_new, axis=0).reshape(1, n_s, d)
              for c in range(6)]
        return mp, ms

    row = lambda a: a.reshape(1, -1)

    mp, ms = mods(0)
    w_in = gmlp_w_in[0].astype(BF16)
    wu, wv = w_in[:, :da], w_in[:, da:]
    wo = gmlp_w_out[0].astype(BF16)
    vg, vb = row(gmlp_v_g[0]), row(gmlp_v_b[0])
    lc = min(GMLP_CHUNK, t_len)
    tm = min(TM_GMLP, t_len)
    xp = _gmlp_call(xp, row(norm_mix_g[0]), mp[0], mp[1], mp[2], wu, wv, vg, vb,
                    gmlp_w_s[0][:, :lc, :lc], gmlp_b_s[0][:, :lc].T, wo,
                    tm=tm, tiles_per_seq=t_len // tm, period=lc, emit_v=False)[0]
    reps = n_s // t_new
    ws_s = jnp.tile(gmlp_w_s[0][:, :t_new, :t_new], (1, reps, reps))
    bs_s = jnp.tile(gmlp_b_s[0][:, :t_new], (1, reps)).T
    xs, v_rows = _gmlp_call(xs, row(norm_mix_g[0]), ms[0], ms[1], ms[2], wu, wv, vg, vb, ws_s, bs_s, wo,
                            tm=n_s, tiles_per_seq=1, period=t_new, emit_v=True)
    tm = min(TM_FFN, t_len)
    xp = _ffn_call(xp, row(norm_ffn_g[0]), mp[3], mp[4], mp[5], ffn_w_gu[0], ffn_w_down[0],
                   tm=tm, tiles_per_seq=t_len // tm)
    xs = _ffn_call(xs, row(norm_ffn_g[0]), ms[3], ms[4], ms[5], ffn_w_gu[0], ffn_w_down[0],
                   tm=n_s, tiles_per_seq=1)

    mp, ms = mods(1)
    w_t = fox_w_in[0].T.astype(BF16)
    wqkv_t, wf_t = w_t[:3 * d], w_t[3 * d:]
    b_f = fox_b_f[0].reshape(heads, 1)
    tm = min(TM_PROJ, t_len)
    qt_p, kt_p, kb_p, vt_p, vtb_p, lft_p, cumt_p = _foxproj_call(
        xp, row(norm_mix_g[1]), mp[0], mp[1], wqkv_t, wf_t, b_f,
        tm=tm, tiles_per_seq=t_len // tm, transposed=True, period=tm)
    q_s, k_s, v_s, lft_s, cumt_s = _foxproj_call(
        xs, row(norm_mix_g[1]), ms[0], ms[1], wqkv_t, wf_t, b_f,
        tm=n_s, tiles_per_seq=1, transposed=False, period=t_new)

    nq, tq = qt_p.shape[1], qt_p.shape[3]
    o_p = _attn_call(qt_p, kb_p, vtb_p, cumt_p.reshape(n_seq_p, heads // 2, 2, nq, tq), t_len=t_len, heads=heads)

    page_size = cache_k.shape[2]
    n_phys = cache_k.shape[1]
    kc_t = jnp.transpose(cache_k[0], (0, 2, 3, 1)).reshape(n_phys, d, page_size)
    vc_t = jnp.transpose(cache_v[0], (0, 2, 3, 1)).reshape(n_phys, d, page_size)
    lfc_t = jnp.transpose(cache_logf[0], (0, 2, 1))
    cn = cumt_s[0].reshape(heads, n_seq_s, t_new).transpose(1, 0, 2)
    cn_pad = jnp.zeros((n_seq_s, heads, LANES), F32).at[:, :, :t_new].set(cn)
    o_s = _attn_sample_call(page_table, q_s, cn_pad, k_s, v_s, kc_t, vc_t, lfc_t, tn=t_new, heads=heads)

    wo_f = fox_w_out[0].astype(BF16)
    wr_pad = jnp.zeros((d, LANES), F32).at[:, :n_experts].set(moe_w_r[0])
    br_pad = jnp.zeros((1, LANES), F32).at[0, :n_experts].set(moe_b_r[0])
    tb = min(TM_POST, t_len)
    ms_blk = [m.reshape(n_s // tb, tb, d) for m in ms]
    x3, h_all, meta, blk_cnt, counts = _post_call(
        o_p, o_s, wo_f, xp, xs, mp[2], ms_blk[2], row(norm_ffn_g[1]), mp[3], ms_blk[3], mp[4], ms_blk[4],
        wr_pad, br_pad, tm=tb, blocks_per_seq=t_len // tb, n_experts=n_experts)

    n_tok = n_p + n_s
    ts = TS_MOE
    n_super = (TOP_K * n_tok) // ts + n_experts + 1
    rt = _routing_tables(meta, blk_cnt, counts, n_experts=n_experts, ts=ts, tb=tb, n_super=n_super)
    xs_sorted, g_sorted = _dispatch_call(rt["tile_lo"], rt["tile_hi"], rt["pos3"], rt["gate3"], h_all,
                                         n_sorted=n_super * ts, tb=tb)
    ys_sorted = _moe_call(rt["sup_e"], rt["sup_nsub"], xs_sorted, g_sorted, moe_w_gu[0], moe_w_down[0], ts=ts)
    y_p, y_s = _combine_call(rt["win_tile"], rt["win_need2"], rt["pos_rows"], x3, mp[5], ms_blk[5],
                             row(final_norm_g), ys_sorted,
                             n_p=n_p, tb=tb, blocks_per_seq=t_len // tb, n_experts=n_experts)

    y_prompt = y_p.reshape(n_seq_p, t_len, d)
    y_sample = y_s.reshape(n_seq_s, t_new, d)
    state_a_v_sample = v_rows.reshape(1, n_seq_s, t_new, da)
    k_prompt = kt_p.reshape(1, n_seq_p, heads, hd, t_len).transpose(0, 1, 4, 2, 3)
    v_prompt = vt_p.reshape(1, n_seq_p, heads, hd, t_len).transpose(0, 1, 4, 2, 3)
    logf_prompt = lft_p.transpose(0, 2, 1)[None]
    k_sample = k_s.reshape(1, n_seq_s, t_new, heads, hd)
    v_sample = v_s.reshape(1, n_seq_s, t_new, heads, hd)
    logf_sample = lft_s[0].T.reshape(1, n_seq_s, t_new, heads)
    return (y_prompt, y_sample, state_a_v_sample, k_prompt, v_prompt, logf_prompt, k_sample, v_sample, logf_sample)
```

```python
import functools

import jax
import jax.numpy as jnp
from jax import lax
from jax.experimental import pallas as pl
from jax.experimental.pallas import tpu as pltpu

F32 = jnp.float32
BF16 = jnp.bfloat16

N_HEADS = 16
GMLP_GROUPS = 8
GMLP_CHUNK = 128
TOP_K = 2
EPS = 1e-6
NEG = -1e30
LOG2E = 1.4426950408889634

LANES = 128
SUBLANES = 8
VMEM_LIMIT = 56 * 1024 * 1024

TM_GMLP = 512
TM_FFN = 1024
TF_FFN = 512
TM_PROJ = 512
TQ_ATTN = 512
TM_POST = 256
SUB = 256
FFN_ROWS = 2 * SUB
TS_MOE = 2048
PAGES_PER_STEP = 16


def _cparams(*sem):
    return pltpu.CompilerParams(dimension_semantics=sem, vmem_limit_bytes=VMEM_LIMIT)


def _dot(a, b):
    return jnp.dot(a, b, preferred_element_type=F32)


def _dot_nt(a, b):
    return lax.dot_general(a, b, (((1,), (1,)), ((), ())), preferred_element_type=F32)


def _norm_mod(x, g, shift, scale):
    ms = jnp.mean(x * x, axis=-1, keepdims=True)
    y = x * lax.rsqrt(ms + EPS) * g
    return y * (1.0 + scale) + shift


def _gelu(x):
    return 0.5 * x * (1.0 + lax.erf(x * (2.0 ** -0.5)))


def _silu(x):
    return x * jax.nn.sigmoid(x)


def _ones_where(cond):
    return jnp.where(cond, 1.0, 0.0).astype(BF16)


def _shift_div(x, c):
    assert c & (c - 1) == 0
    return lax.shift_right_logical(x, c.bit_length() - 1)


def _split3(x):
    hi = x.astype(BF16)
    r = x - hi.astype(F32)
    mid = r.astype(BF16)
    lo = (r - mid.astype(F32)).astype(BF16)
    return hi, mid, lo


def _ada_kernel(c_ref, w_ref, b_ref, o_ref):
    s = _silu(c_ref[...]).astype(BF16)
    o_ref[...] = _dot(s, w_ref[...].astype(BF16)) + b_ref[...]


def _ada_call(c_all, ada_w, ada_b):
    n_layers, d, d6 = ada_w.shape
    r = c_all.shape[0]
    tn = min(d6, 1536)
    return pl.pallas_call(
        _ada_kernel,
        grid=(n_layers, d6 // tn),
        in_specs=[
            pl.BlockSpec((r, d), lambda l, j: (0, 0)),
            pl.BlockSpec((None, d, tn), lambda l, j: (l, 0, j)),
            pl.BlockSpec((None, 1, tn), lambda l, j: (l, 0, j)),
        ],
        out_specs=pl.BlockSpec((None, r, tn), lambda l, j: (l, 0, j)),
        out_shape=jax.ShapeDtypeStruct((n_layers, r, d6), F32),
        compiler_params=_cparams("arbitrary", "arbitrary"),
        name="ada",
    )(c_all, ada_w, ada_b.reshape(n_layers, 1, d6))


def _mod_spec(mod, tiles_per_seq):
    _, rows, d = mod.shape
    return pl.BlockSpec((None, rows, d), lambda i, *_: (i // tiles_per_seq, 0, 0))


def _gmlp_kernel(x_ref, g_ref, sh_ref, sc_ref, gt_ref, wu_ref, wv_ref, vg_ref, vb_ref, ws_ref, bs_ref,
                 wo_ref, *rest, lc, period, groups, emit_v):
    if emit_v:
        o_ref, v_ref, vn_scr, out_scr = rest
    else:
        o_ref, vn_scr, out_scr = rest
    x = x_ref[...]
    tm = x.shape[0]
    h = _norm_mod(x, g_ref[...], sh_ref[...], sc_ref[...]).astype(BF16)
    v = _gelu(_dot(h, wv_ref[...]))
    mu = jnp.mean(v, axis=-1, keepdims=True)
    vc = v - mu
    var = jnp.mean(vc * vc, axis=-1, keepdims=True)
    vn = vc * lax.rsqrt(var + EPS) * vg_ref[...] + vb_ref[...]
    if emit_v:
        v_ref[...] = vn
    vn_scr[...] = vn.astype(BF16)
    gd = vn.shape[1] // groups
    r = lax.broadcasted_iota(jnp.int32, (lc, lc), 0)
    c = lax.broadcasted_iota(jnp.int32, (lc, lc), 1)
    mask = c <= r
    if period < lc:
        blk = ~(period - 1)
        mask = mask & ((r & blk) == (c & blk))
        ri = lax.broadcasted_iota(jnp.int32, (lc, LANES), 0)
        ci = lax.broadcasted_iota(jnp.int32, (lc, LANES), 1)
        sel = _ones_where((ri & (period - 1)) == ci)
    for g in range(groups):
        u = _gelu(_dot(h, wu_ref[:, g * gd:(g + 1) * gd]))
        if period < lc:
            rows_of_block = _dot(sel, ws_ref[g].astype(BF16)).astype(BF16)
            ws_full = _dot_nt(rows_of_block, sel)
        else:
            ws_full = ws_ref[g]
        wsm = jnp.where(mask, ws_full, 0.0).astype(BF16)
        bcol = bs_ref[:, g:g + 1]
        for ci in range(tm // lc):
            rows = slice(ci * lc, (ci + 1) * lc)
            mixed = _dot(wsm, vn_scr[rows, g * gd:(g + 1) * gd]) + bcol
            out_scr[rows, g * gd:(g + 1) * gd] = (u[rows] * mixed).astype(BF16)
    o_ref[...] = x + gt_ref[...] * _dot(out_scr[...], wo_ref[...])


def _gmlp_call(x, g, shift, scale, gate, wu, wv, vg, vb, ws, bs_t, wo, *, tm, tiles_per_seq, period, emit_v):
    n, d = x.shape
    da = wu.shape[1]
    groups = ws.shape[0]
    lc = bs_t.shape[0]
    const2 = lambda i: (0, 0)
    in_specs = [
        pl.BlockSpec((tm, d), lambda i: (i, 0)),
        pl.BlockSpec((1, d), const2),
        _mod_spec(shift, tiles_per_seq), _mod_spec(scale, tiles_per_seq), _mod_spec(gate, tiles_per_seq),
        pl.BlockSpec((d, da), const2), pl.BlockSpec((d, da), const2),
        pl.BlockSpec((1, da), const2), pl.BlockSpec((1, da), const2),
        pl.BlockSpec(ws.shape, lambda i: (0, 0, 0)),
        pl.BlockSpec((lc, groups), const2),
        pl.BlockSpec((da, d), const2),
    ]
    out_specs = [pl.BlockSpec((tm, d), lambda i: (i, 0))]
    out_shape = [jax.ShapeDtypeStruct((n, d), F32)]
    if emit_v:
        out_specs.append(pl.BlockSpec((tm, da), lambda i: (i, 0)))
        out_shape.append(jax.ShapeDtypeStruct((n, da), F32))
    return pl.pallas_call(
        functools.partial(_gmlp_kernel, lc=lc, period=period, groups=groups, emit_v=emit_v),
        grid=(n // tm,),
        in_specs=in_specs, out_specs=out_specs, out_shape=out_shape,
        scratch_shapes=[pltpu.VMEM((tm, da), BF16), pltpu.VMEM((tm, da), BF16)],
        compiler_params=_cparams("arbitrary"),
        name="gmlp",
    )(x, g, shift, scale, gate, wu, wv, vg, vb, ws, bs_t, wo)


def _ffn_kernel(x_ref, g_ref, sh_ref, sc_ref, gt_ref, wg_ref, wu_ref, wd_ref, o_ref, h_scr, acc_scr):
    j = pl.program_id(1)

    @pl.when(j == 0)
    def _():
        h_scr[...] = _norm_mod(x_ref[...], g_ref[...], sh_ref[...], sc_ref[...]).astype(BF16)
        acc_scr[...] = jnp.zeros_like(acc_scr)

    h = h_scr[...]
    a = _silu(_dot(h, wg_ref[...].astype(BF16))) * _dot(h, wu_ref[...].astype(BF16))
    acc_scr[...] += _dot(a.astype(BF16), wd_ref[...].astype(BF16))

    @pl.when(j == pl.num_programs(1) - 1)
    def _():
        o_ref[...] = x_ref[...] + gt_ref[...] * acc_scr[...]


def _ffn_call(x, g, shift, scale, gate, w_gu, w_down, *, tm, tiles_per_seq):
    n, d = x.shape
    f = w_down.shape[0]
    tf = min(TF_FFN, f)
    nf = f // tf
    const2 = lambda i, j: (0, 0)
    return pl.pallas_call(
        _ffn_kernel,
        grid=(n // tm, nf),
        in_specs=[
            pl.BlockSpec((tm, d), lambda i, j: (i, 0)),
            pl.BlockSpec((1, d), const2),
            _mod_spec(shift, tiles_per_seq), _mod_spec(scale, tiles_per_seq), _mod_spec(gate, tiles_per_seq),
            pl.BlockSpec((d, tf), lambda i, j: (0, j)),
            pl.BlockSpec((d, tf), lambda i, j: (0, nf + j)),
            pl.BlockSpec((tf, d), lambda i, j: (j, 0)),
        ],
        out_specs=pl.BlockSpec((tm, d), lambda i, j: (i, 0)),
        out_shape=jax.ShapeDtypeStruct((n, d), F32),
        scratch_shapes=[pltpu.VMEM((tm, d), BF16), pltpu.VMEM((tm, d), F32)],
        compiler_params=_cparams("arbitrary", "arbitrary"),
        name="ffn",
    )(x, g, shift, scale, gate, w_gu, w_gu, w_down)


def _log_sigmoid(x):
    return jnp.minimum(x, 0.0) - jnp.log(1.0 + jnp.exp(-jnp.abs(x)))


def _foxproj_kernel(x_ref, g_ref, sh_ref, sc_ref, w_ref, wf_ref, bf_ref, *rest, transposed, period,
                    tiles_per_seq, q_scale):
    if transposed:
        q_ref, k_ref, kb_ref, v_ref, vb_ref, lf_ref, cum_ref, carry_scr = rest
    else:
        q_ref, k_ref, v_ref, lf_ref, cum_ref, carry_scr = rest
    i = pl.program_id(0)
    x = x_ref[...]
    tm, d = x.shape
    h = _norm_mod(x, g_ref[...], sh_ref[...], sc_ref[...]).astype(BF16)
    if transposed:
        tq = q_ref.shape[-1]
        qt = (_dot_nt(w_ref[0:d, :], h) * q_scale).astype(BF16)
        for jq in range(tm // tq):
            q_ref[jq] = qt[:, jq * tq:(jq + 1) * tq]
        k_ref[...] = _dot_nt(w_ref[d:2 * d, :], h)
        kb_ref[...] = _dot_nt(h, w_ref[d:2 * d, :]).astype(BF16)
        vt = _dot_nt(w_ref[2 * d:3 * d, :], h)
        v_ref[...] = vt
        vb_ref[...] = vt.astype(BF16)
    else:
        q_ref[...] = _dot_nt(h, w_ref[0:d, :]) * q_scale
        k_ref[...] = _dot_nt(h, w_ref[d:2 * d, :])
        v_ref[...] = _dot_nt(h, w_ref[2 * d:3 * d, :])
    logf = _log_sigmoid(_dot_nt(wf_ref[...], h) + bf_ref[...])
    lf_ref[...] = logf
    s = lax.broadcasted_iota(jnp.int32, (tm, tm), 0)
    t = lax.broadcasted_iota(jnp.int32, (tm, tm), 1)
    upper = s <= t
    if period < tm:
        blk = ~(period - 1)
        upper = upper & ((s & blk) == (t & blk))
    upper = _ones_where(upper)
    hi, mid, lo = _split3(logf)
    cum = _dot(hi, upper) + _dot(mid, upper) + _dot(lo, upper)
    if tiles_per_seq > 1:
        @pl.when(i % tiles_per_seq == 0)
        def _():
            carry_scr[...] = jnp.zeros_like(carry_scr)
        cum = cum + carry_scr[:, 0:1]
        carry_scr[...] = jnp.broadcast_to(cum[:, tm - 1:tm], carry_scr.shape)
    cum_ref[...] = cum


def _foxproj_call(x, g, shift, scale, w_t, wf_t, b_f, *, tm, tiles_per_seq, transposed, period):
    n, d = x.shape
    heads = wf_t.shape[0]
    n_seq = n // (tm * tiles_per_seq)
    t_len = tm * tiles_per_seq
    const2 = lambda i: (0, 0)
    row = lambda i: (i, 0)
    seq_t = lambda i: (i // tiles_per_seq, 0, i % tiles_per_seq)
    in_specs = [
        pl.BlockSpec((tm, d), row),
        pl.BlockSpec((1, d), const2),
        _mod_spec(shift, tiles_per_seq), _mod_spec(scale, tiles_per_seq),
        pl.BlockSpec((3 * d, d), const2),
        pl.BlockSpec((heads, d), const2),
        pl.BlockSpec((heads, 1), const2),
    ]
    lf_spec = pl.BlockSpec((None, heads, tm), seq_t)
    lf_shape = jax.ShapeDtypeStruct((n_seq, heads, t_len), F32)
    if transposed:
        tq = min(TQ_ATTN, tm)
        kv_spec = pl.BlockSpec((None, d, tm), seq_t)
        q_spec = pl.BlockSpec((None, tm // tq, d, tq), lambda i: (i // tiles_per_seq, i % tiles_per_seq, 0, 0))
        out_specs = [q_spec, kv_spec, pl.BlockSpec((tm, d), row), kv_spec, kv_spec, lf_spec, lf_spec]
        out_shape = [jax.ShapeDtypeStruct((n_seq, t_len // tq, d, tq), BF16),
                     jax.ShapeDtypeStruct((n_seq, d, t_len), F32), jax.ShapeDtypeStruct((n, d), BF16),
                     jax.ShapeDtypeStruct((n_seq, d, t_len), F32), jax.ShapeDtypeStruct((n_seq, d, t_len), BF16),
                     lf_shape, lf_shape]
    else:
        out_specs = [pl.BlockSpec((tm, d), row)] * 3 + [lf_spec, lf_spec]
        out_shape = [jax.ShapeDtypeStruct((n, d), F32)] * 3 + [lf_shape, lf_shape]
    return pl.pallas_call(
        functools.partial(_foxproj_kernel, transposed=transposed, period=period, tiles_per_seq=tiles_per_seq,
                          q_scale=float(d // heads) ** -0.5 * (LOG2E if transposed else 1.0)),
        grid=(n // tm,),
        in_specs=in_specs, out_specs=out_specs, out_shape=out_shape,
        scratch_shapes=[pltpu.VMEM((heads, LANES), F32)],
        compiler_params=_cparams("arbitrary"),
        name="foxproj",
    )(x, g, shift, scale, w_t, wf_t, b_f)


def _attn_kernel(qt_ref, k_ref, vt_ref, cum_ref, o_ref, ck_scr, m_scr, l_scr, acc_scr, *, hd):
    nq, _, tq = qt_ref.shape
    lane = lax.broadcasted_iota(jnp.int32, (tq, LANES), 1)
    key = lax.broadcasted_iota(jnp.int32, (tq, tq), 0)
    qry = lax.broadcasted_iota(jnp.int32, (tq, tq), 1)
    eye = key == qry
    causal = key <= qry
    m_scr[...] = jnp.full_like(m_scr, NEG)
    l_scr[...] = jnp.zeros_like(l_scr)
    acc_scr[...] = jnp.zeros_like(acc_scr)

    def update(hh, qi, kh, vh, masked):
        t = _dot(kh, qt_ref[qi]) - ck_scr[...]
        if masked:
            t = jnp.where(causal, t, NEG)
        cq = cum_ref[hh, pl.ds(qi, 1), :] * LOG2E
        m_old = m_scr[hh, qi]
        m_new = jnp.maximum(m_old, jnp.max(t, axis=0, keepdims=True) + cq)
        alpha = jnp.exp2(m_old - m_new)
        p = jnp.exp2(t + (cq - m_new))
        l_scr[hh, qi] = alpha * l_scr[hh, qi] + jnp.sum(p, axis=0, keepdims=True)
        acc_scr[hh, qi] = alpha * acc_scr[hh, qi] + _dot(vh, p.astype(BF16))
        m_scr[hh, qi] = m_new

    for ki in range(nq):
        keys = slice(ki * tq, (ki + 1) * tq)
        k_blk = k_ref[keys, :]
        for hh in range(2):
            kh = jnp.where((lane >= hh * hd) & (lane < (hh + 1) * hd), k_blk, jnp.zeros_like(k_blk))
            vh = vt_ref[hh * hd:(hh + 1) * hd, keys]
            col = jnp.sum(jnp.where(eye, cum_ref[hh, ki:ki + 1, :], 0.0), axis=1, keepdims=True)
            ck_scr[...] = jnp.broadcast_to(col * LOG2E, ck_scr.shape)
            update(hh, ki, kh, vh, True)
            for qi in range(ki + 1, nq):
                update(hh, qi, kh, vh, False)

    for qi in range(nq):
        ot = jnp.concatenate([acc_scr[0, qi] / l_scr[0, qi], acc_scr[1, qi] / l_scr[1, qi]], axis=0)
        o_ref[qi * tq:(qi + 1) * tq, :] = ot.T.astype(o_ref.dtype)


def _attn_call(qt, k, vt, cum5, *, t_len, heads):
    n_seq, nq, d, tq = qt.shape
    hd = d // heads
    assert 2 * hd == LANES
    return pl.pallas_call(
        functools.partial(_attn_kernel, hd=hd),
        grid=(n_seq, heads // 2),
        in_specs=[
            pl.BlockSpec((None, nq, LANES, tq), lambda b, p: (b, 0, p, 0)),
            pl.BlockSpec((t_len, LANES), lambda b, p: (b, p)),
            pl.BlockSpec((None, LANES, t_len), lambda b, p: (b, p, 0)),
            pl.BlockSpec((None, None, 2, nq, tq), lambda b, p: (b, p, 0, 0, 0)),
        ],
        out_specs=pl.BlockSpec((t_len, LANES), lambda b, p: (b, p)),
        out_shape=jax.ShapeDtypeStruct((n_seq * t_len, d), BF16),
        scratch_shapes=[pltpu.VMEM((tq, tq), F32), pltpu.VMEM((2, nq, 1, tq), F32),
                        pltpu.VMEM((2, nq, 1, tq), F32), pltpu.VMEM((2, nq, hd, tq), F32)],
        compiler_params=_cparams("arbitrary", "arbitrary"),
        name="attn_prompt",
    )(qt, k, vt, cum5)


def _attn_sample_kernel(pt_ref, q_ref, cn_ref, kn_ref, vn_ref, *rest, pages, hd, page_size):
    k_refs = rest[0:pages]
    v_refs = rest[pages:2 * pages]
    lf_refs = rest[2 * pages:3 * pages]
    o_ref, qbd_scr, cn_scr, m_scr, l_scr, acc_scr, suf_scr = rest[3 * pages:]
    j = pl.program_id(1)
    tn, d = q_ref.shape
    heads = d // hd
    rows = heads * tn
    row = lax.broadcasted_iota(jnp.int32, (rows, LANES), 0)
    lane = lax.broadcasted_iota(jnp.int32, (rows, LANES), 1)
    q_of_row = row & (tn - 1)

    def rep(a):
        return jnp.broadcast_to(a[:, None, :], (heads, tn, a.shape[-1])).reshape(rows, a.shape[-1])

    @pl.when(j == 0)
    def _():
        q = q_ref[...]
        qrep = jnp.broadcast_to(q[None, :, :], (heads, tn, d)).reshape(rows, d)
        rr = lax.broadcasted_iota(jnp.int32, (rows, d), 0)
        cc = lax.broadcasted_iota(jnp.int32, (rows, d), 1)
        qbd_scr[...] = jnp.where(_shift_div(rr, tn) == _shift_div(cc, hd), qrep, 0.0).astype(BF16)
        cn_scr[...] = jnp.sum(jnp.where(lane == q_of_row, rep(cn_ref[...]), 0.0), axis=1, keepdims=True)
        m_scr[...] = jnp.full_like(m_scr, NEG)
        l_scr[...] = jnp.zeros_like(l_scr)
        acc_scr[...] = jnp.zeros_like(acc_scr)
        suf_scr[...] = jnp.zeros_like(suf_scr)

    s_idx = lax.broadcasted_iota(jnp.int32, (page_size, page_size), 0)
    t_idx = lax.broadcasted_iota(jnp.int32, (page_size, page_size), 1)
    later = _ones_where(s_idx > t_idx)
    carry = suf_scr[:, 0:1]
    scores = [None] * pages
    for i in reversed(range(pages)):
        lf = lf_refs[i][...]
        hi, mid, lo = _split3(lf)
        suf = _dot(hi, later) + _dot(mid, later) + _dot(lo, later) + carry
        carry = suf[:, 0:1] + lf[:, 0:1]
        scores[i] = _dot(qbd_scr[...], k_refs[i][...].astype(BF16)) + rep(suf)
    suf_scr[...] = jnp.broadcast_to(carry, suf_scr.shape)
    s = jnp.concatenate(scores, axis=1) + cn_scr[...]
    m_old = m_scr[...]
    m_new = jnp.maximum(m_old, jnp.max(s, axis=1, keepdims=True))
    alpha = jnp.exp(m_old - m_new)
    p = jnp.exp(s - m_new)
    l_scr[...] = alpha * l_scr[...] + jnp.sum(p, axis=1, keepdims=True)
    pb = p.astype(BF16)
    pv = _dot_nt(pb[:, 0:page_size], v_refs[0][...].astype(BF16))
    for i in range(1, pages):
        pv = pv + _dot_nt(pb[:, i * page_size:(i + 1) * page_size], v_refs[i][...].astype(BF16))
    acc_scr[...] = alpha * acc_scr[...] + pv
    m_scr[...] = m_new

    @pl.when(j == pl.num_programs(1) - 1)
    def _():
        pad = jnp.zeros((LANES - tn, d), BF16)
        k_new = jnp.concatenate([kn_ref[...].astype(BF16), pad], axis=0)
        v_new = jnp.concatenate([vn_ref[...].astype(BF16), pad], axis=0)
        s = _dot_nt(qbd_scr[...], k_new) + cn_scr[...] - rep(cn_ref[...])
        s = jnp.where(lane <= q_of_row, s, NEG)
        m_old = m_scr[...]
        m_new = jnp.maximum(m_old, jnp.max(s, axis=1, keepdims=True))
        alpha = jnp.exp(m_old - m_new)
        p = jnp.exp(s - m_new)
        l_new = alpha * l_scr[...] + jnp.sum(p, axis=1, keepdims=True)
        o = (alpha * acc_scr[...] + _dot(p.astype(BF16), v_new)) / l_new
        rr = lax.broadcasted_iota(jnp.int32, (rows, d), 0)
        cc = lax.broadcasted_iota(jnp.int32, (rows, d), 1)
        o = jnp.where(_shift_div(rr, tn) == _shift_div(cc, hd), o, 0.0)
        o_ref[...] = jnp.sum(o.reshape(heads, tn, d), axis=0)


def _attn_sample_call(page_table, q, cn_pad, k_new, v_new, kc_t, vc_t, lfc_t, *, tn, heads):
    n, d = q.shape
    n_seq, n_pages = page_table.shape
    page_size = kc_t.shape[-1]
    pages = min(PAGES_PER_STEP, n_pages)
    n_steps = n_pages // pages
    hd = d // heads

    def page_map(i):
        def index(b, j, pt):
            return (pt[b * n_pages + (n_steps - 1 - j) * pages + i], 0, 0)
        return index

    row = lambda b, j, pt: (b, 0)
    in_specs = [
        pl.BlockSpec((tn, d), row),
        pl.BlockSpec((None, heads, LANES), lambda b, j, pt: (b, 0, 0)),
        pl.BlockSpec((tn, d), row), pl.BlockSpec((tn, d), row),
    ]
    in_specs += [pl.BlockSpec((None, d, page_size), page_map(i)) for i in range(pages)]
    in_specs += [pl.BlockSpec((None, d, page_size), page_map(i)) for i in range(pages)]
    in_specs += [pl.BlockSpec((None, heads, page_size), page_map(i)) for i in range(pages)]
    rows = heads * tn
    return pl.pallas_call(
        functools.partial(_attn_sample_kernel, pages=pages, hd=hd, page_size=page_size),
        grid_spec=pltpu.PrefetchScalarGridSpec(
            num_scalar_prefetch=1,
            grid=(n_seq, n_steps),
            in_specs=in_specs,
            out_specs=pl.BlockSpec((tn, d), row),
            scratch_shapes=[pltpu.VMEM((rows, d), BF16), pltpu.VMEM((rows, 1), F32), pltpu.VMEM((rows, 1), F32),
                            pltpu.VMEM((rows, 1), F32), pltpu.VMEM((rows, d), F32),
                            pltpu.VMEM((heads, LANES), F32)],
        ),
        out_shape=jax.ShapeDtypeStruct((n, d), F32),
        compiler_params=_cparams("arbitrary", "arbitrary"),
        name="attn_sample",
    )(page_table.reshape(-1), q, cn_pad, k_new, v_new, *([kc_t] * pages), *([vc_t] * pages), *([lfc_t] * pages))


def _post_kernel(op_ref, os_ref, wo_ref, xp_ref, xs_ref, gtp_ref, gts_ref, g_ref, shp_ref, shs_ref, scp_ref,
                 scs_ref, wr_ref, br_ref, x3_ref, h_ref, meta_ref, metat_ref, blk_ref, cout_ref, carry_scr,
                 *, n_experts, n_prompt_blocks):
    i = pl.program_id(0)
    tm = xp_ref.shape[0]
    is_p = i < n_prompt_blocks

    @pl.when(i == 0)
    def _():
        carry_scr[...] = jnp.zeros_like(carry_scr)

    o = jnp.where(is_p, op_ref[...], os_ref[...].astype(BF16))
    x = jnp.where(is_p, xp_ref[...], xs_ref[...])
    gate = jnp.where(is_p, gtp_ref[...], gts_ref[...])
    shift = jnp.where(is_p, shp_ref[...], shs_ref[...])
    scale = jnp.where(is_p, scp_ref[...], scs_ref[...])
    x3 = x + gate * _dot(o, wo_ref[...])
    x3_ref[...] = x3
    h = _norm_mod(x3, g_ref[...], shift, scale)
    h_hi = h.astype(BF16)
    h_ref[...] = h_hi
    h_lo = (h - h_hi.astype(F32)).astype(BF16)
    wr = wr_ref[...]
    w_hi = wr.astype(BF16)
    w_lo = (wr - w_hi.astype(F32)).astype(BF16)
    logits = _dot(h_hi, w_hi) + _dot(h_hi, w_lo) + _dot(h_lo, w_hi) + br_ref[...]
    lane_i = lax.broadcasted_iota(jnp.int32, (tm, LANES), 1)
    lane = lane_i.astype(F32)
    logits = jnp.where(lane_i < n_experts, logits, NEG)
    l1 = jnp.max(logits, axis=1, keepdims=True)
    i1 = jnp.min(jnp.where(logits == l1, lane, float(LANES)), axis=1, keepdims=True)
    rest = jnp.where(lane == i1, NEG, logits)
    l2 = jnp.max(rest, axis=1, keepdims=True)
    i2 = jnp.min(jnp.where(rest == l2, lane, float(LANES)), axis=1, keepdims=True)
    e = jnp.exp(l2 - l1)
    g1 = 1.0 / (1.0 + e)
    g2 = e / (1.0 + e)
    onehot = jnp.where((lane == i1) | (lane == i2), 1.0, 0.0)
    r = lax.broadcasted_iota(jnp.int32, (tm, tm), 0)
    c = lax.broadcasted_iota(jnp.int32, (tm, tm), 1)
    before = _ones_where(c < r)
    carry = carry_scr[0:1, :]
    blk_ref[...] = carry
    prefix = _dot(before, onehot.astype(BF16)) + carry
    r1 = jnp.sum(jnp.where(lane == i1, prefix, 0.0), axis=1, keepdims=True)
    r2 = jnp.sum(jnp.where(lane == i2, prefix, 0.0), axis=1, keepdims=True)
    cols = (i1, i2, g1, g2, r1, r2)
    meta = jnp.zeros((tm, LANES), F32)
    for k, col in enumerate(cols):
        meta = jnp.where(lane_i == k, col, meta)
    meta_ref[...] = meta
    metat_ref[...] = meta.T[0:SUBLANES, :]
    carry = carry + jnp.sum(onehot, axis=0, keepdims=True)
    carry_scr[...] = jnp.broadcast_to(carry, carry_scr.shape)
    cout_ref[...] = carry


def _post_call(o_p, o_s, wo, x_p, x_s, gate_p, gate_s, g, shift_p, shift_s, scale_p, scale_s, wr_pad, br_pad,
               *, tm, blocks_per_seq, n_experts):
    n_p, d = x_p.shape
    n_s = x_s.shape[0]
    npb = n_p // tm
    nb = npb + n_s // tm
    n = n_p + n_s
    const2 = lambda i: (0, 0)
    row = lambda i: (i, 0)
    p_row = lambda i: (jnp.minimum(i, npb - 1), 0)
    s_row = lambda i: (jnp.maximum(i - npb, 0), 0)
    p_mod = pl.BlockSpec((None, 1, d), lambda i: (jnp.minimum(i, npb - 1) // blocks_per_seq, 0, 0))
    s_mod = pl.BlockSpec((None, tm, d), lambda i: (jnp.maximum(i - npb, 0), 0, 0))
    return pl.pallas_call(
        functools.partial(_post_kernel, n_experts=n_experts, n_prompt_blocks=npb),
        grid=(nb,),
        in_specs=[
            pl.BlockSpec((tm, d), p_row), pl.BlockSpec((tm, d), s_row),
            pl.BlockSpec((d, d), const2),
            pl.BlockSpec((tm, d), p_row), pl.BlockSpec((tm, d), s_row),
            p_mod, s_mod,
            pl.BlockSpec((1, d), const2),
            p_mod, s_mod, p_mod, s_mod,
            pl.BlockSpec((d, LANES), const2),
            pl.BlockSpec((1, LANES), const2),
        ],
        out_specs=[
            pl.BlockSpec((tm, d), row), pl.BlockSpec((tm, d), row), pl.BlockSpec((tm, LANES), row),
            pl.BlockSpec((None, SUBLANES, tm), lambda i: (i, 0, 0)),
            pl.BlockSpec((None, 1, LANES), lambda i: (i, 0, 0)),
            pl.BlockSpec((1, LANES), const2),
        ],
        out_shape=[
            jax.ShapeDtypeStruct((n, d), F32), jax.ShapeDtypeStruct((n, d), BF16),
            jax.ShapeDtypeStruct((n, LANES), F32),
            jax.ShapeDtypeStruct((nb, SUBLANES, tm), F32),
            jax.ShapeDtypeStruct((nb, 1, LANES), F32),
            jax.ShapeDtypeStruct((1, LANES), F32),
        ],
        scratch_shapes=[pltpu.VMEM((SUBLANES, LANES), F32)],
        compiler_params=_cparams("arbitrary"),
        name="post_attn_router",
    )(o_p, o_s, wo, x_p, x_s, gate_p, gate_s, g, shift_p, shift_s, scale_p, scale_s, wr_pad, br_pad)


def _sorted_pos(idx, rank, start_ref, n_experts):
    pos = rank
    for e in range(n_experts):
        pos = pos + jnp.where(idx == float(e), start_ref[e].astype(F32), 0.0)
    return pos


def _dispatch_kernel(lo_ref, hi_ref, start_ref, mt_ref, h_ref, xs_ref, gs_ref, acc_scr, gacc_scr,
                     *, tb, n_experts):
    j = pl.program_id(0)
    sub = xs_ref.shape[0]
    nb = mt_ref.shape[0]
    lo = lo_ref[j]
    hi = hi_ref[j]
    acc_scr[...] = jnp.zeros_like(acc_scr)
    gacc_scr[...] = jnp.zeros_like(gacc_scr)

    def hits(b, width):
        dest = (j * sub + lax.broadcasted_iota(jnp.int32, (sub, width * tb), 0)).astype(F32)
        mts = [mt_ref[b + w] for w in range(width)]
        field = lambda r: jnp.concatenate([m[r:r + 1, :] for m in mts], axis=1)
        hit0 = _sorted_pos(field(0), field(4), start_ref, n_experts) == dest
        hit1 = _sorted_pos(field(1), field(5), start_ref, n_experts) == dest
        gate = jnp.where(hit0, field(2), 0.0) + jnp.where(hit1, field(3), 0.0)
        return _ones_where(hit0 | hit1), gate

    n_even = nb - nb % 2

    def pair(m, carry):
        b = 2 * m
        oh, gate = hits(b, 2)
        rows = pl.ds(pl.multiple_of(b * tb, 2 * tb), 2 * tb)
        acc_scr[...] += _dot(oh, h_ref[rows, :])
        gacc_scr[...] += gate
        return carry

    lax.fori_loop(lo // 2, (jnp.minimum(hi, n_even) + 1) // 2, pair, 0)

    if nb % 2:
        @pl.when(hi == nb)
        def _():
            oh, gate = hits(nb - 1, 1)
            acc_scr[...] += _dot(oh, h_ref[(nb - 1) * tb:nb * tb, :])
            gacc_scr[:, 0:tb] += gate

    xs_ref[...] = acc_scr[...].astype(BF16)
    gs_ref[...] = jnp.broadcast_to(jnp.sum(gacc_scr[...], axis=1, keepdims=True), gs_ref.shape)


def _dispatch_call(tile_lo, tile_hi, start, meta_t, h, *, n_sorted, n_experts):
    n, d = h.shape
    nb, _, tb = meta_t.shape
    whole = lambda nd: (lambda j, lo, hi, st: (0,) * nd)
    once = pl.Buffered(1)
    return pl.pallas_call(
        functools.partial(_dispatch_kernel, tb=tb, n_experts=n_experts),
        grid_spec=pltpu.PrefetchScalarGridSpec(
            num_scalar_prefetch=3,
            grid=(n_sorted // SUB,),
            in_specs=[
                pl.BlockSpec((nb, SUBLANES, tb), whole(3), pipeline_mode=once),
                pl.BlockSpec((n, d), whole(2), pipeline_mode=once),
            ],
            out_specs=[pl.BlockSpec((SUB, d), lambda j, lo, hi, st: (j, 0)),
                       pl.BlockSpec((SUB, LANES), lambda j, lo, hi, st: (j, 0))],
            scratch_shapes=[pltpu.VMEM((SUB, d), F32), pltpu.VMEM((SUB, 2 * tb), F32)],
        ),
        out_shape=[jax.ShapeDtypeStruct((n_sorted, d), BF16), jax.ShapeDtypeStruct((n_sorted, LANES), F32)],
        compiler_params=_cparams("arbitrary"),
        name="moe_dispatch",
    )(tile_lo, tile_hi, start, meta_t, h)


def _moe_kernel(ex_ref, nsub_ref, x_ref, gs_ref, wg_ref, wu_ref, wd_ref, y_ref, acc_scr, wgb_scr, wub_scr, wdb_scr):
    s = pl.program_id(0)
    j = pl.program_id(1)
    nsub = nsub_ref[s]
    last = j == pl.num_programs(1) - 1

    @pl.when(nsub > 0)
    def _():
        wgb_scr[...] = wg_ref[...].astype(BF16)
        wub_scr[...] = wu_ref[...].astype(BF16)
        wdb_scr[...] = wd_ref[...].astype(BF16)

    n_steps = (nsub * SUB + FFN_ROWS - 1) // FFN_ROWS

    def step_rows(t):
        return pl.ds(pl.multiple_of(t * FFN_ROWS, FFN_ROWS), FFN_ROWS)

    @pl.when(j == 0)
    def _():
        def zero(t, carry):
            acc_scr[step_rows(t), :] = jnp.zeros((FFN_ROWS, acc_scr.shape[1]), F32)
            return carry
        lax.fori_loop(0, n_steps, zero, 0)

    def body(t, carry):
        rows = step_rows(t)
        x = x_ref[rows, :]
        a = _silu(_dot(x, wgb_scr[...])) * _dot(x, wub_scr[...])
        acc_scr[rows, :] += _dot(a.astype(BF16), wdb_scr[...])
        return carry

    lax.fori_loop(0, n_steps, body, 0)

    @pl.when(last)
    def _():
        def emit(t, carry):
            rows = step_rows(t)
            y_ref[rows, :] = (acc_scr[rows, :] * gs_ref[rows, 0:1]).astype(BF16)
            return carry
        lax.fori_loop(0, n_steps, emit, 0)

        def zero(t, carry):
            y_ref[step_rows(t), :] = jnp.zeros((FFN_ROWS, y_ref.shape[1]), BF16)
            return carry
        lax.fori_loop(n_steps, y_ref.shape[0] // FFN_ROWS, zero, 0)


def _moe_call(sup_expert, sup_nsub, xs, gs, w_gu, w_down, *, ts):
    n_sorted, d = xs.shape
    f = w_down.shape[1]
    tf = min(TF_FFN, f)
    nf = f // tf
    n_super = n_sorted // ts

    def jj(s, j, nsub):
        return jnp.where(nsub[s] > 0, j, nf - 1)

    return pl.pallas_call(
        _moe_kernel,
        grid_spec=pltpu.PrefetchScalarGridSpec(
            num_scalar_prefetch=2,
            grid=(n_super, nf),
            in_specs=[
                pl.BlockSpec((ts, d), lambda s, j, ex, ns: (s, 0)),
                pl.BlockSpec((ts, LANES), lambda s, j, ex, ns: (s, 0)),
                pl.BlockSpec((None, d, tf), lambda s, j, ex, ns: (ex[s], 0, jj(s, j, ns))),
                pl.BlockSpec((None, d, tf), lambda s, j, ex, ns: (ex[s], 0, nf + jj(s, j, ns))),
                pl.BlockSpec((None, tf, d), lambda s, j, ex, ns: (ex[s], jj(s, j, ns), 0)),
            ],
            out_specs=pl.BlockSpec((ts, d), lambda s, j, ex, ns: (s, 0)),
            scratch_shapes=[pltpu.VMEM((ts, d), F32), pltpu.VMEM((d, tf), BF16), pltpu.VMEM((d, tf), BF16),
                            pltpu.VMEM((tf, d), BF16)],
        ),
        out_shape=jax.ShapeDtypeStruct((n_sorted, d), BF16),
        compiler_params=_cparams("arbitrary", "arbitrary"),
        name="moe_ffn",
    )(sup_expert, sup_nsub, xs, gs, w_gu, w_gu, w_down)


def _combine_kernel(tile_ref, need2_ref, start_ref, meta_ref, x_ref, gtp_ref, gts_ref, gf_ref, *rest,
                    n_experts, n_prompt_blocks):
    y_refs = rest[0:2 * n_experts]
    op_ref, os_ref, acc_scr = rest[2 * n_experts:]
    b = pl.program_id(0)
    tb = acc_scr.shape[0]
    meta = meta_ref[...]
    col = lax.broadcasted_iota(jnp.int32, (tb, SUB), 1)

    def window(e, w):
        first = start_ref[e].astype(F32)
        p0 = jnp.where(meta[:, 0:1] == float(e), meta[:, 4:5] + first, -1.0)
        p1 = jnp.where(meta[:, 1:2] == float(e), meta[:, 5:6] + first, -1.0)
        src = ((tile_ref[b * n_experts + e] + w) * SUB + col).astype(F32)
        return _ones_where((p0 == src) | (p1 == src))

    total = _dot(window(0, 0), y_refs[0][...])
    for e in range(1, n_experts):
        total = total + _dot(window(e, 0), y_refs[2 * e][...])
    acc_scr[...] = total
    for e in range(n_experts):
        @pl.when(need2_ref[b * n_experts + e] > 0)
        def _(e=e):
            acc_scr[...] += _dot(window(e, 1), y_refs[2 * e + 1][...])

    is_p = b < n_prompt_blocks
    x4 = x_ref[...] + jnp.where(is_p, gtp_ref[...], gts_ref[...]) * acc_scr[...]
    ms = jnp.mean(x4 * x4, axis=-1, keepdims=True)
    y = x4 * lax.rsqrt(ms + EPS) * gf_ref[...]

    @pl.when(is_p)
    def _():
        op_ref[...] = y

    @pl.when(jnp.logical_not(is_p))
    def _():
        os_ref[...] = y


def _combine_call(win_tile, win_need2, start, meta, x, gate_p, gate_s, g_final, ys, *, n_p, tb, blocks_per_seq,
                  n_experts):
    n, d = x.shape
    n_s = n - n_p
    npb = n_p // tb
    nb = n // tb

    def pb(b):
        return jnp.minimum(b, npb - 1)

    def sb(b):
        return jnp.maximum(b - npb, 0)

    def win_map(e, w):
        if w == 0:
            return lambda b, tile, need, st: (tile[b * n_experts + e], 0)
        return lambda b, tile, need, st: (jnp.where(need[b * n_experts + e] > 0, tile[b * n_experts + e] + 1, 0), 0)

    in_specs = [
        pl.BlockSpec((tb, LANES), lambda b, tile, need, st: (b, 0)),
        pl.BlockSpec((tb, d), lambda b, tile, need, st: (b, 0)),
        pl.BlockSpec((None, 1, d), lambda b, tile, need, st: (pb(b) // blocks_per_seq, 0, 0)),
        pl.BlockSpec((None, tb, d), lambda b, tile, need, st: (sb(b), 0, 0)),
        pl.BlockSpec((1, d), lambda b, tile, need, st: (0, 0)),
    ]
    in_specs += [pl.BlockSpec((SUB, d), win_map(e, w)) for e in range(n_experts) for w in range(2)]
    return pl.pallas_call(
        functools.partial(_combine_kernel, n_experts=n_experts, n_prompt_blocks=npb),
        grid_spec=pltpu.PrefetchScalarGridSpec(
            num_scalar_prefetch=3,
            grid=(nb,),
            in_specs=in_specs,
            out_specs=[pl.BlockSpec((tb, d), lambda b, tile, need, st: (pb(b), 0)),
                       pl.BlockSpec((tb, d), lambda b, tile, need, st: (sb(b), 0))],
            scratch_shapes=[pltpu.VMEM((tb, d), F32)],
        ),
        out_shape=[jax.ShapeDtypeStruct((n_p, d), F32), jax.ShapeDtypeStruct((n_s, d), F32)],
        compiler_params=_cparams("arbitrary"),
        name="moe_combine",
    )(win_tile, win_need2, start, meta, x, gate_p, gate_s, g_final, *([ys] * (2 * n_experts)))


def _routing_tables(blk_cnt, counts, *, n_experts, ts, n_super):
    nb = blk_cnt.shape[0]
    cnt = counts[0, :n_experts].astype(jnp.int32)
    n_sup_e = (cnt + ts - 1) // ts
    sup_start = jnp.cumsum(n_sup_e) - n_sup_e
    start = sup_start * ts

    s_ids = jnp.arange(n_super, dtype=jnp.int32)
    used = jnp.sum(n_sup_e)
    sup_end = sup_start + n_sup_e
    sup_e = jnp.sum((s_ids[:, None] >= sup_end[None, :]).astype(jnp.int32), axis=1)
    sup_e = jnp.clip(sup_e, 0, n_experts - 1)
    last_e = jnp.max(jnp.where(cnt > 0, jnp.arange(n_experts, dtype=jnp.int32), 0))
    sup_e = jnp.where(s_ids < used, sup_e, last_e)
    rows_in = jnp.clip(cnt[sup_e] - (s_ids - sup_start[sup_e]) * ts, 0, ts)
    sup_nsub = jnp.where(s_ids < used, (rows_in + SUB - 1) // SUB, 0).astype(jnp.int32)

    blk = blk_cnt[:, 0, :n_experts].astype(jnp.int32)
    blk_end = jnp.concatenate([blk[1:], cnt[None, :]], axis=0)
    n_tiles = n_super * (ts // SUB)
    t_ids = jnp.arange(n_tiles, dtype=jnp.int32)
    t_e = sup_e[t_ids // (ts // SUB)]
    t_r0 = t_ids * SUB - start[t_e]
    t_active = ((t_ids // (ts // SUB)) < used) & (t_r0 < cnt[t_e]) & (t_r0 >= 0)
    t_r1 = jnp.minimum(t_r0 + SUB, cnt[t_e])
    be = blk_end[:, t_e]
    bs = blk[:, t_e]
    overlap = (be > t_r0[None, :]) & (bs < t_r1[None, :])
    b_ids = jnp.arange(nb, dtype=jnp.int32)[:, None]
    lo = jnp.min(jnp.where(overlap, b_ids, nb), axis=0)
    hi = jnp.max(jnp.where(overlap, b_ids + 1, 0), axis=0)
    tile_lo = jnp.where(t_active, lo, 0).astype(jnp.int32)
    tile_hi = jnp.where(t_active, hi, 0).astype(jnp.int32)

    first = start[None, :] + blk
    n_be = blk_end - blk
    win_tile = (first // SUB).astype(jnp.int32)
    win_need2 = ((first + n_be) > (win_tile + 1) * SUB).astype(jnp.int32)

    return dict(start=start.astype(jnp.int32), sup_e=sup_e, sup_nsub=sup_nsub, tile_lo=tile_lo,
                tile_hi=tile_hi, win_tile=win_tile.reshape(-1), win_need2=win_need2.reshape(-1))


def kernel(x_prompt, x_sample, c_prompt, c_sample, cache_k, cache_v, cache_logf, page_table, norm_mix_g, norm_ffn_g, final_norm_g, ada_w, ada_b, gmlp_w_in, gmlp_v_g, gmlp_v_b, gmlp_w_s, gmlp_b_s, gmlp_w_out, fox_w_in, fox_b_f, fox_w_out, ffn_w_gu, ffn_w_down, moe_w_r, moe_b_r, moe_w_gu, moe_w_down):
    n_seq_p, t_len, d = x_prompt.shape
    n_seq_s, t_new, _ = x_sample.shape
    n_p = n_seq_p * t_len
    n_s = n_seq_s * t_new
    heads = N_HEADS
    hd = d // heads
    n_experts = moe_w_r.shape[-1]
    da = gmlp_w_out.shape[1]

    xp = x_prompt.reshape(n_p, d)
    xs = x_sample.reshape(n_s, d)

    mod = _ada_call(jnp.concatenate([c_prompt, c_sample], axis=0), ada_w, ada_b)

    def mods(layer):
        mp = [mod[layer, :n_seq_p, c * d:(c + 1) * d].reshape(n_seq_p, 1, d) for c in range(6)]
        ms = [jnp.repeat(mod[layer, n_seq_p:, c * d:(c + 1) * d], t_new, axis=0).reshape(1, n_s, d)
              for c in range(6)]
        return mp, ms

    row = lambda a: a.reshape(1, -1)

    mp, ms = mods(0)
    w_in = gmlp_w_in[0].astype(BF16)
    wu, wv = w_in[:, :da], w_in[:, da:]
    wo = gmlp_w_out[0].astype(BF16)
    vg, vb = row(gmlp_v_g[0]), row(gmlp_v_b[0])
    lc = min(GMLP_CHUNK, t_len)
    tm = min(TM_GMLP, t_len)
    xp = _gmlp_call(xp, row(norm_mix_g[0]), mp[0], mp[1], mp[2], wu, wv, vg, vb,
                    gmlp_w_s[0][:, :lc, :lc], gmlp_b_s[0][:, :lc].T, wo,
                    tm=tm, tiles_per_seq=t_len // tm, period=lc, emit_v=False)[0]
    reps = n_s // t_new
    ws_s = jnp.zeros((gmlp_w_s.shape[1], LANES, LANES), F32).at[:, :t_new, :t_new].set(
        gmlp_w_s[0][:, :t_new, :t_new])
    bs_s = jnp.tile(gmlp_b_s[0][:, :t_new], (1, reps)).T
    xs, v_rows = _gmlp_call(xs, row(norm_mix_g[0]), ms[0], ms[1], ms[2], wu, wv, vg, vb, ws_s, bs_s, wo,
                            tm=n_s, tiles_per_seq=1, period=t_new, emit_v=True)
    tm = min(TM_FFN, t_len)
    xp = _ffn_call(xp, row(norm_ffn_g[0]), mp[3], mp[4], mp[5], ffn_w_gu[0], ffn_w_down[0],
                   tm=tm, tiles_per_seq=t_len // tm)
    xs = _ffn_call(xs, row(norm_ffn_g[0]), ms[3], ms[4], ms[5], ffn_w_gu[0], ffn_w_down[0],
                   tm=n_s, tiles_per_seq=1)

    mp, ms = mods(1)
    w_t = fox_w_in[0].T.astype(BF16)
    wqkv_t, wf_t = w_t[:3 * d], w_t[3 * d:]
    b_f = fox_b_f[0].reshape(heads, 1)
    tm = min(TM_PROJ, t_len)
    qt_p, kt_p, kb_p, vt_p, vtb_p, lft_p, cumt_p = _foxproj_call(
        xp, row(norm_mix_g[1]), mp[0], mp[1], wqkv_t, wf_t, b_f,
        tm=tm, tiles_per_seq=t_len // tm, transposed=True, period=tm)
    q_s, k_s, v_s, lft_s, cumt_s = _foxproj_call(
        xs, row(norm_mix_g[1]), ms[0], ms[1], wqkv_t, wf_t, b_f,
        tm=n_s, tiles_per_seq=1, transposed=False, period=t_new)

    nq, tq = qt_p.shape[1], qt_p.shape[3]
    o_p = _attn_call(qt_p, kb_p, vtb_p, cumt_p.reshape(n_seq_p, heads // 2, 2, nq, tq), t_len=t_len, heads=heads)

    page_size = cache_k.shape[2]
    n_phys = cache_k.shape[1]
    kc_t = jnp.transpose(cache_k[0], (0, 2, 3, 1)).reshape(n_phys, d, page_size)
    vc_t = jnp.transpose(cache_v[0], (0, 2, 3, 1)).reshape(n_phys, d, page_size)
    lfc_t = jnp.transpose(cache_logf[0], (0, 2, 1))
    cn = cumt_s[0].reshape(heads, n_seq_s, t_new).transpose(1, 0, 2)
    cn_pad = jnp.zeros((n_seq_s, heads, LANES), F32).at[:, :, :t_new].set(cn)
    o_s = _attn_sample_call(page_table, q_s, cn_pad, k_s, v_s, kc_t, vc_t, lfc_t, tn=t_new, heads=heads)

    wo_f = fox_w_out[0].astype(BF16)
    wr_pad = jnp.zeros((d, LANES), F32).at[:, :n_experts].set(moe_w_r[0])
    br_pad = jnp.zeros((1, LANES), F32).at[0, :n_experts].set(moe_b_r[0])
    tb = min(TM_POST, t_len)
    ms_blk = [m.reshape(n_s // tb, tb, d) for m in ms]
    x3, h_all, meta, meta_t, blk_cnt, counts = _post_call(
        o_p, o_s, wo_f, xp, xs, mp[2], ms_blk[2], row(norm_ffn_g[1]), mp[3], ms_blk[3], mp[4], ms_blk[4],
        wr_pad, br_pad, tm=tb, blocks_per_seq=t_len // tb, n_experts=n_experts)

    n_tok = n_p + n_s
    ts = TS_MOE
    n_super = (TOP_K * n_tok) // ts + n_experts + 1
    rt = _routing_tables(blk_cnt, counts, n_experts=n_experts, ts=ts, n_super=n_super)
    xs_sorted, g_sorted = _dispatch_call(rt["tile_lo"], rt["tile_hi"], rt["start"], meta_t, h_all,
                                         n_sorted=n_super * ts, n_experts=n_experts)
    ys_sorted = _moe_call(rt["sup_e"], rt["sup_nsub"], xs_sorted, g_sorted, moe_w_gu[0], moe_w_down[0], ts=ts)
    y_p, y_s = _combine_call(rt["win_tile"], rt["win_need2"], rt["start"], meta, x3, mp[5], ms_blk[5],
                             row(final_norm_g), ys_sorted,
                             n_p=n_p, tb=tb, blocks_per_seq=t_len // tb, n_experts=n_experts)

    y_prompt = y_p.reshape(n_seq_p, t_len, d)
    y_sample = y_s.reshape(n_seq_s, t_new, d)
    state_a_v_sample = v_rows.reshape(1, n_seq_s, t_new, da)
    k_prompt = kt_p.reshape(1, n_seq_p, heads, hd, t_len).transpose(0, 1, 4, 2, 3)
    v_prompt = vt_p.reshape(1, n_seq_p, heads, hd, t_len).transpose(0, 1, 4, 2, 3)
    logf_prompt = lft_p.transpose(0, 2, 1)[None]
    k_sample = k_s.reshape(1, n_seq_s, t_new, heads, hd)
    v_sample = v_s.reshape(1, n_seq_s, t_new, heads, hd)
    logf_sample = lft_s[0].T.reshape(1, n_seq_s, t_new, heads)
    return (y_prompt, y_sample, state_a_v_sample, k_prompt, v_prompt, logf_prompt, k_sample, v_sample, logf_sample)
```

```python
import functools

import jax
import jax.numpy as jnp
from jax import lax
from jax.experimental import pallas as pl
from jax.experimental.pallas import tpu as pltpu

F32 = jnp.float32
BF16 = jnp.bfloat16

N_HEADS = 16
GMLP_GROUPS = 8
GMLP_CHUNK = 128
TOP_K = 2
EPS = 1e-6
NEG = -1e30
LOG2E = 1.4426950408889634

LANES = 128
SUBLANES = 8
MXU_TILE = 256
VMEM_LIMIT = 56 * 1024 * 1024

TM_GMLP = 512
TM_FFN = 1024
TF_FFN = 512
TM_PROJ = 512
TQ_ATTN = 512
TM_POST = 256
SUB = 256
FFN_ROWS = 2 * SUB
TS_MOE = 2048
PAGES_PER_STEP = 16


def _cparams(*sem):
    return pltpu.CompilerParams(dimension_semantics=sem, vmem_limit_bytes=VMEM_LIMIT)


def _dot(a, b):
    return jnp.dot(a, b, preferred_element_type=F32)


def _dot_nt(a, b):
    return lax.dot_general(a, b, (((1,), (1,)), ((), ())), preferred_element_type=F32)


def _norm_mod(x, g, shift, scale):
    ms = jnp.mean(x * x, axis=-1, keepdims=True)
    y = x * lax.rsqrt(ms + EPS) * g
    return y * (1.0 + scale) + shift


def _gelu(x):
    return 0.5 * x * (1.0 + lax.erf(x * (2.0 ** -0.5)))


def _silu(x):
    return x * jax.nn.sigmoid(x)


def _ones_where(cond):
    return jnp.where(cond, 1.0, 0.0).astype(BF16)


def _shift_div(x, c):
    assert c & (c - 1) == 0
    return lax.shift_right_logical(x, c.bit_length() - 1)


def _split3(x):
    hi = x.astype(BF16)
    r = x - hi.astype(F32)
    mid = r.astype(BF16)
    lo = (r - mid.astype(F32)).astype(BF16)
    return hi, mid, lo


def _ada_kernel(c_ref, w_ref, b_ref, o_ref):
    s = _silu(c_ref[...]).astype(BF16)
    o_ref[...] = _dot(s, w_ref[...].astype(BF16)) + b_ref[...]


def _ada_call(c_all, ada_w, ada_b):
    n_layers, d, d6 = ada_w.shape
    r = c_all.shape[0]
    tn = min(d6, 1536)
    return pl.pallas_call(
        _ada_kernel,
        grid=(n_layers, d6 // tn),
        in_specs=[
            pl.BlockSpec((r, d), lambda l, j: (0, 0)),
            pl.BlockSpec((None, d, tn), lambda l, j: (l, 0, j)),
            pl.BlockSpec((None, 1, tn), lambda l, j: (l, 0, j)),
        ],
        out_specs=pl.BlockSpec((None, r, tn), lambda l, j: (l, 0, j)),
        out_shape=jax.ShapeDtypeStruct((n_layers, r, d6), F32),
        compiler_params=_cparams("arbitrary", "arbitrary"),
        name="ada",
    )(c_all, ada_w, ada_b.reshape(n_layers, 1, d6))


def _mod_spec(mod, tiles_per_seq):
    _, rows, d = mod.shape
    return pl.BlockSpec((None, rows, d), lambda i, *_: (i // tiles_per_seq, 0, 0))


def _gmlp_kernel(x_ref, g_ref, sh_ref, sc_ref, gt_ref, wu_ref, wv_ref, vg_ref, vb_ref, ws_ref, bs_ref,
                 wo_ref, *rest, lc, period, groups, emit_v):
    if emit_v:
        o_ref, v_ref, vn_scr, out_scr = rest
    else:
        o_ref, vn_scr, out_scr = rest
    x = x_ref[...]
    tm = x.shape[0]
    h = _norm_mod(x, g_ref[...], sh_ref[...], sc_ref[...]).astype(BF16)
    v = _gelu(_dot(h, wv_ref[...]))
    mu = jnp.mean(v, axis=-1, keepdims=True)
    vc = v - mu
    var = jnp.mean(vc * vc, axis=-1, keepdims=True)
    vn = vc * lax.rsqrt(var + EPS) * vg_ref[...] + vb_ref[...]
    if emit_v:
        v_ref[...] = vn
    vn_scr[...] = vn.astype(BF16)
    gd = vn.shape[1] // groups
    r = lax.broadcasted_iota(jnp.int32, (lc, lc), 0)
    c = lax.broadcasted_iota(jnp.int32, (lc, lc), 1)
    mask = c <= r
    if period < lc:
        blk = ~(period - 1)
        mask = mask & ((r & blk) == (c & blk))
        ri = lax.broadcasted_iota(jnp.int32, (lc, LANES), 0)
        ci = lax.broadcasted_iota(jnp.int32, (lc, LANES), 1)
        sel = _ones_where((ri & (period - 1)) == ci)
    for g in range(groups):
        u = _gelu(_dot(h, wu_ref[:, g * gd:(g + 1) * gd]))
        if period < lc:
            rows_of_block = _dot(sel, ws_ref[g].astype(BF16)).astype(BF16)
            ws_full = _dot_nt(rows_of_block, sel)
        else:
            ws_full = ws_ref[g]
        wsm = jnp.where(mask, ws_full, 0.0).astype(BF16)
        bcol = bs_ref[:, g:g + 1]
        for ci in range(tm // lc):
            rows = slice(ci * lc, (ci + 1) * lc)
            mixed = _dot(wsm, vn_scr[rows, g * gd:(g + 1) * gd]) + bcol
            out_scr[rows, g * gd:(g + 1) * gd] = (u[rows] * mixed).astype(BF16)
    o_ref[...] = x + gt_ref[...] * _dot(out_scr[...], wo_ref[...])


def _gmlp_call(x, g, shift, scale, gate, wu, wv, vg, vb, ws, bs_t, wo, *, tm, tiles_per_seq, period, emit_v):
    n, d = x.shape
    da = wu.shape[1]
    groups = ws.shape[0]
    lc = bs_t.shape[0]
    const2 = lambda i: (0, 0)
    in_specs = [
        pl.BlockSpec((tm, d), lambda i: (i, 0)),
        pl.BlockSpec((1, d), const2),
        _mod_spec(shift, tiles_per_seq), _mod_spec(scale, tiles_per_seq), _mod_spec(gate, tiles_per_seq),
        pl.BlockSpec((d, da), const2), pl.BlockSpec((d, da), const2),
        pl.BlockSpec((1, da), const2), pl.BlockSpec((1, da), const2),
        pl.BlockSpec(ws.shape, lambda i: (0, 0, 0)),
        pl.BlockSpec((lc, groups), const2),
        pl.BlockSpec((da, d), const2),
    ]
    out_specs = [pl.BlockSpec((tm, d), lambda i: (i, 0))]
    out_shape = [jax.ShapeDtypeStruct((n, d), F32)]
    if emit_v:
        out_specs.append(pl.BlockSpec((tm, da), lambda i: (i, 0)))
        out_shape.append(jax.ShapeDtypeStruct((n, da), F32))
    return pl.pallas_call(
        functools.partial(_gmlp_kernel, lc=lc, period=period, groups=groups, emit_v=emit_v),
        grid=(n // tm,),
        in_specs=in_specs, out_specs=out_specs, out_shape=out_shape,
        scratch_shapes=[pltpu.VMEM((tm, da), BF16), pltpu.VMEM((tm, da), BF16)],
        compiler_params=_cparams("arbitrary"),
        name="gmlp",
    )(x, g, shift, scale, gate, wu, wv, vg, vb, ws, bs_t, wo)


def _ffn_kernel(x_ref, g_ref, sh_ref, sc_ref, gt_ref, wg_ref, wu_ref, wd_ref, o_ref, h_scr, acc_scr):
    j = pl.program_id(1)

    @pl.when(j == 0)
    def _():
        h_scr[...] = _norm_mod(x_ref[...], g_ref[...], sh_ref[...], sc_ref[...]).astype(BF16)
        acc_scr[...] = jnp.zeros_like(acc_scr)

    h = h_scr[...]
    a = _silu(_dot(h, wg_ref[...].astype(BF16))) * _dot(h, wu_ref[...].astype(BF16))
    acc_scr[...] += _dot(a.astype(BF16), wd_ref[...].astype(BF16))

    @pl.when(j == pl.num_programs(1) - 1)
    def _():
        o_ref[...] = x_ref[...] + gt_ref[...] * acc_scr[...]


def _ffn_call(x, g, shift, scale, gate, w_gu, w_down, *, tm, tiles_per_seq):
    n, d = x.shape
    f = w_down.shape[0]
    tf = min(TF_FFN, f)
    nf = f // tf
    const2 = lambda i, j: (0, 0)
    return pl.pallas_call(
        _ffn_kernel,
        grid=(n // tm, nf),
        in_specs=[
            pl.BlockSpec((tm, d), lambda i, j: (i, 0)),
            pl.BlockSpec((1, d), const2),
            _mod_spec(shift, tiles_per_seq), _mod_spec(scale, tiles_per_seq), _mod_spec(gate, tiles_per_seq),
            pl.BlockSpec((d, tf), lambda i, j: (0, j)),
            pl.BlockSpec((d, tf), lambda i, j: (0, nf + j)),
            pl.BlockSpec((tf, d), lambda i, j: (j, 0)),
        ],
        out_specs=pl.BlockSpec((tm, d), lambda i, j: (i, 0)),
        out_shape=jax.ShapeDtypeStruct((n, d), F32),
        scratch_shapes=[pltpu.VMEM((tm, d), BF16), pltpu.VMEM((tm, d), F32)],
        compiler_params=_cparams("arbitrary", "arbitrary"),
        name="ffn",
    )(x, g, shift, scale, gate, w_gu, w_gu, w_down)


def _log_sigmoid(x):
    return jnp.minimum(x, 0.0) - jnp.log(1.0 + jnp.exp(-jnp.abs(x)))


def _foxproj_kernel(x_ref, g_ref, sh_ref, sc_ref, w_ref, wf_ref, bf_ref, *rest, transposed, period,
                    tiles_per_seq, q_scale):
    if transposed:
        q_ref, k_ref, kb_ref, v_ref, vb_ref, lf_ref, cum_ref, carry_scr = rest
    else:
        q_ref, k_ref, v_ref, lf_ref, cum_ref, carry_scr = rest
    i = pl.program_id(0)
    x = x_ref[...]
    tm, d = x.shape
    h = _norm_mod(x, g_ref[...], sh_ref[...], sc_ref[...]).astype(BF16)
    if transposed:
        tq = q_ref.shape[-1]
        qt = (_dot_nt(w_ref[0:d, :], h) * q_scale).astype(BF16)
        for jq in range(tm // tq):
            q_ref[jq] = qt[:, jq * tq:(jq + 1) * tq]
        k_ref[...] = _dot_nt(w_ref[d:2 * d, :], h)
        kb_ref[...] = _dot_nt(h, w_ref[d:2 * d, :]).astype(BF16)
        vt = _dot_nt(w_ref[2 * d:3 * d, :], h)
        v_ref[...] = vt
        vb_ref[...] = vt.astype(BF16)
    else:
        q_ref[...] = _dot_nt(h, w_ref[0:d, :]) * q_scale
        k_ref[...] = _dot_nt(h, w_ref[d:2 * d, :])
        v_ref[...] = _dot_nt(h, w_ref[2 * d:3 * d, :])
    logf = _log_sigmoid(_dot_nt(wf_ref[...], h) + bf_ref[...])
    lf_ref[...] = logf
    s = lax.broadcasted_iota(jnp.int32, (tm, tm), 0)
    t = lax.broadcasted_iota(jnp.int32, (tm, tm), 1)
    upper = s <= t
    if period < tm:
        blk = ~(period - 1)
        upper = upper & ((s & blk) == (t & blk))
    upper = _ones_where(upper)
    hi, mid, lo = _split3(logf)
    cum = _dot(hi, upper) + _dot(mid, upper) + _dot(lo, upper)
    if tiles_per_seq > 1:
        @pl.when(i % tiles_per_seq == 0)
        def _():
            carry_scr[...] = jnp.zeros_like(carry_scr)
        cum = cum + carry_scr[:, 0:1]
        carry_scr[...] = jnp.broadcast_to(cum[:, tm - 1:tm], carry_scr.shape)
    cum_ref[...] = cum


def _foxproj_call(x, g, shift, scale, w_t, wf_t, b_f, *, tm, tiles_per_seq, transposed, period):
    n, d = x.shape
    heads = wf_t.shape[0]
    n_seq = n // (tm * tiles_per_seq)
    t_len = tm * tiles_per_seq
    const2 = lambda i: (0, 0)
    row = lambda i: (i, 0)
    seq_t = lambda i: (i // tiles_per_seq, 0, i % tiles_per_seq)
    in_specs = [
        pl.BlockSpec((tm, d), row),
        pl.BlockSpec((1, d), const2),
        _mod_spec(shift, tiles_per_seq), _mod_spec(scale, tiles_per_seq),
        pl.BlockSpec((3 * d, d), const2),
        pl.BlockSpec((heads, d), const2),
        pl.BlockSpec((heads, 1), const2),
    ]
    lf_spec = pl.BlockSpec((None, heads, tm), seq_t)
    lf_shape = jax.ShapeDtypeStruct((n_seq, heads, t_len), F32)
    if transposed:
        tq = min(TQ_ATTN, tm)
        kv_spec = pl.BlockSpec((None, d, tm), seq_t)
        q_spec = pl.BlockSpec((None, tm // tq, d, tq), lambda i: (i // tiles_per_seq, i % tiles_per_seq, 0, 0))
        out_specs = [q_spec, kv_spec, pl.BlockSpec((tm, d), row), kv_spec, kv_spec, lf_spec, lf_spec]
        out_shape = [jax.ShapeDtypeStruct((n_seq, t_len // tq, d, tq), BF16),
                     jax.ShapeDtypeStruct((n_seq, d, t_len), F32), jax.ShapeDtypeStruct((n, d), BF16),
                     jax.ShapeDtypeStruct((n_seq, d, t_len), F32), jax.ShapeDtypeStruct((n_seq, d, t_len), BF16),
                     lf_shape, lf_shape]
    else:
        out_specs = [pl.BlockSpec((tm, d), row)] * 3 + [lf_spec, lf_spec]
        out_shape = [jax.ShapeDtypeStruct((n, d), F32)] * 3 + [lf_shape, lf_shape]
    return pl.pallas_call(
        functools.partial(_foxproj_kernel, transposed=transposed, period=period, tiles_per_seq=tiles_per_seq,
                          q_scale=float(d // heads) ** -0.5 * (LOG2E if transposed else 1.0)),
        grid=(n // tm,),
        in_specs=in_specs, out_specs=out_specs, out_shape=out_shape,
        scratch_shapes=[pltpu.VMEM((heads, LANES), F32)],
        compiler_params=_cparams("arbitrary"),
        name="foxproj",
    )(x, g, shift, scale, w_t, wf_t, b_f)


def _attn_kernel(qt_ref, k_ref, vt_ref, cum_ref, o_ref, ck_scr, m_scr, l_scr, acc_scr, *, hd):
    nq, _, tq = qt_ref.shape
    lane = lax.broadcasted_iota(jnp.int32, (tq, LANES), 1)
    key = lax.broadcasted_iota(jnp.int32, (tq, tq), 0)
    qry = lax.broadcasted_iota(jnp.int32, (tq, tq), 1)
    eye = key == qry
    causal = key <= qry
    m_scr[...] = jnp.full_like(m_scr, NEG)
    l_scr[...] = jnp.zeros_like(l_scr)
    acc_scr[...] = jnp.zeros_like(acc_scr)

    def update(hh, qi, kh, vh, masked):
        t = _dot(kh, qt_ref[qi]) - ck_scr[...]
        if masked:
            t = jnp.where(causal, t, NEG)
        cq = cum_ref[hh, qi:qi + 1, :] * LOG2E
        m_old = m_scr[hh, qi]
        m_new = jnp.maximum(m_old, jnp.max(t, axis=0, keepdims=True) + cq)
        alpha = jnp.exp2(m_old - m_new)
        p = jnp.exp2(t + (cq - m_new))
        l_scr[hh, qi] = alpha * l_scr[hh, qi] + jnp.sum(p, axis=0, keepdims=True)
        acc_scr[hh, qi] = alpha * acc_scr[hh, qi] + _dot(vh, p.astype(BF16))
        m_scr[hh, qi] = m_new

    for ki in range(nq):
        keys = slice(ki * tq, (ki + 1) * tq)
        k_blk = k_ref[keys, :]
        for hh in range(2):
            kh = jnp.where((lane >= hh * hd) & (lane < (hh + 1) * hd), k_blk, jnp.zeros_like(k_blk))
            vh = vt_ref[hh * hd:(hh + 1) * hd, keys]
            col = jnp.sum(jnp.where(eye, cum_ref[hh, ki:ki + 1, :], 0.0), axis=1, keepdims=True)
            ck_scr[...] = jnp.broadcast_to(col * LOG2E, ck_scr.shape)
            update(hh, ki, kh, vh, True)
            for qi in range(ki + 1, nq):
                update(hh, qi, kh, vh, False)

    for qi in range(nq):
        ot = jnp.concatenate([acc_scr[0, qi] / l_scr[0, qi], acc_scr[1, qi] / l_scr[1, qi]], axis=0)
        o_ref[qi * tq:(qi + 1) * tq, :] = ot.T.astype(o_ref.dtype)


def _attn_call(qt, k, vt, cum5, *, t_len, heads):
    n_seq, nq, d, tq = qt.shape
    hd = d // heads
    assert 2 * hd == LANES
    return pl.pallas_call(
        functools.partial(_attn_kernel, hd=hd),
        grid=(n_seq, heads // 2),
        in_specs=[
            pl.BlockSpec((None, nq, LANES, tq), lambda b, p: (b, 0, p, 0)),
            pl.BlockSpec((t_len, LANES), lambda b, p: (b, p)),
            pl.BlockSpec((None, LANES, t_len), lambda b, p: (b, p, 0)),
            pl.BlockSpec((None, None, 2, nq, tq), lambda b, p: (b, p, 0, 0, 0)),
        ],
        out_specs=pl.BlockSpec((t_len, LANES), lambda b, p: (b, p)),
        out_shape=jax.ShapeDtypeStruct((n_seq * t_len, d), BF16),
        scratch_shapes=[pltpu.VMEM((tq, tq), F32), pltpu.VMEM((2, nq, 1, tq), F32),
                        pltpu.VMEM((2, nq, 1, tq), F32), pltpu.VMEM((2, nq, hd, tq), F32)],
        compiler_params=_cparams("arbitrary", "arbitrary"),
        name="attn_prompt",
    )(qt, k, vt, cum5)


def _attn_sample_kernel(pt_ref, q_ref, cn_ref, kn_ref, vn_ref, *rest, pages, hd, page_size):
    k_refs = rest[0:pages]
    v_refs = rest[pages:2 * pages]
    lf_refs = rest[2 * pages:3 * pages]
    o_ref, qbd_scr, cn_scr, m_scr, l_scr, acc_scr, suf_scr = rest[3 * pages:]
    j = pl.program_id(1)
    tn, d = q_ref.shape
    heads = d // hd
    rows = heads * tn
    row = lax.broadcasted_iota(jnp.int32, (rows, LANES), 0)
    lane = lax.broadcasted_iota(jnp.int32, (rows, LANES), 1)
    q_of_row = row & (tn - 1)

    def rep(a):
        return jnp.broadcast_to(a[:, None, :], (heads, tn, a.shape[-1])).reshape(rows, a.shape[-1])

    gw = qbd_scr.shape[1]
    gr = (gw // hd) * tn
    n_groups = d // gw
    g_rows = lambda g: slice(g * gr, (g + 1) * gr)
    g_cols = lambda g: slice(g * gw, (g + 1) * gw)
    rr = lax.broadcasted_iota(jnp.int32, (gr, gw), 0)
    cc = lax.broadcasted_iota(jnp.int32, (gr, gw), 1)
    own_head = _shift_div(rr, tn) == _shift_div(cc, hd)

    @pl.when(j == 0)
    def _():
        q = q_ref[...]
        for g in range(n_groups):
            qrep = jnp.broadcast_to(q[None, :, g_cols(g)], (gw // hd, tn, gw)).reshape(gr, gw)
            qbd_scr[g_rows(g), :] = jnp.where(own_head, qrep, 0.0).astype(BF16)
        cn_scr[...] = jnp.sum(jnp.where(lane == q_of_row, rep(cn_ref[...]), 0.0), axis=1, keepdims=True)
        m_scr[...] = jnp.full_like(m_scr, NEG)
        l_scr[...] = jnp.zeros_like(l_scr)
        acc_scr[...] = jnp.zeros_like(acc_scr)
        suf_scr[...] = jnp.zeros_like(suf_scr)

    s_idx = lax.broadcasted_iota(jnp.int32, (page_size, page_size), 0)
    t_idx = lax.broadcasted_iota(jnp.int32, (page_size, page_size), 1)
    later = _ones_where(s_idx > t_idx)
    lf_all = jnp.concatenate([lf_refs[i][...] for i in range(pages)], axis=0)
    hi, mid, lo = _split3(lf_all)
    suf_all = _dot(hi, later) + _dot(mid, later) + _dot(lo, later)
    tot_all = suf_all[:, 0:1] + lf_all[:, 0:1]
    carry = suf_scr[:, 0:1]
    scores = [None] * pages
    for i in reversed(range(pages)):
        suf = suf_all[i * heads:(i + 1) * heads, :] + carry
        carry = carry + tot_all[i * heads:(i + 1) * heads, :]
        qk =[_dot(qbd_scr[g_rows(g), :], k_refs[i][g_cols(g), :].astype(BF16)) for g in range(n_groups)]
        scores[i] = jnp.concatenate(qk, axis=0) + rep(suf)
    suf_scr[...] = jnp.broadcast_to(carry, suf_scr.shape)
    s = jnp.concatenate(scores, axis=1) + cn_scr[...]
    m_old = m_scr[...]
    m_new = jnp.maximum(m_old, jnp.max(s, axis=1, keepdims=True))
    alpha = jnp.exp(m_old - m_new)
    p = jnp.exp(s - m_new)
    l_scr[...] = alpha * l_scr[...] + jnp.sum(p, axis=1, keepdims=True)
    pb = p.astype(BF16)
    for g in range(n_groups):
        pv = _dot_nt(pb[g_rows(g), 0:page_size], v_refs[0][g_cols(g), :].astype(BF16))
        for i in range(1, pages):
            pv = pv + _dot_nt(pb[g_rows(g), i * page_size:(i + 1) * page_size],
                              v_refs[i][g_cols(g), :].astype(BF16))
        acc_scr[g_rows(g), :] = alpha[g_rows(g), :] * acc_scr[g_rows(g), :] + pv
    m_scr[...] = m_new

    @pl.when(j == pl.num_programs(1) - 1)
    def _():
        pad = jnp.zeros((LANES - tn, d), BF16)
        k_new = jnp.concatenate([kn_ref[...].astype(BF16), pad], axis=0)
        v_new = jnp.concatenate([vn_ref[...].astype(BF16), pad], axis=0)
        qk = [_dot_nt(qbd_scr[g_rows(g), :], k_new[:, g_cols(g)]) for g in range(n_groups)]
        s = jnp.concatenate(qk, axis=0) + cn_scr[...] - rep(cn_ref[...])
        s = jnp.where(lane <= q_of_row, s, NEG)
        m_old = m_scr[...]
        m_new = jnp.maximum(m_old, jnp.max(s, axis=1, keepdims=True))
        alpha = jnp.exp(m_old - m_new)
        pb = jnp.exp(s - m_new)
        l_new = alpha * l_scr[...] + jnp.sum(pb, axis=1, keepdims=True)
        pb = pb.astype(BF16)
        for g in range(n_groups):
            o = alpha[g_rows(g), :] * acc_scr[g_rows(g), :] + _dot(pb[g_rows(g), :], v_new[:, g_cols(g)])
            o = jnp.where(own_head, o / l_new[g_rows(g), :], 0.0)
            o_ref[:, g_cols(g)] = jnp.sum(o.reshape(gw // hd, tn, gw), axis=0)


def _attn_sample_call(page_table, q, cn_pad, k_new, v_new, kc_t, vc_t, lfc_t, *, tn, heads):
    n, d = q.shape
    n_seq, n_pages = page_table.shape
    page_size = kc_t.shape[-1]
    pages = min(PAGES_PER_STEP, n_pages)
    n_steps = n_pages // pages
    hd = d // heads

    def page_map(i):
        def index(b, j, pt):
            return (pt[b * n_pages + (n_steps - 1 - j) * pages + i], 0, 0)
        return index

    row = lambda b, j, pt: (b, 0)
    in_specs = [
        pl.BlockSpec((tn, d), row),
        pl.BlockSpec((None, heads, LANES), lambda b, j, pt: (b, 0, 0)),
        pl.BlockSpec((tn, d), row), pl.BlockSpec((tn, d), row),
    ]
    in_specs += [pl.BlockSpec((None, d, page_size), page_map(i)) for i in range(pages)]
    in_specs += [pl.BlockSpec((None, d, page_size), page_map(i)) for i in range(pages)]
    in_specs += [pl.BlockSpec((None, heads, page_size), page_map(i)) for i in range(pages)]
    rows = heads * tn
    return pl.pallas_call(
        functools.partial(_attn_sample_kernel, pages=pages, hd=hd, page_size=page_size),
        grid_spec=pltpu.PrefetchScalarGridSpec(
            num_scalar_prefetch=1,
            grid=(n_seq, n_steps),
            in_specs=in_specs,
            out_specs=pl.BlockSpec((tn, d), row),
            scratch_shapes=[pltpu.VMEM((rows, MXU_TILE), BF16), pltpu.VMEM((rows, 1), F32),
                            pltpu.VMEM((rows, 1), F32), pltpu.VMEM((rows, 1), F32),
                            pltpu.VMEM((rows, MXU_TILE), F32), pltpu.VMEM((heads, LANES), F32)],
        ),
        out_shape=jax.ShapeDtypeStruct((n, d), F32),
        compiler_params=_cparams("arbitrary", "arbitrary"),
        name="attn_sample",
    )(page_table.reshape(-1), q, cn_pad, k_new, v_new, *([kc_t] * pages), *([vc_t] * pages), *([lfc_t] * pages))


def _post_kernel(op_ref, os_ref, wo_ref, xp_ref, xs_ref, gtp_ref, gts_ref, g_ref, shp_ref, shs_ref, scp_ref,
                 scs_ref, wr_ref, br_ref, x3_ref, h_ref, meta_ref, metat_ref, blk_ref, cout_ref, carry_scr,
                 *, n_experts, n_prompt_blocks):
    i = pl.program_id(0)
    tm = xp_ref.shape[0]
    is_p = i < n_prompt_blocks

    @pl.when(i == 0)
    def _():
        carry_scr[...] = jnp.zeros_like(carry_scr)

    o = jnp.where(is_p, op_ref[...], os_ref[...].astype(BF16))
    x = jnp.where(is_p, xp_ref[...], xs_ref[...])
    gate = jnp.where(is_p, gtp_ref[...], gts_ref[...])
    shift = jnp.where(is_p, shp_ref[...], shs_ref[...])
    scale = jnp.where(is_p, scp_ref[...], scs_ref[...])
    x3 = x + gate * _dot(o, wo_ref[...])
    x3_ref[...] = x3
    h = _norm_mod(x3, g_ref[...], shift, scale)
    h_hi = h.astype(BF16)
    h_ref[...] = h_hi
    h_lo = (h - h_hi.astype(F32)).astype(BF16)
    wr = wr_ref[...]
    w_hi = wr.astype(BF16)
    w_lo = (wr - w_hi.astype(F32)).astype(BF16)
    logits = _dot(h_hi, w_hi) + _dot(h_hi, w_lo) + _dot(h_lo, w_hi) + br_ref[...]
    lane_i = lax.broadcasted_iota(jnp.int32, (tm, LANES), 1)
    lane = lane_i.astype(F32)
    logits = jnp.where(lane_i < n_experts, logits, NEG)
    l1 = jnp.max(logits, axis=1, keepdims=True)
    i1 = jnp.min(jnp.where(logits == l1, lane, float(LANES)), axis=1, keepdims=True)
    rest = jnp.where(lane == i1, NEG, logits)
    l2 = jnp.max(rest, axis=1, keepdims=True)
    i2 = jnp.min(jnp.where(rest == l2, lane, float(LANES)), axis=1, keepdims=True)
    e = jnp.exp(l2 - l1)
    g1 = 1.0 / (1.0 + e)
    g2 = e / (1.0 + e)
    onehot = jnp.where((lane == i1) | (lane == i2), 1.0, 0.0)
    r = lax.broadcasted_iota(jnp.int32, (tm, tm), 0)
    c = lax.broadcasted_iota(jnp.int32, (tm, tm), 1)
    before = _ones_where(c < r)
    carry = carry_scr[0:1, :]
    blk_ref[...] = carry
    prefix = _dot(before, onehot.astype(BF16)) + carry
    r1 = jnp.sum(jnp.where(lane == i1, prefix, 0.0), axis=1, keepdims=True)
    r2 = jnp.sum(jnp.where(lane == i2, prefix, 0.0), axis=1, keepdims=True)
    cols = (i1, i2, g1, g2, r1, r2)
    meta = jnp.zeros((tm, LANES), F32)
    for k, col in enumerate(cols):
        meta = jnp.where(lane_i == k, col, meta)
    meta_ref[...] = meta
    metat_ref[...] = meta.T[0:SUBLANES, :]
    carry = carry + jnp.sum(onehot, axis=0, keepdims=True)
    carry_scr[...] = jnp.broadcast_to(carry, carry_scr.shape)
    cout_ref[...] = carry


def _post_call(o_p, o_s, wo, x_p, x_s, gate_p, gate_s, g, shift_p, shift_s, scale_p, scale_s, wr_pad, br_pad,
               *, tm, blocks_per_seq, n_experts):
    n_p, d = x_p.shape
    n_s = x_s.shape[0]
    npb = n_p // tm
    nb = npb + n_s // tm
    n = n_p + n_s
    const2 = lambda i: (0, 0)
    row = lambda i: (i, 0)
    p_row = lambda i: (jnp.minimum(i, npb - 1), 0)
    s_row = lambda i: (jnp.maximum(i - npb, 0), 0)
    p_mod = pl.BlockSpec((None, 1, d), lambda i: (jnp.minimum(i, npb - 1) // blocks_per_seq, 0, 0))
    s_mod = pl.BlockSpec((None, tm, d), lambda i: (jnp.maximum(i - npb, 0), 0, 0))
    return pl.pallas_call(
        functools.partial(_post_kernel, n_experts=n_experts, n_prompt_blocks=npb),
        grid=(nb,),
        in_specs=[
            pl.BlockSpec((tm, d), p_row), pl.BlockSpec((tm, d), s_row),
            pl.BlockSpec((d, d), const2),
            pl.BlockSpec((tm, d), p_row), pl.BlockSpec((tm, d), s_row),
            p_mod, s_mod,
            pl.BlockSpec((1, d), const2),
            p_mod, s_mod, p_mod, s_mod,
            pl.BlockSpec((d, LANES), const2),
            pl.BlockSpec((1, LANES), const2),
        ],
        out_specs=[
            pl.BlockSpec((tm, d), row), pl.BlockSpec((tm, d), row), pl.BlockSpec((tm, LANES), row),
            pl.BlockSpec((None, SUBLANES, tm), lambda i: (i, 0, 0)),
            pl.BlockSpec((None, 1, LANES), lambda i: (i, 0, 0)),
            pl.BlockSpec((1, LANES), const2),
        ],
        out_shape=[
            jax.ShapeDtypeStruct((n, d), F32), jax.ShapeDtypeStruct((n, d), BF16),
            jax.ShapeDtypeStruct((n, LANES), F32),
            jax.ShapeDtypeStruct((nb, SUBLANES, tm), F32),
            jax.ShapeDtypeStruct((nb, 1, LANES), F32),
            jax.ShapeDtypeStruct((1, LANES), F32),
        ],
        scratch_shapes=[pltpu.VMEM((SUBLANES, LANES), F32)],
        compiler_params=_cparams("arbitrary"),
        name="post_attn_router",
    )(o_p, o_s, wo, x_p, x_s, gate_p, gate_s, g, shift_p, shift_s, scale_p, scale_s, wr_pad, br_pad)


def _sorted_pos(idx, rank, start_ref, n_experts):
    pos = rank
    for e in range(n_experts):
        pos = pos + jnp.where(idx == float(e), start_ref[e].astype(F32), 0.0)
    return pos


def _dispatch_kernel(lo_ref, hi_ref, start_ref, mt_ref, h_ref, xs_ref, gs_ref, acc_scr, gacc_scr,
                     *, tb, n_experts):
    j = pl.program_id(0)
    sub = xs_ref.shape[0]
    nb = mt_ref.shape[0]
    lo = lo_ref[j]
    hi = hi_ref[j]

    @pl.when(hi <= lo)
    def _():
        xs_ref[...] = jnp.zeros_like(xs_ref)
        gs_ref[...] = jnp.zeros_like(gs_ref)

    def hits(b, width):
        dest = (j * sub + lax.broadcasted_iota(jnp.int32, (sub, width * tb), 0)).astype(F32)
        mts = [mt_ref[b + w] for w in range(width)]
        field = lambda r: jnp.concatenate([m[r:r + 1, :] for m in mts], axis=1)
        hit0 = _sorted_pos(field(0), field(4), start_ref, n_experts) == dest
        hit1 = _sorted_pos(field(1), field(5), start_ref, n_experts) == dest
        gate = jnp.where(hit0, field(2), 0.0) + jnp.where(hit1, field(3), 0.0)
        return _ones_where(hit0 | hit1), gate

    n_even = nb - nb % 2

    def pair(m, carry):
        b = 2 * m
        oh, gate = hits(b, 2)
        rows = pl.ds(pl.multiple_of(b * tb, 2 * tb), 2 * tb)
        acc_scr[...] += _dot(oh, h_ref[rows, :])
        gacc_scr[...] += gate
        return carry

    @pl.when(hi > lo)
    def _():
        acc_scr[...] = jnp.zeros_like(acc_scr)
        gacc_scr[...] = jnp.zeros_like(gacc_scr)
        lax.fori_loop(lo // 2, (jnp.minimum(hi, n_even) + 1) // 2, pair, 0)

        if nb % 2:
            @pl.when(hi == nb)
            def _():
                oh, gate = hits(nb - 1, 1)
                acc_scr[...] += _dot(oh, h_ref[(nb - 1) * tb:nb * tb, :])
                gacc_scr[:, 0:tb] += gate

        xs_ref[...] = acc_scr[...].astype(BF16)
        gs_ref[...] = jnp.broadcast_to(jnp.sum(gacc_scr[...], axis=1, keepdims=True), gs_ref.shape)


def _dispatch_call(tile_lo, tile_hi, start, meta_t, h, *, n_sorted, n_experts):
    n, d = h.shape
    nb, _, tb = meta_t.shape
    whole = lambda nd: (lambda j, lo, hi, st: (0,) * nd)
    once = pl.Buffered(1)
    return pl.pallas_call(
        functools.partial(_dispatch_kernel, tb=tb, n_experts=n_experts),
        grid_spec=pltpu.PrefetchScalarGridSpec(
            num_scalar_prefetch=3,
            grid=(n_sorted // SUB,),
            in_specs=[
                pl.BlockSpec((nb, SUBLANES, tb), whole(3), pipeline_mode=once),
                pl.BlockSpec((n, d), whole(2), pipeline_mode=once),
            ],
            out_specs=[pl.BlockSpec((SUB, d), lambda j, lo, hi, st: (j, 0)),
                       pl.BlockSpec((SUB, LANES), lambda j, lo, hi, st: (j, 0))],
            scratch_shapes=[pltpu.VMEM((SUB, d), F32), pltpu.VMEM((SUB, 2 * tb), F32)],
        ),
        out_shape=[jax.ShapeDtypeStruct((n_sorted, d), BF16), jax.ShapeDtypeStruct((n_sorted, LANES), F32)],
        compiler_params=_cparams("arbitrary"),
        name="moe_dispatch",
    )(tile_lo, tile_hi, start, meta_t, h)


def _moe_kernel(ex_ref, nsub_ref, x_ref, gs_ref, wg_ref, wu_ref, wd_ref, y_ref, acc_scr, wgb_scr, wub_scr, wdb_scr):
    s = pl.program_id(0)
    j = pl.program_id(1)
    nsub = nsub_ref[s]
    last = j == pl.num_programs(1) - 1

    @pl.when(nsub > 0)
    def _():
        wgb_scr[...] = wg_ref[...].astype(BF16)
        wub_scr[...] = wu_ref[...].astype(BF16)
        wdb_scr[...] = wd_ref[...].astype(BF16)

    n_full = (nsub * SUB) // FFN_ROWS
    odd = nsub * SUB - n_full * FFN_ROWS > 0

    def tile_rows(t):
        return pl.ds(pl.multiple_of(t * SUB, SUB), SUB)

    def step_rows(t):
        return pl.ds(pl.multiple_of(t * FFN_ROWS, FFN_ROWS), FFN_ROWS)

    @pl.when(j == 0)
    def _():
        def zero(t, carry):
            acc_scr[tile_rows(t), :] = jnp.zeros((SUB, acc_scr.shape[1]), F32)
            return carry
        lax.fori_loop(0, nsub, zero, 0)

    def ffn(rows):
        x = x_ref[rows, :]
        a = _silu(_dot(x, wgb_scr[...])) * _dot(x, wub_scr[...])
        acc_scr[rows, :] += _dot(a.astype(BF16), wdb_scr[...])

    def body(t, carry):
        ffn(step_rows(t))
        return carry

    lax.fori_loop(0, n_full, body, 0)

    @pl.when(odd)
    def _():
        ffn(tile_rows(nsub - 1))

    @pl.when(last)
    def _():
        def emit(t, carry):
            rows = tile_rows(t)
            y_ref[rows, :] = (acc_scr[rows, :] * gs_ref[rows, 0:1]).astype(BF16)
            return carry
        lax.fori_loop(0, nsub, emit, 0)

        def zero(t, carry):
            y_ref[tile_rows(t), :] = jnp.zeros((SUB, y_ref.shape[1]), BF16)
            return carry
        lax.fori_loop(nsub, y_ref.shape[0] // SUB, zero, 0)


def _moe_call(sup_expert, sup_nsub, xs, gs, w_gu, w_down, *, ts):
    n_sorted, d = xs.shape
    f = w_down.shape[1]
    tf = min(TF_FFN, f)
    nf = f // tf
    n_super = n_sorted // ts

    def jj(s, j, nsub):
        return jnp.where(nsub[s] > 0, j, nf - 1)

    return pl.pallas_call(
        _moe_kernel,
        grid_spec=pltpu.PrefetchScalarGridSpec(
            num_scalar_prefetch=2,
            grid=(n_super, nf),
            in_specs=[
                pl.BlockSpec((ts, d), lambda s, j, ex, ns: (s, 0)),
                pl.BlockSpec((ts, LANES), lambda s, j, ex, ns: (s, 0)),
                pl.BlockSpec((None, d, tf), lambda s, j, ex, ns: (ex[s], 0, jj(s, j, ns))),
                pl.BlockSpec((None, d, tf), lambda s, j, ex, ns: (ex[s], 0, nf + jj(s, j, ns))),
                pl.BlockSpec((None, tf, d), lambda s, j, ex, ns: (ex[s], jj(s, j, ns), 0)),
            ],
            out_specs=pl.BlockSpec((ts, d), lambda s, j, ex, ns: (s, 0)),
            scratch_shapes=[pltpu.VMEM((ts, d), F32), pltpu.VMEM((d, tf), BF16), pltpu.VMEM((d, tf), BF16),
                            pltpu.VMEM((tf, d), BF16)],
        ),
        out_shape=jax.ShapeDtypeStruct((n_sorted, d), BF16),
        compiler_params=_cparams("arbitrary", "arbitrary"),
        name="moe_ffn",
    )(sup_expert, sup_nsub, xs, gs, w_gu, w_gu, w_down)


def _combine_kernel(tile_ref, need2_ref, start_ref, meta_ref, x_ref, gtp_ref, gts_ref, gf_ref, *rest,
                    n_experts, n_prompt_blocks):
    y_refs = rest[0:2 * n_experts]
    op_ref, os_ref, acc_scr = rest[2 * n_experts:]
    b = pl.program_id(0)
    tb = acc_scr.shape[0]
    meta = meta_ref[...]
    col = lax.broadcasted_iota(jnp.int32, (tb, SUB), 1)

    def window(e, w):
        first = start_ref[e].astype(F32)
        p0 = jnp.where(meta[:, 0:1] == float(e), meta[:, 4:5] + first, -1.0)
        p1 = jnp.where(meta[:, 1:2] == float(e), meta[:, 5:6] + first, -1.0)
        src = ((tile_ref[b * n_experts + e] + w) * SUB + col).astype(F32)
        return _ones_where((p0 == src) | (p1 == src))

    total = _dot(window(0, 0), y_refs[0][...])
    for e in range(1, n_experts):
        total = total + _dot(window(e, 0), y_refs[2 * e][...])
    acc_scr[...] = total
    for e in range(n_experts):
        @pl.when(need2_ref[b * n_experts + e] > 0)
        def _(e=e):
            acc_scr[...] += _dot(window(e, 1), y_refs[2 * e + 1][...])

    is_p = b < n_prompt_blocks
    x4 = x_ref[...] + jnp.where(is_p, gtp_ref[...], gts_ref[...]) * acc_scr[...]
    ms = jnp.mean(x4 * x4, axis=-1, keepdims=True)
    y = x4 * lax.rsqrt(ms + EPS) * gf_ref[...]

    @pl.when(is_p)
    def _():
        op_ref[...] = y

    @pl.when(jnp.logical_not(is_p))
    def _():
        os_ref[...] = y


def _combine_call(win_tile, win_need2, start, meta, x, gate_p, gate_s, g_final, ys, *, n_p, tb, blocks_per_seq,
                  n_experts):
    n, d = x.shape
    n_s = n - n_p
    npb = n_p // tb
    nb = n // tb

    def pb(b):
        return jnp.minimum(b, npb - 1)

    def sb(b):
        return jnp.maximum(b - npb, 0)

    def win_map(e, w):
        if w == 0:
            return lambda b, tile, need, st: (tile[b * n_experts + e], 0)
        return lambda b, tile, need, st: (jnp.where(need[b * n_experts + e] > 0, tile[b * n_experts + e] + 1, 0), 0)

    in_specs = [
        pl.BlockSpec((tb, LANES), lambda b, tile, need, st: (b, 0)),
        pl.BlockSpec((tb, d), lambda b, tile, need, st: (b, 0)),
        pl.BlockSpec((None, 1, d), lambda b, tile, need, st: (pb(b) // blocks_per_seq, 0, 0)),
        pl.BlockSpec((None, tb, d), lambda b, tile, need, st: (sb(b), 0, 0)),
        pl.BlockSpec((1, d), lambda b, tile, need, st: (0, 0)),
    ]
    in_specs += [pl.BlockSpec((SUB, d), win_map(e, w)) for e in range(n_experts) for w in range(2)]
    return pl.pallas_call(
        functools.partial(_combine_kernel, n_experts=n_experts, n_prompt_blocks=npb),
        grid_spec=pltpu.PrefetchScalarGridSpec(
            num_scalar_prefetch=3,
            grid=(nb,),
            in_specs=in_specs,
            out_specs=[pl.BlockSpec((tb, d), lambda b, tile, need, st: (pb(b), 0)),
                       pl.BlockSpec((tb, d), lambda b, tile, need, st: (sb(b), 0))],
            scratch_shapes=[pltpu.VMEM((tb, d), F32)],
        ),
        out_shape=[jax.ShapeDtypeStruct((n_p, d), F32), jax.ShapeDtypeStruct((n_s, d), F32)],
        compiler_params=_cparams("arbitrary"),
        name="moe_combine",
    )(win_tile, win_need2, start, meta, x, gate_p, gate_s, g_final, *([ys] * (2 * n_experts)))


def _routing_tables(blk_cnt, counts, *, n_experts, ts, n_super):
    nb = blk_cnt.shape[0]
    cnt = counts[0, :n_experts].astype(jnp.int32)
    n_sup_e = (cnt + ts - 1) // ts
    sup_start = jnp.cumsum(n_sup_e) - n_sup_e
    start = sup_start * ts

    s_ids = jnp.arange(n_super, dtype=jnp.int32)
    used = jnp.sum(n_sup_e)
    sup_end = sup_start + n_sup_e
    sup_e = jnp.sum((s_ids[:, None] >= sup_end[None, :]).astype(jnp.int32), axis=1)
    sup_e = jnp.clip(sup_e, 0, n_experts - 1)
    last_e = jnp.max(jnp.where(cnt > 0, jnp.arange(n_experts, dtype=jnp.int32), 0))
    sup_e = jnp.where(s_ids < used, sup_e, last_e)
    rows_in = jnp.clip(cnt[sup_e] - (s_ids - sup_start[sup_e]) * ts, 0, ts)
    sup_nsub = jnp.where(s_ids < used, (rows_in + SUB - 1) // SUB, 0).astype(jnp.int32)

    blk = blk_cnt[:, 0, :n_experts].astype(jnp.int32)
    blk_end = jnp.concatenate([blk[1:], cnt[None, :]], axis=0)
    n_tiles = n_super * (ts // SUB)
    t_ids = jnp.arange(n_tiles, dtype=jnp.int32)
    t_e = sup_e[t_ids // (ts // SUB)]
    t_r0 = t_ids * SUB - start[t_e]
    t_active = ((t_ids // (ts // SUB)) < used) & (t_r0 < cnt[t_e]) & (t_r0 >= 0)
    t_r1 = jnp.minimum(t_r0 + SUB, cnt[t_e])
    be = blk_end[:, t_e]
    bs = blk[:, t_e]
    overlap = (be > t_r0[None, :]) & (bs < t_r1[None, :])
    b_ids = jnp.arange(nb, dtype=jnp.int32)[:, None]
    lo = jnp.min(jnp.where(overlap, b_ids, nb), axis=0)
    hi = jnp.max(jnp.where(overlap, b_ids + 1, 0), axis=0)
    tile_lo = jnp.where(t_active, lo, 0).astype(jnp.int32)
    tile_hi = jnp.where(t_active, hi, 0).astype(jnp.int32)

    first = start[None, :] + blk
    n_be = blk_end - blk
    win_tile = (first // SUB).astype(jnp.int32)
    win_need2 = ((first + n_be) > (win_tile + 1) * SUB).astype(jnp.int32)

    return dict(start=start.astype(jnp.int32), sup_e=sup_e, sup_nsub=sup_nsub, tile_lo=tile_lo,
                tile_hi=tile_hi, win_tile=win_tile.reshape(-1), win_need2=win_need2.reshape(-1))


def kernel(x_prompt, x_sample, c_prompt, c_sample, cache_k, cache_v, cache_logf, page_table, norm_mix_g, norm_ffn_g, final_norm_g, ada_w, ada_b, gmlp_w_in, gmlp_v_g, gmlp_v_b, gmlp_w_s, gmlp_b_s, gmlp_w_out, fox_w_in, fox_b_f, fox_w_out, ffn_w_gu, ffn_w_down, moe_w_r, moe_b_r, moe_w_gu, moe_w_down):
    n_seq_p, t_len, d = x_prompt.shape
    n_seq_s, t_new, _ = x_sample.shape
    n_p = n_seq_p * t_len
    n_s = n_seq_s * t_new
    heads = N_HEADS
    hd = d // heads
    n_experts = moe_w_r.shape[-1]
    da = gmlp_w_out.shape[1]

    xp = x_prompt.reshape(n_p, d)
    xs = x_sample.reshape(n_s, d)

    mod = _ada_call(jnp.concatenate([c_prompt, c_sample], axis=0), ada_w, ada_b)

    def mods(layer):
        mp = [mod[layer, :n_seq_p, c * d:(c + 1) * d].reshape(n_seq_p, 1, d) for c in range(6)]
        ms = [jnp.repeat(mod[layer, n_seq_p:, c * d:(c + 1) * d], t_new, axis=0).reshape(1, n_s, d)
              for c in range(6)]
        return mp, ms

    row = lambda a: a.reshape(1, -1)

    mp, ms = mods(0)
    w_in = gmlp_w_in[0].astype(BF16)
    wu, wv = w_in[:, :da], w_in[:, da:]
    wo = gmlp_w_out[0].astype(BF16)
    vg, vb = row(gmlp_v_g[0]), row(gmlp_v_b[0])
    lc = min(GMLP_CHUNK, t_len)
    tm = min(TM_GMLP, t_len)
    xp = _gmlp_call(xp, row(norm_mix_g[0]), mp[0], mp[1], mp[2], wu, wv, vg, vb,
                    gmlp_w_s[0][:, :lc, :lc], gmlp_b_s[0][:, :lc].T, wo,
                    tm=tm, tiles_per_seq=t_len // tm, period=lc, emit_v=False)[0]
    reps = n_s // t_new
    ws_s = jnp.zeros((gmlp_w_s.shape[1], LANES, LANES), F32).at[:, :t_new, :t_new].set(
        gmlp_w_s[0][:, :t_new, :t_new])
    bs_s = jnp.tile(gmlp_b_s[0][:, :t_new], (1, reps)).T
    xs, v_rows = _gmlp_call(xs, row(norm_mix_g[0]), ms[0], ms[1], ms[2], wu, wv, vg, vb, ws_s, bs_s, wo,
                            tm=n_s, tiles_per_seq=1, period=t_new, emit_v=True)
    tm = min(TM_FFN, t_len)
    xp = _ffn_call(xp, row(norm_ffn_g[0]), mp[3], mp[4], mp[5], ffn_w_gu[0], ffn_w_down[0],
                   tm=tm, tiles_per_seq=t_len // tm)
    xs = _ffn_call(xs, row(norm_ffn_g[0]), ms[3], ms[4], ms[5], ffn_w_gu[0], ffn_w_down[0],
                   tm=n_s, tiles_per_seq=1)

    mp, ms = mods(1)
    w_t = fox_w_in[0].T.astype(BF16)
    wqkv_t, wf_t = w_t[:3 * d], w_t[3 * d:]
    b_f = fox_b_f[0].reshape(heads, 1)
    tm = min(TM_PROJ, t_len)
    qt_p, kt_p, kb_p, vt_p, vtb_p, lft_p, cumt_p = _foxproj_call(
        xp, row(norm_mix_g[1]), mp[0], mp[1], wqkv_t, wf_t, b_f,
        tm=tm, tiles_per_seq=t_len // tm, transposed=True, period=tm)
    q_s, k_s, v_s, lft_s, cumt_s = _foxproj_call(
        xs, row(norm_mix_g[1]), ms[0], ms[1], wqkv_t, wf_t, b_f,
        tm=n_s, tiles_per_seq=1, transposed=False, period=t_new)

    nq, tq = qt_p.shape[1], qt_p.shape[3]
    o_p = _attn_call(qt_p, kb_p, vtb_p, cumt_p.reshape(n_seq_p, heads // 2, 2, nq, tq), t_len=t_len, heads=heads)

    page_size = cache_k.shape[2]
    n_phys = cache_k.shape[1]
    kc_t = jnp.transpose(cache_k[0], (0, 2, 3, 1)).reshape(n_phys, d, page_size)
    vc_t = jnp.transpose(cache_v[0], (0, 2, 3, 1)).reshape(n_phys, d, page_size)
    lfc_t = jnp.transpose(cache_logf[0], (0, 2, 1))
    cn = cumt_s[0].reshape(heads, n_seq_s, t_new).transpose(1, 0, 2)
    cn_pad = jnp.zeros((n_seq_s, heads, LANES), F32).at[:, :, :t_new].set(cn)
    o_s = _attn_sample_call(page_table, q_s, cn_pad, k_s, v_s, kc_t, vc_t, lfc_t, tn=t_new, heads=heads)

    wo_f = fox_w_out[0].astype(BF16)
    wr_pad = jnp.zeros((d, LANES), F32).at[:, :n_experts].set(moe_w_r[0])
    br_pad = jnp.zeros((1, LANES), F32).at[0, :n_experts].set(moe_b_r[0])
    tb = min(TM_POST, t_len)
    ms_blk = [m.reshape(n_s // tb, tb, d) for m in ms]
    x3, h_all, meta, meta_t, blk_cnt, counts = _post_call(
        o_p, o_s, wo_f, xp, xs, mp[2], ms_blk[2], row(norm_ffn_g[1]), mp[3], ms_blk[3], mp[4], ms_blk[4],
        wr_pad, br_pad, tm=tb, blocks_per_seq=t_len // tb, n_experts=n_experts)

    n_tok = n_p + n_s
    ts = TS_MOE
    n_super = (TOP_K * n_tok) // ts + n_experts + 1
    rt = _routing_tables(blk_cnt, counts, n_experts=n_experts, ts=ts, n_super=n_super)
    xs_sorted, g_sorted = _dispatch_call(rt["tile_lo"], rt["tile_hi"], rt["start"], meta_t, h_all,
                                         n_sorted=n_super * ts, n_experts=n_experts)
    ys_sorted = _moe_call(rt["sup_e"], rt["sup_nsub"], xs_sorted, g_sorted, moe_w_gu[0], moe_w_down[0], ts=ts)
    y_p, y_s = _combine_call(rt["win_tile"], rt["win_need2"], rt["start"], meta, x3, mp[5], ms_blk[5],
                             row(final_norm_g), ys_sorted,
                             n_p=n_p, tb=tb, blocks_per_seq=t_len // tb, n_experts=n_experts)

    y_prompt = y_p.reshape(n_seq_p, t_len, d)
    y_sample = y_s.reshape(n_seq_s, t_new, d)
    state_a_v_sample = v_rows.reshape(1, n_seq_s, t_new, da)
    k_prompt = kt_p.reshape(1, n_seq_p, heads, hd, t_len).transpose(0, 1, 4, 2, 3)
    v_prompt = vt_p.reshape(1, n_seq_p, heads, hd, t_len).transpose(0, 1, 4, 2, 3)
    logf_prompt = lft_p.transpose(0, 2, 1)[None]
    k_sample = k_s.reshape(1, n_seq_s, t_new, heads, hd)
    v_sample = v_s.reshape(1, n_seq_s, t_new, heads, hd)
    logf_sample = lft_s[0].T.reshape(1, n_seq_s, t_new, heads)
    return (y_prompt, y_sample, state_a_v_sample, k_prompt, v_prompt, logf_prompt, k_sample, v_sample, logf_sample)
```

```python
import functools

import jax
import jax.numpy as jnp
from jax import lax
from jax.experimental import pallas as pl
from jax.experimental.pallas import tpu as pltpu

F32 = jnp.float32
BF16 = jnp.bfloat16

N_HEADS = 16
GMLP_GROUPS = 8
GMLP_CHUNK = 128
TOP_K = 2
EPS = 1e-6
NEG = -1e30
LOG2E = 1.4426950408889634

LANES = 128
SUBLANES = 8
MXU_TILE = 256
VMEM_LIMIT = 56 * 1024 * 1024

TM_GMLP = 512
TM_FFN = 1024
TF_FFN = 512
TM_PROJ = 512
TQ_ATTN = 512
TM_POST = 256
SUB = 256
FFN_ROWS = 2 * SUB
TS_MOE = 2560
PAGES_PER_STEP = 16


def _cparams(*sem):
    return pltpu.CompilerParams(dimension_semantics=sem, vmem_limit_bytes=VMEM_LIMIT)


def _dot(a, b):
    return jnp.dot(a, b, preferred_element_type=F32)


def _dot_nt(a, b):
    return lax.dot_general(a, b, (((1,), (1,)), ((), ())), preferred_element_type=F32)


def _norm_mod(x, g, shift, scale):
    ms = jnp.mean(x * x, axis=-1, keepdims=True)
    y = x * lax.rsqrt(ms + EPS) * g
    return y * (1.0 + scale) + shift


def _gelu(x):
    return 0.5 * x * (1.0 + lax.erf(x * (2.0 ** -0.5)))


def _silu(x):
    return x * jax.nn.sigmoid(x)


def _ones_where(cond):
    return jnp.where(cond, 1.0, 0.0).astype(BF16)


def _shift_div(x, c):
    assert c & (c - 1) == 0
    return lax.shift_right_logical(x, c.bit_length() - 1)


def _split3(x):
    hi = x.astype(BF16)
    r = x - hi.astype(F32)
    mid = r.astype(BF16)
    lo = (r - mid.astype(F32)).astype(BF16)
    return hi, mid, lo


def _ada_kernel(c_ref, w_ref, b_ref, o_ref):
    s = _silu(c_ref[...]).astype(BF16)
    o_ref[...] = _dot(s, w_ref[...].astype(BF16)) + b_ref[...]


def _ada_call(c_all, ada_w, ada_b):
    n_layers, d, d6 = ada_w.shape
    r = c_all.shape[0]
    tn = min(d6, 1536)
    return pl.pallas_call(
        _ada_kernel,
        grid=(n_layers, d6 // tn),
        in_specs=[
            pl.BlockSpec((r, d), lambda l, j: (0, 0)),
            pl.BlockSpec((None, d, tn), lambda l, j: (l, 0, j)),
            pl.BlockSpec((None, 1, tn), lambda l, j: (l, 0, j)),
        ],
        out_specs=pl.BlockSpec((None, r, tn), lambda l, j: (l, 0, j)),
        out_shape=jax.ShapeDtypeStruct((n_layers, r, d6), F32),
        compiler_params=_cparams("arbitrary", "arbitrary"),
        name="ada",
    )(c_all, ada_w, ada_b.reshape(n_layers, 1, d6))


def _mod_spec(mod, tiles_per_seq):
    _, rows, d = mod.shape
    return pl.BlockSpec((None, rows, d), lambda i, *_: (i // tiles_per_seq, 0, 0))


def _gmlp_kernel(x_ref, g_ref, sh_ref, sc_ref, gt_ref, wu_ref, wv_ref, vg_ref, vb_ref, ws_ref, bs_ref,
                 wo_ref, *rest, lc, period, groups, emit_v):
    if emit_v:
        o_ref, v_ref, vn_scr, out_scr = rest
    else:
        o_ref, vn_scr, out_scr = rest
    x = x_ref[...]
    tm = x.shape[0]
    h = _norm_mod(x, g_ref[...], sh_ref[...], sc_ref[...]).astype(BF16)
    v = _gelu(_dot(h, wv_ref[...]))
    mu = jnp.mean(v, axis=-1, keepdims=True)
    vc = v - mu
    var = jnp.mean(vc * vc, axis=-1, keepdims=True)
    vn = vc * lax.rsqrt(var + EPS) * vg_ref[...] + vb_ref[...]
    if emit_v:
        v_ref[...] = vn
    vn_scr[...] = vn.astype(BF16)
    gd = vn.shape[1] // groups
    r = lax.broadcasted_iota(jnp.int32, (lc, lc), 0)
    c = lax.broadcasted_iota(jnp.int32, (lc, lc), 1)
    mask = c <= r
    if period < lc:
        blk = ~(period - 1)
        mask = mask & ((r & blk) == (c & blk))
        ri = lax.broadcasted_iota(jnp.int32, (lc, LANES), 0)
        ci = lax.broadcasted_iota(jnp.int32, (lc, LANES), 1)
        sel = _ones_where((ri & (period - 1)) == ci)
    for g in range(groups):
        u = _gelu(_dot(h, wu_ref[:, g * gd:(g + 1) * gd]))
        if period < lc:
            rows_of_block = _dot(sel, ws_ref[g].astype(BF16)).astype(BF16)
            ws_full = _dot_nt(rows_of_block, sel)
        else:
            ws_full = ws_ref[g]
        wsm = jnp.where(mask, ws_full, 0.0).astype(BF16)
        bcol = bs_ref[:, g:g + 1]
        for ci in range(tm // lc):
            rows = slice(ci * lc, (ci + 1) * lc)
            mixed = _dot(wsm, vn_scr[rows, g * gd:(g + 1) * gd]) + bcol
            out_scr[rows, g * gd:(g + 1) * gd] = (u[rows] * mixed).astype(BF16)
    o_ref[...] = x + gt_ref[...] * _dot(out_scr[...], wo_ref[...])


def _gmlp_call(x, g, shift, scale, gate, wu, wv, vg, vb, ws, bs_t, wo, *, tm, tiles_per_seq, period, emit_v):
    n, d = x.shape
    da = wu.shape[1]
    groups = ws.shape[0]
    lc = bs_t.shape[0]
    const2 = lambda i: (0, 0)
    in_specs = [
        pl.BlockSpec((tm, d), lambda i: (i, 0)),
        pl.BlockSpec((1, d), const2),
        _mod_spec(shift, tiles_per_seq), _mod_spec(scale, tiles_per_seq), _mod_spec(gate, tiles_per_seq),
        pl.BlockSpec((d, da), const2), pl.BlockSpec((d, da), const2),
        pl.BlockSpec((1, da), const2), pl.BlockSpec((1, da), const2),
        pl.BlockSpec(ws.shape, lambda i: (0, 0, 0)),
        pl.BlockSpec((lc, groups), const2),
        pl.BlockSpec((da, d), const2),
    ]
    out_specs = [pl.BlockSpec((tm, d), lambda i: (i, 0))]
    out_shape = [jax.ShapeDtypeStruct((n, d), F32)]
    if emit_v:
        out_specs.append(pl.BlockSpec((tm, da), lambda i: (i, 0)))
        out_shape.append(jax.ShapeDtypeStruct((n, da), F32))
    return pl.pallas_call(
        functools.partial(_gmlp_kernel, lc=lc, period=period, groups=groups, emit_v=emit_v),
        grid=(n // tm,),
        in_specs=in_specs, out_specs=out_specs, out_shape=out_shape,
        scratch_shapes=[pltpu.VMEM((tm, da), BF16), pltpu.VMEM((tm, da), BF16)],
        compiler_params=_cparams("arbitrary"),
        name="gmlp",
    )(x, g, shift, scale, gate, wu, wv, vg, vb, ws, bs_t, wo)


def _ffn_kernel(x_ref, g_ref, sh_ref, sc_ref, gt_ref, wg_ref, wu_ref, wd_ref, o_ref, h_scr, acc_scr):
    j = pl.program_id(1)

    @pl.when(j == 0)
    def _():
        h_scr[...] = _norm_mod(x_ref[...], g_ref[...], sh_ref[...], sc_ref[...]).astype(BF16)
        acc_scr[...] = jnp.zeros_like(acc_scr)

    h = h_scr[...]
    a = _silu(_dot(h, wg_ref[...].astype(BF16))) * _dot(h, wu_ref[...].astype(BF16))
    acc_scr[...] += _dot(a.astype(BF16), wd_ref[...].astype(BF16))

    @pl.when(j == pl.num_programs(1) - 1)
    def _():
        o_ref[...] = x_ref[...] + gt_ref[...] * acc_scr[...]


def _ffn_call(x, g, shift, scale, gate, w_gu, w_down, *, tm, tiles_per_seq):
    n, d = x.shape
    f = w_down.shape[0]
    tf = min(TF_FFN, f)
    nf = f // tf
    const2 = lambda i, j: (0, 0)
    return pl.pallas_call(
        _ffn_kernel,
        grid=(n // tm, nf),
        in_specs=[
            pl.BlockSpec((tm, d), lambda i, j: (i, 0)),
            pl.BlockSpec((1, d), const2),
            _mod_spec(shift, tiles_per_seq), _mod_spec(scale, tiles_per_seq), _mod_spec(gate, tiles_per_seq),
            pl.BlockSpec((d, tf), lambda i, j: (0, j)),
            pl.BlockSpec((d, tf), lambda i, j: (0, nf + j)),
            pl.BlockSpec((tf, d), lambda i, j: (j, 0)),
        ],
        out_specs=pl.BlockSpec((tm, d), lambda i, j: (i, 0)),
        out_shape=jax.ShapeDtypeStruct((n, d), F32),
        scratch_shapes=[pltpu.VMEM((tm, d), BF16), pltpu.VMEM((tm, d), F32)],
        compiler_params=_cparams("arbitrary", "arbitrary"),
        name="ffn",
    )(x, g, shift, scale, gate, w_gu, w_gu, w_down)


def _log_sigmoid(x):
    return jnp.minimum(x, 0.0) - jnp.log(1.0 + jnp.exp(-jnp.abs(x)))


def _foxproj_kernel(x_ref, g_ref, sh_ref, sc_ref, w_ref, wf_ref, bf_ref, *rest, transposed, period,
                    tiles_per_seq, q_scale):
    if transposed:
        q_ref, k_ref, kb_ref, v_ref, vb_ref, lf_ref, cum_ref, carry_scr = rest
    else:
        q_ref, k_ref, v_ref, lf_ref, cum_ref, carry_scr = rest
    i = pl.program_id(0)
    x = x_ref[...]
    tm, d = x.shape
    h = _norm_mod(x, g_ref[...], sh_ref[...], sc_ref[...]).astype(BF16)
    if transposed:
        tq = q_ref.shape[-1]
        qt = (_dot_nt(w_ref[0:d, :], h) * q_scale).astype(BF16)
        for jq in range(tm // tq):
            q_ref[jq] = qt[:, jq * tq:(jq + 1) * tq]
        k_ref[...] = _dot_nt(w_ref[d:2 * d, :], h)
        kb_ref[...] = _dot_nt(h, w_ref[d:2 * d, :]).astype(BF16)
        vt = _dot_nt(w_ref[2 * d:3 * d, :], h)
        v_ref[...] = vt
        vb_ref[...] = vt.astype(BF16)
    else:
        q_ref[...] = _dot_nt(h, w_ref[0:d, :]) * q_scale
        k_ref[...] = _dot_nt(h, w_ref[d:2 * d, :])
        v_ref[...] = _dot_nt(h, w_ref[2 * d:3 * d, :])
    logf = _log_sigmoid(_dot_nt(wf_ref[...], h) + bf_ref[...])
    lf_ref[...] = logf
    s = lax.broadcasted_iota(jnp.int32, (tm, tm), 0)
    t = lax.broadcasted_iota(jnp.int32, (tm, tm), 1)
    upper = s <= t
    if period < tm:
        blk = ~(period - 1)
        upper = upper & ((s & blk) == (t & blk))
    upper = _ones_where(upper)
    hi, mid, lo = _split3(logf)
    cum = _dot(hi, upper) + _dot(mid, upper) + _dot(lo, upper)
    if tiles_per_seq > 1:
        @pl.when(i % tiles_per_seq == 0)
        def _():
            carry_scr[...] = jnp.zeros_like(carry_scr)
        cum = cum + carry_scr[:, 0:1]
        carry_scr[...] = jnp.broadcast_to(cum[:, tm - 1:tm], carry_scr.shape)
    cum_ref[...] = cum


def _foxproj_call(x, g, shift, scale, w_t, wf_t, b_f, *, tm, tiles_per_seq, transposed, period):
    n, d = x.shape
    heads = wf_t.shape[0]
    n_seq = n // (tm * tiles_per_seq)
    t_len = tm * tiles_per_seq
    const2 = lambda i: (0, 0)
    row = lambda i: (i, 0)
    seq_t = lambda i: (i // tiles_per_seq, 0, i % tiles_per_seq)
    in_specs = [
        pl.BlockSpec((tm, d), row),
        pl.BlockSpec((1, d), const2),
        _mod_spec(shift, tiles_per_seq), _mod_spec(scale, tiles_per_seq),
        pl.BlockSpec((3 * d, d), const2),
        pl.BlockSpec((heads, d), const2),
        pl.BlockSpec((heads, 1), const2),
    ]
    lf_spec = pl.BlockSpec((None, heads, tm), seq_t)
    lf_shape = jax.ShapeDtypeStruct((n_seq, heads, t_len), F32)
    if transposed:
        tq = min(TQ_ATTN, tm)
        kv_spec = pl.BlockSpec((None, d, tm), seq_t)
        q_spec = pl.BlockSpec((None, tm // tq, d, tq), lambda i: (i // tiles_per_seq, i % tiles_per_seq, 0, 0))
        out_specs = [q_spec, kv_spec, pl.BlockSpec((tm, d), row), kv_spec, kv_spec, lf_spec, lf_spec]
        out_shape = [jax.ShapeDtypeStruct((n_seq, t_len // tq, d, tq), BF16),
                     jax.ShapeDtypeStruct((n_seq, d, t_len), F32), jax.ShapeDtypeStruct((n, d), BF16),
                     jax.ShapeDtypeStruct((n_seq, d, t_len), F32), jax.ShapeDtypeStruct((n_seq, d, t_len), BF16),
                     lf_shape, lf_shape]
    else:
        out_specs = [pl.BlockSpec((tm, d), row)] * 3 + [lf_spec, lf_spec]
        out_shape = [jax.ShapeDtypeStruct((n, d), F32)] * 3 + [lf_shape, lf_shape]
    return pl.pallas_call(
        functools.partial(_foxproj_kernel, transposed=transposed, period=period, tiles_per_seq=tiles_per_seq,
                          q_scale=float(d // heads) ** -0.5 * (LOG2E if transposed else 1.0)),
        grid=(n // tm,),
        in_specs=in_specs, out_specs=out_specs, out_shape=out_shape,
        scratch_shapes=[pltpu.VMEM((heads, LANES), F32)],
        compiler_params=_cparams("arbitrary"),
        name="foxproj",
    )(x, g, shift, scale, w_t, wf_t, b_f)


def _attn_kernel(qt_ref, k_ref, vt_ref, cum_ref, o_ref, ck_scr, m_scr, l_scr, acc_scr, *, hd):
    nq, _, tq = qt_ref.shape
    lane = lax.broadcasted_iota(jnp.int32, (tq, LANES), 1)
    key = lax.broadcasted_iota(jnp.int32, (tq, tq), 0)
    qry = lax.broadcasted_iota(jnp.int32, (tq, tq), 1)
    eye = key == qry
    causal = key <= qry
    m_scr[...] = jnp.full_like(m_scr, NEG)
    l_scr[...] = jnp.zeros_like(l_scr)
    acc_scr[...] = jnp.zeros_like(acc_scr)

    def update(hh, qi, kh, vh, masked):
        t = _dot(kh, qt_ref[qi]) - ck_scr[...]
        if masked:
            t = jnp.where(causal, t, NEG)
        cq = cum_ref[hh, qi:qi + 1, :] * LOG2E
        m_old = m_scr[hh, qi]
        m_new = jnp.maximum(m_old, jnp.max(t, axis=0, keepdims=True) + cq)
        alpha = jnp.exp2(m_old - m_new)
        p = jnp.exp2(t + (cq - m_new))
        l_scr[hh, qi] = alpha * l_scr[hh, qi] + jnp.sum(p, axis=0, keepdims=True)
        acc_scr[hh, qi] = alpha * acc_scr[hh, qi] + _dot(vh, p.astype(BF16))
        m_scr[hh, qi] = m_new

    for ki in range(nq):
        keys = slice(ki * tq, (ki + 1) * tq)
        k_blk = k_ref[keys, :]
        for hh in range(2):
            kh = jnp.where((lane >= hh * hd) & (lane < (hh + 1) * hd), k_blk, jnp.zeros_like(k_blk))
            vh = vt_ref[hh * hd:(hh + 1) * hd, keys]
            col = jnp.sum(jnp.where(eye, cum_ref[hh, ki:ki + 1, :], 0.0), axis=1, keepdims=True)
            ck_scr[...] = jnp.broadcast_to(col * LOG2E, ck_scr.shape)
            update(hh, ki, kh, vh, True)
            for qi in range(ki + 1, nq):
                update(hh, qi, kh, vh, False)

    for qi in range(nq):
        ot = jnp.concatenate([acc_scr[0, qi] / l_scr[0, qi], acc_scr[1, qi] / l_scr[1, qi]], axis=0)
        o_ref[qi * tq:(qi + 1) * tq, :] = ot.T.astype(o_ref.dtype)


def _attn_call(qt, k, vt, cum5, *, t_len, heads):
    n_seq, nq, d, tq = qt.shape
    hd = d // heads
    assert 2 * hd == LANES
    return pl.pallas_call(
        functools.partial(_attn_kernel, hd=hd),
        grid=(n_seq, heads // 2),
        in_specs=[
            pl.BlockSpec((None, nq, LANES, tq), lambda b, p: (b, 0, p, 0)),
            pl.BlockSpec((t_len, LANES), lambda b, p: (b, p)),
            pl.BlockSpec((None, LANES, t_len), lambda b, p: (b, p, 0)),
            pl.BlockSpec((None, None, 2, nq, tq), lambda b, p: (b, p, 0, 0, 0)),
        ],
        out_specs=pl.BlockSpec((t_len, LANES), lambda b, p: (b, p)),
        out_shape=jax.ShapeDtypeStruct((n_seq * t_len, d), BF16),
        scratch_shapes=[pltpu.VMEM((tq, tq), F32), pltpu.VMEM((2, nq, 1, tq), F32),
                        pltpu.VMEM((2, nq, 1, tq), F32), pltpu.VMEM((2, nq, hd, tq), F32)],
        compiler_params=_cparams("arbitrary", "arbitrary"),
        name="attn_prompt",
    )(qt, k, vt, cum5)


def _attn_sample_kernel(pt_ref, q_ref, cn_ref, kn_ref, vn_ref, *rest, pages, hd, page_size):
    k_refs = rest[0:pages]
    v_refs = rest[pages:2 * pages]
    lf_refs = rest[2 * pages:3 * pages]
    o_ref, qbd_scr, cn_scr, m_scr, l_scr, acc_scr, suf_scr = rest[3 * pages:]
    j = pl.program_id(1)
    tn, d = q_ref.shape
    heads = d // hd
    rows = heads * tn
    row = lax.broadcasted_iota(jnp.int32, (rows, LANES), 0)
    lane = lax.broadcasted_iota(jnp.int32, (rows, LANES), 1)
    q_of_row = row & (tn - 1)

    def rep(a):
        return jnp.broadcast_to(a[:, None, :], (heads, tn, a.shape[-1])).reshape(rows, a.shape[-1])

    gw = qbd_scr.shape[1]
    gr = (gw // hd) * tn
    n_groups = d // gw
    g_rows = lambda g: slice(g * gr, (g + 1) * gr)
    g_cols = lambda g: slice(g * gw, (g + 1) * gw)
    rr = lax.broadcasted_iota(jnp.int32, (gr, gw), 0)
    cc = lax.broadcasted_iota(jnp.int32, (gr, gw), 1)
    own_head = _shift_div(rr, tn) == _shift_div(cc, hd)

    @pl.when(j == 0)
    def _():
        q = q_ref[...]
        for g in range(n_groups):
            qrep = jnp.broadcast_to(q[None, :, g_cols(g)], (gw // hd, tn, gw)).reshape(gr, gw)
            qbd_scr[g_rows(g), :] = jnp.where(own_head, qrep, 0.0).astype(BF16)
        cn_scr[...] = jnp.sum(jnp.where(lane == q_of_row, rep(cn_ref[...]), 0.0), axis=1, keepdims=True)
        m_scr[...] = jnp.full_like(m_scr, NEG)
        l_scr[...] = jnp.zeros_like(l_scr)
        acc_scr[...] = jnp.zeros_like(acc_scr)
        suf_scr[...] = jnp.zeros_like(suf_scr)

    s_idx = lax.broadcasted_iota(jnp.int32, (page_size, page_size), 0)
    t_idx = lax.broadcasted_iota(jnp.int32, (page_size, page_size), 1)
    later = _ones_where(s_idx > t_idx)
    lf_all = jnp.concatenate([lf_refs[i][...] for i in range(pages)], axis=0)
    hi, mid, lo = _split3(lf_all)
    suf_all = _dot(hi, later) + _dot(mid, later) + _dot(lo, later)
    tot_all = suf_all[:, 0:1] + lf_all[:, 0:1]
    carry = suf_scr[:, 0:1]
    scores = [None] * pages
    for i in reversed(range(pages)):
        suf = suf_all[i * heads:(i + 1) * heads, :] + carry
        carry = carry + tot_all[i * heads:(i + 1) * heads, :]
        qk =[_dot(qbd_scr[g_rows(g), :], k_refs[i][g_cols(g), :].astype(BF16)) for g in range(n_groups)]
        scores[i] = jnp.concatenate(qk, axis=0) + rep(suf)
    suf_scr[...] = jnp.broadcast_to(carry, suf_scr.shape)
    s = jnp.concatenate(scores, axis=1) + cn_scr[...]
    m_old = m_scr[...]
    m_new = jnp.maximum(m_old, jnp.max(s, axis=1, keepdims=True))
    alpha = jnp.exp(m_old - m_new)
    p = jnp.exp(s - m_new)
    l_scr[...] = alpha * l_scr[...] + jnp.sum(p, axis=1, keepdims=True)
    pb = p.astype(BF16)
    for g in range(n_groups):
        pv = _dot_nt(pb[g_rows(g), 0:page_size], v_refs[0][g_cols(g), :].astype(BF16))
        for i in range(1, pages):
            pv = pv + _dot_nt(pb[g_rows(g), i * page_size:(i + 1) * page_size],
                              v_refs[i][g_cols(g), :].astype(BF16))
        acc_scr[g_rows(g), :] = alpha[g_rows(g), :] * acc_scr[g_rows(g), :] + pv
    m_scr[...] = m_new

    @pl.when(j == pl.num_programs(1) - 1)
    def _():
        pad = jnp.zeros((LANES - tn, d), BF16)
        k_new = jnp.concatenate([kn_ref[...].astype(BF16), pad], axis=0)
        v_new = jnp.concatenate([vn_ref[...].astype(BF16), pad], axis=0)
        qk = [_dot_nt(qbd_scr[g_rows(g), :], k_new[:, g_cols(g)]) for g in range(n_groups)]
        s = jnp.concatenate(qk, axis=0) + cn_scr[...] - rep(cn_ref[...])
        s = jnp.where(lane <= q_of_row, s, NEG)
        m_old = m_scr[...]
        m_new = jnp.maximum(m_old, jnp.max(s, axis=1, keepdims=True))
        alpha = jnp.exp(m_old - m_new)
        pb = jnp.exp(s - m_new)
        l_new = alpha * l_scr[...] + jnp.sum(pb, axis=1, keepdims=True)
        pb = pb.astype(BF16)
        for g in range(n_groups):
            o = alpha[g_rows(g), :] * acc_scr[g_rows(g), :] + _dot(pb[g_rows(g), :], v_new[:, g_cols(g)])
            o = jnp.where(own_head, o / l_new[g_rows(g), :], 0.0)
            o_ref[:, g_cols(g)] = jnp.sum(o.reshape(gw // hd, tn, gw), axis=0)


def _attn_sample_call(page_table, q, cn_pad, k_new, v_new, kc_t, vc_t, lfc_t, *, tn, heads):
    n, d = q.shape
    n_seq, n_pages = page_table.shape
    page_size = kc_t.shape[-1]
    pages = min(PAGES_PER_STEP, n_pages)
    n_steps = n_pages // pages
    hd = d // heads

    def page_map(i):
        def index(b, j, pt):
            return (pt[b * n_pages + (n_steps - 1 - j) * pages + i], 0, 0)
        return index

    row = lambda b, j, pt: (b, 0)
    in_specs = [
        pl.BlockSpec((tn, d), row),
        pl.BlockSpec((None, heads, LANES), lambda b, j, pt: (b, 0, 0)),
        pl.BlockSpec((tn, d), row), pl.BlockSpec((tn, d), row),
    ]
    in_specs += [pl.BlockSpec((None, d, page_size), page_map(i)) for i in range(pages)]
    in_specs += [pl.BlockSpec((None, d, page_size), page_map(i)) for i in range(pages)]
    in_specs += [pl.BlockSpec((None, heads, page_size), page_map(i)) for i in range(pages)]
    rows = heads * tn
    return pl.pallas_call(
        functools.partial(_attn_sample_kernel, pages=pages, hd=hd, page_size=page_size),
        grid_spec=pltpu.PrefetchScalarGridSpec(
            num_scalar_prefetch=1,
            grid=(n_seq, n_steps),
            in_specs=in_specs,
            out_specs=pl.BlockSpec((tn, d), row),
            scratch_shapes=[pltpu.VMEM((rows, MXU_TILE), BF16), pltpu.VMEM((rows, 1), F32),
                            pltpu.VMEM((rows, 1), F32), pltpu.VMEM((rows, 1), F32),
                            pltpu.VMEM((rows, MXU_TILE), F32), pltpu.VMEM((heads, LANES), F32)],
        ),
        out_shape=jax.ShapeDtypeStruct((n, d), F32),
        compiler_params=_cparams("arbitrary", "arbitrary"),
        name="attn_sample",
    )(page_table.reshape(-1), q, cn_pad, k_new, v_new, *([kc_t] * pages), *([vc_t] * pages), *([lfc_t] * pages))


def _post_kernel(op_ref, os_ref, wo_ref, xp_ref, xs_ref, gtp_ref, gts_ref, g_ref, shp_ref, shs_ref, scp_ref,
                 scs_ref, wr_ref, br_ref, x3_ref, h_ref, meta_ref, metat_ref, blk_ref, cout_ref, carry_scr,
                 *, n_experts, n_prompt_blocks):
    i = pl.program_id(0)
    tm = xp_ref.shape[0]
    is_p = i < n_prompt_blocks

    @pl.when(i == 0)
    def _():
        carry_scr[...] = jnp.zeros_like(carry_scr)

    o = jnp.where(is_p, op_ref[...], os_ref[...].astype(BF16))
    x = jnp.where(is_p, xp_ref[...], xs_ref[...])
    gate = jnp.where(is_p, gtp_ref[...], gts_ref[...])
    shift = jnp.where(is_p, shp_ref[...], shs_ref[...])
    scale = jnp.where(is_p, scp_ref[...], scs_ref[...])
    x3 = x + gate * _dot(o, wo_ref[...])
    x3_ref[...] = x3
    h = _norm_mod(x3, g_ref[...], shift, scale)
    h_hi = h.astype(BF16)
    h_ref[...] = h_hi
    h_lo = (h - h_hi.astype(F32)).astype(BF16)
    wr = wr_ref[...]
    w_hi = wr.astype(BF16)
    w_lo = (wr - w_hi.astype(F32)).astype(BF16)
    logits = _dot(h_hi, w_hi) + _dot(h_hi, w_lo) + _dot(h_lo, w_hi) + br_ref[...]
    lane_i = lax.broadcasted_iota(jnp.int32, (tm, LANES), 1)
    lane = lane_i.astype(F32)
    logits = jnp.where(lane_i < n_experts, logits, NEG)
    l1 = jnp.max(logits, axis=1, keepdims=True)
    i1 = jnp.min(jnp.where(logits == l1, lane, float(LANES)), axis=1, keepdims=True)
    rest = jnp.where(lane == i1, NEG, logits)
    l2 = jnp.max(rest, axis=1, keepdims=True)
    i2 = jnp.min(jnp.where(rest == l2, lane, float(LANES)), axis=1, keepdims=True)
    e = jnp.exp(l2 - l1)
    g1 = 1.0 / (1.0 + e)
    g2 = e / (1.0 + e)
    onehot = jnp.where((lane == i1) | (lane == i2), 1.0, 0.0)
    r = lax.broadcasted_iota(jnp.int32, (tm, tm), 0)
    c = lax.broadcasted_iota(jnp.int32, (tm, tm), 1)
    before = _ones_where(c < r)
    carry = carry_scr[0:1, :]
    blk_ref[...] = carry
    prefix = _dot(before, onehot.astype(BF16)) + carry
    r1 = jnp.sum(jnp.where(lane == i1, prefix, 0.0), axis=1, keepdims=True)
    r2 = jnp.sum(jnp.where(lane == i2, prefix, 0.0), axis=1, keepdims=True)
    cols = (i1, i2, g1, g2, r1, r2)
    meta = jnp.zeros((tm, LANES), F32)
    for k, col in enumerate(cols):
        meta = jnp.where(lane_i == k, col, meta)
    meta_ref[...] = meta
    metat_ref[...] = meta.T[0:SUBLANES, :]
    carry = carry + jnp.sum(onehot, axis=0, keepdims=True)
    carry_scr[...] = jnp.broadcast_to(carry, carry_scr.shape)
    cout_ref[...] = carry


def _post_call(o_p, o_s, wo, x_p, x_s, gate_p, gate_s, g, shift_p, shift_s, scale_p, scale_s, wr_pad, br_pad,
               *, tm, blocks_per_seq, n_experts):
    n_p, d = x_p.shape
    n_s = x_s.shape[0]
    npb = n_p // tm
    nb = npb + n_s // tm
    n = n_p + n_s
    const2 = lambda i: (0, 0)
    row = lambda i: (i, 0)
    p_row = lambda i: (jnp.minimum(i, npb - 1), 0)
    s_row = lambda i: (jnp.maximum(i - npb, 0), 0)
    p_mod = pl.BlockSpec((None, 1, d), lambda i: (jnp.minimum(i, npb - 1) // blocks_per_seq, 0, 0))
    s_mod = pl.BlockSpec((None, tm, d), lambda i: (jnp.maximum(i - npb, 0), 0, 0))
    return pl.pallas_call(
        functools.partial(_post_kernel, n_experts=n_experts, n_prompt_blocks=npb),
        grid=(nb,),
        in_specs=[
            pl.BlockSpec((tm, d), p_row), pl.BlockSpec((tm, d), s_row),
            pl.BlockSpec((d, d), const2),
            pl.BlockSpec((tm, d), p_row), pl.BlockSpec((tm, d), s_row),
            p_mod, s_mod,
            pl.BlockSpec((1, d), const2),
            p_mod, s_mod, p_mod, s_mod,
            pl.BlockSpec((d, LANES), const2),
            pl.BlockSpec((1, LANES), const2),
        ],
        out_specs=[
            pl.BlockSpec((tm, d), row), pl.BlockSpec((tm, d), row), pl.BlockSpec((tm, LANES), row),
            pl.BlockSpec((None, SUBLANES, tm), lambda i: (i, 0, 0)),
            pl.BlockSpec((None, 1, LANES), lambda i: (i, 0, 0)),
            pl.BlockSpec((1, LANES), const2),
        ],
        out_shape=[
            jax.ShapeDtypeStruct((n, d), F32), jax.ShapeDtypeStruct((n, d), BF16),
            jax.ShapeDtypeStruct((n, LANES), F32),
            jax.ShapeDtypeStruct((nb, SUBLANES, tm), F32),
            jax.ShapeDtypeStruct((nb, 1, LANES), F32),
            jax.ShapeDtypeStruct((1, LANES), F32),
        ],
        scratch_shapes=[pltpu.VMEM((SUBLANES, LANES), F32)],
        compiler_params=_cparams("arbitrary"),
        name="post_attn_router",
    )(o_p, o_s, wo, x_p, x_s, gate_p, gate_s, g, shift_p, shift_s, scale_p, scale_s, wr_pad, br_pad)


def _sorted_pos(idx, rank, start_ref, n_experts):
    pos = rank
    for e in range(n_experts):
        pos = pos + jnp.where(idx == float(e), start_ref[e].astype(F32), 0.0)
    return pos


def _dispatch_kernel(lo_ref, hi_ref, start_ref, mt_ref, h_ref, xs_ref, gs_ref, acc_scr, gacc_scr,
                     *, tb, n_experts):
    j = pl.program_id(0)
    sub = xs_ref.shape[0]
    nb = mt_ref.shape[0]
    lo = lo_ref[j]
    hi = hi_ref[j]

    @pl.when(hi <= lo)
    def _():
        xs_ref[...] = jnp.zeros_like(xs_ref)
        gs_ref[...] = jnp.zeros_like(gs_ref)

    def hits(b, width):
        dest = (j * sub + lax.broadcasted_iota(jnp.int32, (sub, width * tb), 0)).astype(F32)
        mts = [mt_ref[b + w] for w in range(width)]
        field = lambda r: jnp.concatenate([m[r:r + 1, :] for m in mts], axis=1)
        hit0 = _sorted_pos(field(0), field(4), start_ref, n_experts) == dest
        hit1 = _sorted_pos(field(1), field(5), start_ref, n_experts) == dest
        gate = jnp.where(hit0, field(2), 0.0) + jnp.where(hit1, field(3), 0.0)
        return _ones_where(hit0 | hit1), gate

    n_even = nb - nb % 2

    def pair(m, carry):
        b = 2 * m
        oh, gate = hits(b, 2)
        rows = pl.ds(pl.multiple_of(b * tb, 2 * tb), 2 * tb)
        acc_scr[...] += _dot(oh, h_ref[rows, :])
        gacc_scr[...] += gate
        return carry

    @pl.when(hi > lo)
    def _():
        acc_scr[...] = jnp.zeros_like(acc_scr)
        gacc_scr[...] = jnp.zeros_like(gacc_scr)
        lax.fori_loop(lo // 2, (jnp.minimum(hi, n_even) + 1) // 2, pair, 0)

        if nb % 2:
            @pl.when(hi == nb)
            def _():
                oh, gate = hits(nb - 1, 1)
                acc_scr[...] += _dot(oh, h_ref[(nb - 1) * tb:nb * tb, :])
                gacc_scr[:, 0:tb] += gate

        xs_ref[...] = acc_scr[...].astype(BF16)
        gs_ref[...] = jnp.broadcast_to(jnp.sum(gacc_scr[...], axis=1, keepdims=True), gs_ref.shape)


def _dispatch_call(tile_lo, tile_hi, start, meta_t, h, *, n_sorted, n_experts):
    n, d = h.shape
    nb, _, tb = meta_t.shape
    whole = lambda nd: (lambda j, lo, hi, st: (0,) * nd)
    once = pl.Buffered(1)
    return pl.pallas_call(
        functools.partial(_dispatch_kernel, tb=tb, n_experts=n_experts),
        grid_spec=pltpu.PrefetchScalarGridSpec(
            num_scalar_prefetch=3,
            grid=(n_sorted // SUB,),
            in_specs=[
                pl.BlockSpec((nb, SUBLANES, tb), whole(3), pipeline_mode=once),
                pl.BlockSpec((n, d), whole(2), pipeline_mode=once),
            ],
            out_specs=[pl.BlockSpec((SUB, d), lambda j, lo, hi, st: (j, 0)),
                       pl.BlockSpec((SUB, LANES), lambda j, lo, hi, st: (j, 0))],
            scratch_shapes=[pltpu.VMEM((SUB, d), F32), pltpu.VMEM((SUB, 2 * tb), F32)],
        ),
        out_shape=[jax.ShapeDtypeStruct((n_sorted, d), BF16), jax.ShapeDtypeStruct((n_sorted, LANES), F32)],
        compiler_params=_cparams("arbitrary"),
        name="moe_dispatch",
    )(tile_lo, tile_hi, start, meta_t, h)


def _moe_kernel(ex_ref, nsub_ref, x_ref, gs_ref, wg_ref, wu_ref, wd_ref, y_ref, acc_scr):
    s = pl.program_id(0)
    j = pl.program_id(1)
    nsub = nsub_ref[s]
    last = j == pl.num_programs(1) - 1

    n_full = (nsub * SUB) // FFN_ROWS
    odd = nsub * SUB - n_full * FFN_ROWS > 0

    def tile_rows(t):
        return pl.ds(pl.multiple_of(t * SUB, SUB), SUB)

    def step_rows(t):
        return pl.ds(pl.multiple_of(t * FFN_ROWS, FFN_ROWS), FFN_ROWS)

    @pl.when(j == 0)
    def _():
        def zero(t, carry):
            acc_scr[tile_rows(t), :] = jnp.zeros((SUB, acc_scr.shape[1]), F32)
            return carry
        lax.fori_loop(0, nsub, zero, 0)

    def ffn(rows):
        x = x_ref[rows, :]
        a = _silu(_dot(x, wg_ref[...].astype(BF16))) * _dot(x, wu_ref[...].astype(BF16))
        acc_scr[rows, :] += _dot(a.astype(BF16), wd_ref[...].astype(BF16))

    def body(t, carry):
        ffn(step_rows(t))
        return carry

    lax.fori_loop(0, n_full, body, 0)

    @pl.when(odd)
    def _():
        ffn(tile_rows(nsub - 1))

    @pl.when(last)
    def _():
        def emit(t, carry):
            rows = tile_rows(t)
            y_ref[rows, :] = (acc_scr[rows, :] * gs_ref[rows, 0:1]).astype(BF16)
            return carry
        lax.fori_loop(0, nsub, emit, 0)

        def zero(t, carry):
            y_ref[tile_rows(t), :] = jnp.zeros((SUB, y_ref.shape[1]), BF16)
            return carry
        lax.fori_loop(nsub, y_ref.shape[0] // SUB, zero, 0)


def _moe_call(sup_expert, sup_nsub, xs, gs, w_gu, w_down, *, ts):
    n_sorted, d = xs.shape
    f = w_down.shape[1]
    tf = min(TF_FFN, f)
    nf = f // tf
    n_super = n_sorted // ts

    def jj(s, j, nsub):
        return jnp.where(nsub[s] > 0, j, nf - 1)

    return pl.pallas_call(
        _moe_kernel,
        grid_spec=pltpu.PrefetchScalarGridSpec(
            num_scalar_prefetch=2,
            grid=(n_super, nf),
            in_specs=[
                pl.BlockSpec((ts, d), lambda s, j, ex, ns: (s, 0)),
                pl.BlockSpec((ts, LANES), lambda s, j, ex, ns: (s, 0)),
                pl.BlockSpec((None, d, tf), lambda s, j, ex, ns: (ex[s], 0, jj(s, j, ns))),
                pl.BlockSpec((None, d, tf), lambda s, j, ex, ns: (ex[s], 0, nf + jj(s, j, ns))),
                pl.BlockSpec((None, tf, d), lambda s, j, ex, ns: (ex[s], jj(s, j, ns), 0)),
            ],
            out_specs=pl.BlockSpec((ts, d), lambda s, j, ex, ns: (s, 0)),
            scratch_shapes=[pltpu.VMEM((ts, d), F32)],
        ),
        out_shape=jax.ShapeDtypeStruct((n_sorted, d), BF16),
        compiler_params=_cparams("arbitrary", "arbitrary"),
        name="moe_ffn",
    )(sup_expert, sup_nsub, xs, gs, w_gu, w_gu, w_down)


def _combine_kernel(tile_ref, need2_ref, start_ref, meta_ref, x_ref, gtp_ref, gts_ref, gf_ref, *rest,
                    n_experts, n_prompt_blocks):
    y_refs = rest[0:2 * n_experts]
    op_ref, os_ref, acc_scr = rest[2 * n_experts:]
    b = pl.program_id(0)
    tb = acc_scr.shape[0]
    meta = meta_ref[...]
    col = lax.broadcasted_iota(jnp.int32, (tb, SUB), 1)

    def window(e, w):
        first = start_ref[e].astype(F32)
        p0 = jnp.where(meta[:, 0:1] == float(e), meta[:, 4:5] + first, -1.0)
        p1 = jnp.where(meta[:, 1:2] == float(e), meta[:, 5:6] + first, -1.0)
        src = ((tile_ref[b * n_experts + e] + w) * SUB + col).astype(F32)
        return _ones_where((p0 == src) | (p1 == src))

    total = _dot(window(0, 0), y_refs[0][...])
    for e in range(1, n_experts):
        total = total + _dot(window(e, 0), y_refs[2 * e][...])
    acc_scr[...] = total
    for e in range(n_experts):
        @pl.when(need2_ref[b * n_experts + e] > 0)
        def _(e=e):
            acc_scr[...] += _dot(window(e, 1), y_refs[2 * e + 1][...])

    is_p = b < n_prompt_blocks
    x4 = x_ref[...] + jnp.where(is_p, gtp_ref[...], gts_ref[...]) * acc_scr[...]
    ms = jnp.mean(x4 * x4, axis=-1, keepdims=True)
    y = x4 * lax.rsqrt(ms + EPS) * gf_ref[...]

    @pl.when(is_p)
    def _():
        op_ref[...] = y

    @pl.when(jnp.logical_not(is_p))
    def _():
        os_ref[...] = y


def _combine_call(win_tile, win_need2, start, meta, x, gate_p, gate_s, g_final, ys, *, n_p, tb, blocks_per_seq,
                  n_experts):
    n, d = x.shape
    n_s = n - n_p
    npb = n_p // tb
    nb = n // tb

    def pb(b):
        return jnp.minimum(b, npb - 1)

    def sb(b):
        return jnp.maximum(b - npb, 0)

    def win_map(e, w):
        if w == 0:
            return lambda b, tile, need, st: (tile[b * n_experts + e], 0)
        return lambda b, tile, need, st: (jnp.where(need[b * n_experts + e] > 0, tile[b * n_experts + e] + 1, 0), 0)

    in_specs = [
        pl.BlockSpec((tb, LANES), lambda b, tile, need, st: (b, 0)),
        pl.BlockSpec((tb, d), lambda b, tile, need, st: (b, 0)),
        pl.BlockSpec((None, 1, d), lambda b, tile, need, st: (pb(b) // blocks_per_seq, 0, 0)),
        pl.BlockSpec((None, tb, d), lambda b, tile, need, st: (sb(b), 0, 0)),
        pl.BlockSpec((1, d), lambda b, tile, need, st: (0, 0)),
    ]
    in_specs += [pl.BlockSpec((SUB, d), win_map(e, w)) for e in range(n_experts) for w in range(2)]
    return pl.pallas_call(
        functools.partial(_combine_kernel, n_experts=n_experts, n_prompt_blocks=npb),
        grid_spec=pltpu.PrefetchScalarGridSpec(
            num_scalar_prefetch=3,
            grid=(nb,),
            in_specs=in_specs,
            out_specs=[pl.BlockSpec((tb, d), lambda b, tile, need, st: (pb(b), 0)),
                       pl.BlockSpec((tb, d), lambda b, tile, need, st: (sb(b), 0))],
            scratch_shapes=[pltpu.VMEM((tb, d), F32)],
        ),
        out_shape=[jax.ShapeDtypeStruct((n_p, d), F32), jax.ShapeDtypeStruct((n_s, d), F32)],
        compiler_params=_cparams("arbitrary"),
        name="moe_combine",
    )(win_tile, win_need2, start, meta, x, gate_p, gate_s, g_final, *([ys] * (2 * n_experts)))


def _routing_tables(blk_cnt, counts, *, n_experts, ts, n_super):
    nb = blk_cnt.shape[0]
    cnt = counts[0, :n_experts].astype(jnp.int32)
    n_sup_e = (cnt + ts - 1) // ts
    sup_start = jnp.cumsum(n_sup_e) - n_sup_e
    start = sup_start * ts

    s_ids = jnp.arange(n_super, dtype=jnp.int32)
    used = jnp.sum(n_sup_e)
    sup_end = sup_start + n_sup_e
    sup_e = jnp.sum((s_ids[:, None] >= sup_end[None, :]).astype(jnp.int32), axis=1)
    sup_e = jnp.clip(sup_e, 0, n_experts - 1)
    last_e = jnp.max(jnp.where(cnt > 0, jnp.arange(n_experts, dtype=jnp.int32), 0))
    sup_e = jnp.where(s_ids < used, sup_e, last_e)
    rows_in = jnp.clip(cnt[sup_e] - (s_ids - sup_start[sup_e]) * ts, 0, ts)
    sup_nsub = jnp.where(s_ids < used, (rows_in + SUB - 1) // SUB, 0).astype(jnp.int32)

    blk = blk_cnt[:, 0, :n_experts].astype(jnp.int32)
    blk_end = jnp.concatenate([blk[1:], cnt[None, :]], axis=0)
    n_tiles = n_super * (ts // SUB)
    t_ids = jnp.arange(n_tiles, dtype=jnp.int32)
    t_e = sup_e[t_ids // (ts // SUB)]
    t_r0 = t_ids * SUB - start[t_e]
    t_active = ((t_ids // (ts // SUB)) < used) & (t_r0 < cnt[t_e]) & (t_r0 >= 0)
    t_r1 = jnp.minimum(t_r0 + SUB, cnt[t_e])
    be = blk_end[:, t_e]
    bs = blk[:, t_e]
    overlap = (be > t_r0[None, :]) & (bs < t_r1[None, :])
    b_ids = jnp.arange(nb, dtype=jnp.int32)[:, None]
    lo = jnp.min(jnp.where(overlap, b_ids, nb), axis=0)
    hi = jnp.max(jnp.where(overlap, b_ids + 1, 0), axis=0)
    tile_lo = jnp.where(t_active, lo, 0).astype(jnp.int32)
    tile_hi = jnp.where(t_active, hi, 0).astype(jnp.int32)

    first = start[None, :] + blk
    n_be = blk_end - blk
    win_tile = (first // SUB).astype(jnp.int32)
    win_need2 = ((first + n_be) > (win_tile + 1) * SUB).astype(jnp.int32)

    return dict(start=start.astype(jnp.int32), sup_e=sup_e, sup_nsub=sup_nsub, tile_lo=tile_lo,
                tile_hi=tile_hi, win_tile=win_tile.reshape(-1), win_need2=win_need2.reshape(-1))


def kernel(x_prompt, x_sample, c_prompt, c_sample, cache_k, cache_v, cache_logf, page_table, norm_mix_g, norm_ffn_g, final_norm_g, ada_w, ada_b, gmlp_w_in, gmlp_v_g, gmlp_v_b, gmlp_w_s, gmlp_b_s, gmlp_w_out, fox_w_in, fox_b_f, fox_w_out, ffn_w_gu, ffn_w_down, moe_w_r, moe_b_r, moe_w_gu, moe_w_down):
    n_seq_p, t_len, d = x_prompt.shape
    n_seq_s, t_new, _ = x_sample.shape
    n_p = n_seq_p * t_len
    n_s = n_seq_s * t_new
    heads = N_HEADS
    hd = d // heads
    n_experts = moe_w_r.shape[-1]
    da = gmlp_w_out.shape[1]

    xp = x_prompt.reshape(n_p, d)
    xs = x_sample.reshape(n_s, d)

    mod = _ada_call(jnp.concatenate([c_prompt, c_sample], axis=0), ada_w, ada_b)

    def mods(layer):
        mp = [mod[layer, :n_seq_p, c * d:(c + 1) * d].reshape(n_seq_p, 1, d) for c in range(6)]
        ms = [jnp.repeat(mod[layer, n_seq_p:, c * d:(c + 1) * d], t_new, axis=0).reshape(1, n_s, d)
              for c in range(6)]
        return mp, ms

    row = lambda a: a.reshape(1, -1)

    mp, ms = mods(0)
    w_in = gmlp_w_in[0].astype(BF16)
    wu, wv = w_in[:, :da], w_in[:, da:]
    wo = gmlp_w_out[0].astype(BF16)
    vg, vb = row(gmlp_v_g[0]), row(gmlp_v_b[0])
    lc = min(GMLP_CHUNK, t_len)
    tm = min(TM_GMLP, t_len)
    xp = _gmlp_call(xp, row(norm_mix_g[0]), mp[0], mp[1], mp[2], wu, wv, vg, vb,
                    gmlp_w_s[0][:, :lc, :lc], gmlp_b_s[0][:, :lc].T, wo,
                    tm=tm, tiles_per_seq=t_len // tm, period=lc, emit_v=False)[0]
    reps = n_s // t_new
    ws_s = jnp.zeros((gmlp_w_s.shape[1], LANES, LANES), F32).at[:, :t_new, :t_new].set(
        gmlp_w_s[0][:, :t_new, :t_new])
    bs_s = jnp.tile(gmlp_b_s[0][:, :t_new], (1, reps)).T
    xs, v_rows = _gmlp_call(xs, row(norm_mix_g[0]), ms[0], ms[1], ms[2], wu, wv, vg, vb, ws_s, bs_s, wo,
                            tm=n_s, tiles_per_seq=1, period=t_new, emit_v=True)
    tm = min(TM_FFN, t_len)
    xp = _ffn_call(xp, row(norm_ffn_g[0]), mp[3], mp[4], mp[5], ffn_w_gu[0], ffn_w_down[0],
                   tm=tm, tiles_per_seq=t_len // tm)
    xs = _ffn_call(xs, row(norm_ffn_g[0]), ms[3], ms[4], ms[5], ffn_w_gu[0], ffn_w_down[0],
                   tm=n_s, tiles_per_seq=1)

    mp, ms = mods(1)
    w_t = fox_w_in[0].T.astype(BF16)
    wqkv_t, wf_t = w_t[:3 * d], w_t[3 * d:]
    b_f = fox_b_f[0].reshape(heads, 1)
    tm = min(TM_PROJ, t_len)
    qt_p, kt_p, kb_p, vt_p, vtb_p, lft_p, cumt_p = _foxproj_call(
        xp, row(norm_mix_g[1]), mp[0], mp[1], wqkv_t, wf_t, b_f,
        tm=tm, tiles_per_seq=t_len // tm, transposed=True, period=tm)
    q_s, k_s, v_s, lft_s, cumt_s = _foxproj_call(
        xs, row(norm_mix_g[1]), ms[0], ms[1], wqkv_t, wf_t, b_f,
        tm=n_s, tiles_per_seq=1, transposed=False, period=t_new)

    nq, tq = qt_p.shape[1], qt_p.shape[3]
    o_p = _attn_call(qt_p, kb_p, vtb_p, cumt_p.reshape(n_seq_p, heads // 2, 2, nq, tq), t_len=t_len, heads=heads)

    page_size = cache_k.shape[2]
    n_phys = cache_k.shape[1]
    kc_t = jnp.transpose(cache_k[0], (0, 2, 3, 1)).reshape(n_phys, d, page_size)
    vc_t = jnp.transpose(cache_v[0], (0, 2, 3, 1)).reshape(n_phys, d, page_size)
    lfc_t = jnp.transpose(cache_logf[0], (0, 2, 1))
    cn = cumt_s[0].reshape(heads, n_seq_s, t_new).transpose(1, 0, 2)
    cn_pad = jnp.zeros((n_seq_s, heads, LANES), F32).at[:, :, :t_new].set(cn)
    o_s = _attn_sample_call(page_table, q_s, cn_pad, k_s, v_s, kc_t, vc_t, lfc_t, tn=t_new, heads=heads)

    wo_f = fox_w_out[0].astype(BF16)
    wr_pad = jnp.zeros((d, LANES), F32).at[:, :n_experts].set(moe_w_r[0])
    br_pad = jnp.zeros((1, LANES), F32).at[0, :n_experts].set(moe_b_r[0])
    tb = min(TM_POST, t_len)
    ms_blk = [m.reshape(n_s // tb, tb, d) for m in ms]
    x3, h_all, meta, meta_t, blk_cnt, counts = _post_call(
        o_p, o_s, wo_f, xp, xs, mp[2], ms_blk[2], row(norm_ffn_g[1]), mp[3], ms_blk[3], mp[4], ms_blk[4],
        wr_pad, br_pad, tm=tb, blocks_per_seq=t_len // tb, n_experts=n_experts)

    n_tok = n_p + n_s
    ts = TS_MOE
    n_super = (TOP_K * n_tok) // ts + n_experts + 1
    rt = _routing_tables(blk_cnt, counts, n_experts=n_experts, ts=ts, n_super=n_super)
    xs_sorted, g_sorted = _dispatch_call(rt["tile_lo"], rt["tile_hi"], rt["start"], meta_t, h_all,
                                         n_sorted=n_super * ts, n_experts=n_experts)
    ys_sorted = _moe_call(rt["sup_e"], rt["sup_nsub"], xs_sorted, g_sorted, moe_w_gu[0], moe_w_down[0], ts=ts)
    y_p, y_s = _combine_call(rt["win_tile"], rt["win_need2"], rt["start"], meta, x3, mp[5], ms_blk[5],
                             row(final_norm_g), ys_sorted,
                             n_p=n_p, tb=tb, blocks_per_seq=t_len // tb, n_experts=n_experts)

    y_prompt = y_p.reshape(n_seq_p, t_len, d)
    y_sample = y_s.reshape(n_seq_s, t_new, d)
    state_a_v_sample = v_rows.reshape(1, n_seq_s, t_new, da)
    k_prompt = kt_p.reshape(1, n_seq_p, heads, hd, t_len).transpose(0, 1, 4, 2, 3)
    v_prompt = vt_p.reshape(1, n_seq_p, heads, hd, t_len).transpose(0, 1, 4, 2, 3)
    logf_prompt = lft_p.transpose(0, 2, 1)[None]
    k_sample = k_s.reshape(1, n_seq_s, t_new, heads, hd)
    v_sample = v_s.reshape(1, n_seq_s, t_new, heads, hd)
    logf_sample = lft_s[0].T.reshape(1, n_seq_s, t_new, heads)
    return (y_prompt, y_sample, state_a_v_sample, k_prompt, v_prompt, logf_prompt, k_sample, v_sample, logf_sample)
```

```python
import functools

import jax
import jax.numpy as jnp
from jax import lax
from jax.experimental import pallas as pl
from jax.experimental.pallas import tpu as pltpu

F32 = jnp.float32
BF16 = jnp.bfloat16

N_HEADS = 16
GMLP_GROUPS = 8
GMLP_CHUNK = 128
TOP_K = 2
EPS = 1e-6
NEG = -1e30
LOG2E = 1.4426950408889634

LANES = 128
SUBLANES = 8
MXU_TILE = 256
VMEM_LIMIT = 56 * 1024 * 1024

TM_FFN = 1024
TF_FFN = 512
TM_PROJ = 512
TQ_ATTN = 512
TM_POST = 256
SUB = 256
FFN_ROWS = 2 * SUB
TS_MOE = 2560
PAGES_PER_STEP = 16


def _cparams(*sem):
    return pltpu.CompilerParams(dimension_semantics=sem, vmem_limit_bytes=VMEM_LIMIT)


def _dot(a, b):
    return jnp.dot(a, b, preferred_element_type=F32)


def _dot_nt(a, b):
    return lax.dot_general(a, b, (((1,), (1,)), ((), ())), preferred_element_type=F32)


def _norm_mod(x, g, shift, scale):
    ms = jnp.mean(x * x, axis=-1, keepdims=True)
    y = x * lax.rsqrt(ms + EPS) * g
    return y * (1.0 + scale) + shift


def _gelu(x):
    return 0.5 * x * (1.0 + lax.erf(x * (2.0 ** -0.5)))


def _silu(x):
    return x * jax.nn.sigmoid(x)


def _ones_where(cond):
    return jnp.where(cond, 1.0, 0.0).astype(BF16)


def _shift_div(x, c):
    assert c & (c - 1) == 0
    return lax.shift_right_logical(x, c.bit_length() - 1)


def _split3(x):
    hi = x.astype(BF16)
    r = x - hi.astype(F32)
    mid = r.astype(BF16)
    lo = (r - mid.astype(F32)).astype(BF16)
    return hi, mid, lo


def _ada_kernel(c_ref, w_ref, b_ref, o_ref):
    s = _silu(c_ref[...]).astype(BF16)
    o_ref[...] = _dot(s, w_ref[...].astype(BF16)) + b_ref[...]


def _ada_call(c_all, ada_w, ada_b):
    n_layers, d, d6 = ada_w.shape
    r = c_all.shape[0]
    tn = min(d6, 1536)
    return pl.pallas_call(
        _ada_kernel,
        grid=(n_layers, d6 // tn),
        in_specs=[
            pl.BlockSpec((r, d), lambda l, j: (0, 0)),
            pl.BlockSpec((None, d, tn), lambda l, j: (l, 0, j)),
            pl.BlockSpec((None, 1, tn), lambda l, j: (l, 0, j)),
        ],
        out_specs=pl.BlockSpec((None, r, tn), lambda l, j: (l, 0, j)),
        out_shape=jax.ShapeDtypeStruct((n_layers, r, d6), F32),
        compiler_params=_cparams("arbitrary", "arbitrary"),
        name="ada",
    )(c_all, ada_w, ada_b.reshape(n_layers, 1, d6))


def _mod_spec(mod, tiles_per_seq):
    _, rows, d = mod.shape
    return pl.BlockSpec((None, rows, d), lambda i, *_: (i // tiles_per_seq, 0, 0))


def _gmlp_kernel(x_ref, g_ref, sh_ref, sc_ref, gt_ref, wu_ref, wv_ref, vg_ref, vb_ref, ws_ref, bs_ref,
                 wo_ref, *rest, lc, period, groups, emit_v):
    if emit_v:
        o_ref, v_ref, vn_scr, out_scr = rest
    else:
        o_ref, vn_scr, out_scr = rest
    x = x_ref[...]
    tm = x.shape[0]
    h = _norm_mod(x, g_ref[...], sh_ref[...], sc_ref[...]).astype(BF16)
    v = _gelu(_dot(h, wv_ref[...]))
    mu = jnp.mean(v, axis=-1, keepdims=True)
    vc = v - mu
    var = jnp.mean(vc * vc, axis=-1, keepdims=True)
    vn = vc * lax.rsqrt(var + EPS) * vg_ref[...] + vb_ref[...]
    if emit_v:
        v_ref[...] = vn
    vn_scr[...] = vn.astype(BF16)
    gd = vn.shape[1] // groups
    r = lax.broadcasted_iota(jnp.int32, (lc, lc), 0)
    c = lax.broadcasted_iota(jnp.int32, (lc, lc), 1)
    mask = c <= r
    if period < lc:
        blk = ~(period - 1)
        mask = mask & ((r & blk) == (c & blk))
        ri = lax.broadcasted_iota(jnp.int32, (lc, LANES), 0)
        ci = lax.broadcasted_iota(jnp.int32, (lc, LANES), 1)
        sel = _ones_where((ri & (period - 1)) == ci)
    for g in range(groups):
        u = _gelu(_dot(h, wu_ref[:, g * gd:(g + 1) * gd]))
        if period < lc:
            rows_of_block = _dot(sel, ws_ref[g].astype(BF16)).astype(BF16)
            ws_full = _dot_nt(rows_of_block, sel)
        else:
            ws_full = ws_ref[g]
        wsm = jnp.where(mask, ws_full, 0.0).astype(BF16)
        bcol = bs_ref[:, g:g + 1]
        for ci in range(tm // lc):
            rows = slice(ci * lc, (ci + 1) * lc)
            mixed = _dot(wsm, vn_scr[rows, g * gd:(g + 1) * gd]) + bcol
            out_scr[rows, g * gd:(g + 1) * gd] = (u[rows] * mixed).astype(BF16)
    o_ref[...] = x + gt_ref[...] * _dot(out_scr[...], wo_ref[...])


def _gmlp_call(x, g, shift, scale, gate, wu, wv, vg, vb, ws, bs_t, wo, *, tm, tiles_per_seq, period, emit_v):
    n, d = x.shape
    da = wu.shape[1]
    groups = ws.shape[0]
    lc = bs_t.shape[0]
    const2 = lambda i: (0, 0)
    in_specs = [
        pl.BlockSpec((tm, d), lambda i: (i, 0)),
        pl.BlockSpec((1, d), const2),
        _mod_spec(shift, tiles_per_seq), _mod_spec(scale, tiles_per_seq), _mod_spec(gate, tiles_per_seq),
        pl.BlockSpec((d, da), const2), pl.BlockSpec((d, da), const2),
        pl.BlockSpec((1, da), const2), pl.BlockSpec((1, da), const2),
        pl.BlockSpec(ws.shape, lambda i: (0, 0, 0)),
        pl.BlockSpec((lc, groups), const2),
        pl.BlockSpec((da, d), const2),
    ]
    out_specs = [pl.BlockSpec((tm, d), lambda i: (i, 0))]
    out_shape = [jax.ShapeDtypeStruct((n, d), F32)]
    if emit_v:
        out_specs.append(pl.BlockSpec((tm, da), lambda i: (i, 0)))
        out_shape.append(jax.ShapeDtypeStruct((n, da), F32))
    return pl.pallas_call(
        functools.partial(_gmlp_kernel, lc=lc, period=period, groups=groups, emit_v=emit_v),
        grid=(n // tm,),
        in_specs=in_specs, out_specs=out_specs, out_shape=out_shape,
        scratch_shapes=[pltpu.VMEM((tm, da), BF16), pltpu.VMEM((tm, da), BF16)],
        compiler_params=_cparams("arbitrary"),
        name="gmlp",
    )(x, g, shift, scale, gate, wu, wv, vg, vb, ws, bs_t, wo)


def _ffn_kernel(x_ref, g_ref, sh_ref, sc_ref, gt_ref, wg_ref, wu_ref, wd_ref, o_ref, h_scr, acc_scr):
    j = pl.program_id(1)

    @pl.when(j == 0)
    def _():
        h_scr[...] = _norm_mod(x_ref[...], g_ref[...], sh_ref[...], sc_ref[...]).astype(BF16)
        acc_scr[...] = jnp.zeros_like(acc_scr)

    h = h_scr[...]
    a = _silu(_dot(h, wg_ref[...].astype(BF16))) * _dot(h, wu_ref[...].astype(BF16))
    acc_scr[...] += _dot(a.astype(BF16), wd_ref[...].astype(BF16))

    @pl.when(j == pl.num_programs(1) - 1)
    def _():
        o_ref[...] = x_ref[...] + gt_ref[...] * acc_scr[...]


def _ffn_call(x, g, shift, scale, gate, w_gu, w_down, *, tm, tiles_per_seq):
    n, d = x.shape
    f = w_down.shape[0]
    tf = min(TF_FFN, f)
    nf = f // tf
    const2 = lambda i, j: (0, 0)
    return pl.pallas_call(
        _ffn_kernel,
        grid=(n // tm, nf),
        in_specs=[
            pl.BlockSpec((tm, d), lambda i, j: (i, 0)),
            pl.BlockSpec((1, d), const2),
            _mod_spec(shift, tiles_per_seq), _mod_spec(scale, tiles_per_seq), _mod_spec(gate, tiles_per_seq),
            pl.BlockSpec((d, tf), lambda i, j: (0, j)),
            pl.BlockSpec((d, tf), lambda i, j: (0, nf + j)),
            pl.BlockSpec((tf, d), lambda i, j: (j, 0)),
        ],
        out_specs=pl.BlockSpec((tm, d), lambda i, j: (i, 0)),
        out_shape=jax.ShapeDtypeStruct((n, d), F32),
        scratch_shapes=[pltpu.VMEM((tm, d), BF16), pltpu.VMEM((tm, d), F32)],
        compiler_params=_cparams("arbitrary", "arbitrary"),
        name="ffn",
    )(x, g, shift, scale, gate, w_gu, w_gu, w_down)


def _log_sigmoid(x):
    return jnp.minimum(x, 0.0) - jnp.log(1.0 + jnp.exp(-jnp.abs(x)))


def _foxproj_kernel(x_ref, g_ref, sh_ref, sc_ref, w_ref, wf_ref, bf_ref, *rest, transposed, period,
                    tiles_per_seq, q_scale):
    if transposed:
        q_ref, k_ref, kb_ref, v_ref, vb_ref, lf_ref, cum_ref, carry_scr = rest
    else:
        q_ref, k_ref, v_ref, lf_ref, cum_ref, carry_scr = rest
    i = pl.program_id(0)
    x = x_ref[...]
    tm, d = x.shape
    h = _norm_mod(x, g_ref[...], sh_ref[...], sc_ref[...]).astype(BF16)
    if transposed:
        tq = q_ref.shape[-1]
        qt = (_dot_nt(w_ref[0:d, :], h) * q_scale).astype(BF16)
        for jq in range(tm // tq):
            q_ref[jq] = qt[:, jq * tq:(jq + 1) * tq]
        k_ref[...] = _dot_nt(w_ref[d:2 * d, :], h)
        kb_ref[...] = _dot_nt(h, w_ref[d:2 * d, :]).astype(BF16)
        vt = _dot_nt(w_ref[2 * d:3 * d, :], h)
        v_ref[...] = vt
        vb_ref[...] = vt.astype(BF16)
    else:
        q_ref[...] = _dot_nt(h, w_ref[0:d, :]) * q_scale
        k_ref[...] = _dot_nt(h, w_ref[d:2 * d, :])
        v_ref[...] = _dot_nt(h, w_ref[2 * d:3 * d, :])
    logf = _log_sigmoid(_dot_nt(wf_ref[...], h) + bf_ref[...])
    lf_ref[...] = logf
    s = lax.broadcasted_iota(jnp.int32, (tm, tm), 0)
    t = lax.broadcasted_iota(jnp.int32, (tm, tm), 1)
    upper = s <= t
    if period < tm:
        blk = ~(period - 1)
        upper = upper & ((s & blk) == (t & blk))
    upper = _ones_where(upper)
    hi, mid, lo = _split3(logf)
    cum = _dot(hi, upper) + _dot(mid, upper) + _dot(lo, upper)
    if tiles_per_seq > 1:
        @pl.when(i % tiles_per_seq == 0)
        def _():
            carry_scr[...] = jnp.zeros_like(carry_scr)
        cum = cum + carry_scr[:, 0:1]
        carry_scr[...] = jnp.broadcast_to(cum[:, tm - 1:tm], carry_scr.shape)
    cum_ref[...] = cum


def _foxproj_call(x, g, shift, scale, w_t, wf_t, b_f, *, tm, tiles_per_seq, transposed, period):
    n, d = x.shape
    heads = wf_t.shape[0]
    n_seq = n // (tm * tiles_per_seq)
    t_len = tm * tiles_per_seq
    const2 = lambda i: (0, 0)
    row = lambda i: (i, 0)
    seq_t = lambda i: (i // tiles_per_seq, 0, i % tiles_per_seq)
    in_specs = [
        pl.BlockSpec((tm, d), row),
        pl.BlockSpec((1, d), const2),
        _mod_spec(shift, tiles_per_seq), _mod_spec(scale, tiles_per_seq),
        pl.BlockSpec((3 * d, d), const2),
        pl.BlockSpec((heads, d), const2),
        pl.BlockSpec((heads, 1), const2),
    ]
    lf_spec = pl.BlockSpec((None, heads, tm), seq_t)
    lf_shape = jax.ShapeDtypeStruct((n_seq, heads, t_len), F32)
    if transposed:
        tq = min(TQ_ATTN, tm)
        kv_spec = pl.BlockSpec((None, d, tm), seq_t)
        q_spec = pl.BlockSpec((None, tm // tq, d, tq), lambda i: (i // tiles_per_seq, i % tiles_per_seq, 0, 0))
        out_specs = [q_spec, kv_spec, pl.BlockSpec((tm, d), row), kv_spec, kv_spec, lf_spec, lf_spec]
        out_shape = [jax.ShapeDtypeStruct((n_seq, t_len // tq, d, tq), BF16),
                     jax.ShapeDtypeStruct((n_seq, d, t_len), F32), jax.ShapeDtypeStruct((n, d), BF16),
                     jax.ShapeDtypeStruct((n_seq, d, t_len), F32), jax.ShapeDtypeStruct((n_seq, d, t_len), BF16),
                     lf_shape, lf_shape]
    else:
        out_specs = [pl.BlockSpec((tm, d), row)] * 3 + [lf_spec, lf_spec]
        out_shape = [jax.ShapeDtypeStruct((n, d), F32)] * 3 + [lf_shape, lf_shape]
    return pl.pallas_call(
        functools.partial(_foxproj_kernel, transposed=transposed, period=period, tiles_per_seq=tiles_per_seq,
                          q_scale=float(d // heads) ** -0.5 * (LOG2E if transposed else 1.0)),
        grid=(n // tm,),
        in_specs=in_specs, out_specs=out_specs, out_shape=out_shape,
        scratch_shapes=[pltpu.VMEM((heads, LANES), F32)],
        compiler_params=_cparams("arbitrary"),
        name="foxproj",
    )(x, g, shift, scale, w_t, wf_t, b_f)


def _attn_kernel(qt_ref, k_ref, vt_ref, cum_ref, o_ref, ck_scr, m_scr, l_scr, acc_scr, *, hd):
    nq, _, tq = qt_ref.shape
    lane = lax.broadcasted_iota(jnp.int32, (tq, LANES), 1)
    key = lax.broadcasted_iota(jnp.int32, (tq, tq), 0)
    qry = lax.broadcasted_iota(jnp.int32, (tq, tq), 1)
    eye = key == qry
    causal = key <= qry
    m_scr[...] = jnp.full_like(m_scr, NEG)
    l_scr[...] = jnp.zeros_like(l_scr)
    acc_scr[...] = jnp.zeros_like(acc_scr)

    def update(hh, qi, kh, vh, masked):
        t = _dot(kh, qt_ref[qi]) - ck_scr[...]
        if masked:
            t = jnp.where(causal, t, NEG)
        cq = cum_ref[hh, qi:qi + 1, :] * LOG2E
        m_old = m_scr[hh, qi]
        m_new = jnp.maximum(m_old, jnp.max(t, axis=0, keepdims=True) + cq)
        alpha = jnp.exp2(m_old - m_new)
        p = jnp.exp2(t + (cq - m_new))
        l_scr[hh, qi] = alpha * l_scr[hh, qi] + jnp.sum(p, axis=0, keepdims=True)
        acc_scr[hh, qi] = alpha * acc_scr[hh, qi] + _dot(vh, p.astype(BF16))
        m_scr[hh, qi] = m_new

    for ki in range(nq):
        keys = slice(ki * tq, (ki + 1) * tq)
        k_blk = k_ref[keys, :]
        for hh in range(2):
            kh = jnp.where((lane >= hh * hd) & (lane < (hh + 1) * hd), k_blk, jnp.zeros_like(k_blk))
            vh = vt_ref[hh * hd:(hh + 1) * hd, keys]
            col = jnp.sum(jnp.where(eye, cum_ref[hh, ki:ki + 1, :], 0.0), axis=1, keepdims=True)
            ck_scr[...] = jnp.broadcast_to(col * LOG2E, ck_scr.shape)
            update(hh, ki, kh, vh, True)
            for qi in range(ki + 1, nq):
                update(hh, qi, kh, vh, False)

    for qi in range(nq):
        ot = jnp.concatenate([acc_scr[0, qi] / l_scr[0, qi], acc_scr[1, qi] / l_scr[1, qi]], axis=0)
        o_ref[qi * tq:(qi + 1) * tq, :] = ot.T.astype(o_ref.dtype)


def _attn_call(qt, k, vt, cum5, *, t_len, heads):
    n_seq, nq, d, tq = qt.shape
    hd = d // heads
    assert 2 * hd == LANES
    return pl.pallas_call(
        functools.partial(_attn_kernel, hd=hd),
        grid=(n_seq, heads // 2),
        in_specs=[
            pl.BlockSpec((None, nq, LANES, tq), lambda b, p: (b, 0, p, 0)),
            pl.BlockSpec((t_len, LANES), lambda b, p: (b, p)),
            pl.BlockSpec((None, LANES, t_len), lambda b, p: (b, p, 0)),
            pl.BlockSpec((None, None, 2, nq, tq), lambda b, p: (b, p, 0, 0, 0)),
        ],
        out_specs=pl.BlockSpec((t_len, LANES), lambda b, p: (b, p)),
        out_shape=jax.ShapeDtypeStruct((n_seq * t_len, d), BF16),
        scratch_shapes=[pltpu.VMEM((tq, tq), F32), pltpu.VMEM((2, nq, 1, tq), F32),
                        pltpu.VMEM((2, nq, 1, tq), F32), pltpu.VMEM((2, nq, hd, tq), F32)],
        compiler_params=_cparams("arbitrary", "arbitrary"),
        name="attn_prompt",
    )(qt, k, vt, cum5)


def _attn_sample_body(j, n_j, co_work, q_ref, cn_ref, kn_ref, vn_ref, *rest, pages, hd, page_size):
    k_refs = rest[0:pages]
    v_refs = rest[pages:2 * pages]
    lf_refs = rest[2 * pages:3 * pages]
    o_ref, qbd_scr, cn_scr, m_scr, l_scr, acc_scr, suf_scr = rest[3 * pages:]
    tn, d = q_ref.shape
    heads = d // hd
    rows = heads * tn
    row = lax.broadcasted_iota(jnp.int32, (rows, LANES), 0)
    lane = lax.broadcasted_iota(jnp.int32, (rows, LANES), 1)
    q_of_row = row & (tn - 1)

    def rep(a):
        return jnp.broadcast_to(a[:, None, :], (heads, tn, a.shape[-1])).reshape(rows, a.shape[-1])

    gw = qbd_scr.shape[1]
    gr = (gw // hd) * tn
    n_groups = d // gw
    g_rows = lambda g: slice(g * gr, (g + 1) * gr)
    g_cols = lambda g: slice(g * gw, (g + 1) * gw)
    rr = lax.broadcasted_iota(jnp.int32, (gr, gw), 0)
    cc = lax.broadcasted_iota(jnp.int32, (gr, gw), 1)
    own_head = _shift_div(rr, tn) == _shift_div(cc, hd)

    @pl.when(j == 0)
    def _():
        q = q_ref[...]
        for g in range(n_groups):
            qrep = jnp.broadcast_to(q[None, :, g_cols(g)], (gw // hd, tn, gw)).reshape(gr, gw)
            qbd_scr[g_rows(g), :] = jnp.where(own_head, qrep, 0.0).astype(BF16)
        cn_scr[...] = jnp.sum(jnp.where(lane == q_of_row, rep(cn_ref[...]), 0.0), axis=1, keepdims=True)
        m_scr[...] = jnp.full_like(m_scr, NEG)
        l_scr[...] = jnp.zeros_like(l_scr)
        acc_scr[...] = jnp.zeros_like(acc_scr)
        suf_scr[...] = jnp.zeros_like(suf_scr)

    co_work()

    s_idx = lax.broadcasted_iota(jnp.int32, (page_size, page_size), 0)
    t_idx = lax.broadcasted_iota(jnp.int32, (page_size, page_size), 1)
    later = _ones_where(s_idx > t_idx)
    lf_all = jnp.concatenate([lf_refs[i][...] for i in range(pages)], axis=0)
    hi, mid, lo = _split3(lf_all)
    suf_all = _dot(hi, later) + _dot(mid, later) + _dot(lo, later)
    tot_all = suf_all[:, 0:1] + lf_all[:, 0:1]
    carry = suf_scr[:, 0:1]
    scores = [None] * pages
    for i in reversed(range(pages)):
        suf = suf_all[i * heads:(i + 1) * heads, :] + carry
        carry = carry + tot_all[i * heads:(i + 1) * heads, :]
        qk = [_dot(qbd_scr[g_rows(g), :], k_refs[i][g_cols(g), :].astype(BF16)) for g in range(n_groups)]
        scores[i] = jnp.concatenate(qk, axis=0) + rep(suf)
    suf_scr[...] = jnp.broadcast_to(carry, suf_scr.shape)
    s = jnp.concatenate(scores, axis=1) + cn_scr[...]
    m_old = m_scr[...]
    m_new = jnp.maximum(m_old, jnp.max(s, axis=1, keepdims=True))
    alpha = jnp.exp(m_old - m_new)
    p = jnp.exp(s - m_new)
    l_scr[...] = alpha * l_scr[...] + jnp.sum(p, axis=1, keepdims=True)
    pb = p.astype(BF16)
    for g in range(n_groups):
        pv = _dot_nt(pb[g_rows(g), 0:page_size], v_refs[0][g_cols(g), :].astype(BF16))
        for i in range(1, pages):
            pv = pv + _dot_nt(pb[g_rows(g), i * page_size:(i + 1) * page_size],
                              v_refs[i][g_cols(g), :].astype(BF16))
        acc_scr[g_rows(g), :] = alpha[g_rows(g), :] * acc_scr[g_rows(g), :] + pv
    m_scr[...] = m_new

    @pl.when(j == n_j - 1)
    def _():
        pad = jnp.zeros((LANES - tn, d), BF16)
        k_new = jnp.concatenate([kn_ref[...].astype(BF16), pad], axis=0)
        v_new = jnp.concatenate([vn_ref[...].astype(BF16), pad], axis=0)
        qk = [_dot_nt(qbd_scr[g_rows(g), :], k_new[:, g_cols(g)]) for g in range(n_groups)]
        s = jnp.concatenate(qk, axis=0) + cn_scr[...] - rep(cn_ref[...])
        s = jnp.where(lane <= q_of_row, s, NEG)
        m_old = m_scr[...]
        m_new = jnp.maximum(m_old, jnp.max(s, axis=1, keepdims=True))
        alpha = jnp.exp(m_old - m_new)
        pb = jnp.exp(s - m_new)
        l_new = alpha * l_scr[...] + jnp.sum(pb, axis=1, keepdims=True)
        pb = pb.astype(BF16)
        for g in range(n_groups):
            o = alpha[g_rows(g), :] * acc_scr[g_rows(g), :] + _dot(pb[g_rows(g), :], v_new[:, g_cols(g)])
            o = jnp.where(own_head, o / l_new[g_rows(g), :], 0.0)
            o_ref[:, g_cols(g)] = jnp.sum(o.reshape(gw // hd, tn, gw), axis=0)


N_GMLP_IN = 12


def _gmlp_attn_kernel(pt_ref, *refs, n_steps, pages, hd, page_size, lc, groups):
    n_attn_in = 4 + 3 * pages
    gmlp_in = refs[:N_GMLP_IN]
    attn_in = refs[N_GMLP_IN:N_GMLP_IN + n_attn_in]
    x1_ref, o_ref, vn_scr, out_scr, *attn_scr = refs[N_GMLP_IN + n_attn_in:]
    i = pl.program_id(0)
    mixer = functools.partial(_gmlp_kernel, *gmlp_in, x1_ref, vn_scr, out_scr, lc=lc, period=lc, groups=groups,
                              emit_v=False)
    _attn_sample_body(i % n_steps, n_steps, mixer, *attn_in, o_ref, *attn_scr,
                      pages=pages, hd=hd, page_size=page_size)


def _gmlp_attn_call(x, g, shift, scale, gate, wu, wv, vg, vb, ws, bs_t, wo,
                    page_table, q, cn_pad, k_new, v_new, kc_t, vc_t, lfc_t, *, t_len, tn, heads):
    n, d = x.shape
    da = wu.shape[1]
    groups = ws.shape[0]
    lc = bs_t.shape[0]
    n_s = q.shape[0]
    n_seq, n_pages = page_table.shape
    page_size = kc_t.shape[-1]
    pages = min(PAGES_PER_STEP, n_pages)
    n_steps = n_pages // pages
    hd = d // heads
    n_grid = n_seq * n_steps
    tm = n // n_grid
    assert tm * n_grid == n and tm % lc == 0 and t_len % tm == 0, (n, n_grid, lc)
    tiles_per_seq = t_len // tm
    once = pl.Buffered(1)
    const2 = lambda i, pt: (0, 0)
    in_specs = [
        pl.BlockSpec((tm, d), lambda i, pt: (i, 0)),
        pl.BlockSpec((1, d), const2),
        _mod_spec(shift, tiles_per_seq), _mod_spec(scale, tiles_per_seq), _mod_spec(gate, tiles_per_seq),
        pl.BlockSpec((d, da), const2, pipeline_mode=once), pl.BlockSpec((d, da), const2, pipeline_mode=once),
        pl.BlockSpec((1, da), const2), pl.BlockSpec((1, da), const2),
        pl.BlockSpec(ws.shape, lambda i, pt: (0, 0, 0)),
        pl.BlockSpec((lc, groups), const2),
        pl.BlockSpec((da, d), const2, pipeline_mode=once),
    ]
    assert len(in_specs) == N_GMLP_IN

    def page_map(ip):
        def index(i, pt):
            return (pt[(i // n_steps) * n_pages + (n_steps - 1 - i % n_steps) * pages + ip], 0, 0)
        return index

    seq_row = lambda i, pt: (i // n_steps, 0)
    in_specs += [
        pl.BlockSpec((tn, d), seq_row),
        pl.BlockSpec((None, heads, LANES), lambda i, pt: (i // n_steps, 0, 0)),
        pl.BlockSpec((tn, d), seq_row), pl.BlockSpec((tn, d), seq_row),
    ]
    in_specs += [pl.BlockSpec((None, d, page_size), page_map(ip)) for ip in range(pages)]
    in_specs += [pl.BlockSpec((None, d, page_size), page_map(ip)) for ip in range(pages)]
    in_specs += [pl.BlockSpec((None, heads, page_size), page_map(ip)) for ip in range(pages)]
    rows = heads * tn
    return pl.pallas_call(
        functools.partial(_gmlp_attn_kernel, n_steps=n_steps, pages=pages, hd=hd, page_size=page_size,
                          lc=lc, groups=groups),
        grid_spec=pltpu.PrefetchScalarGridSpec(
            num_scalar_prefetch=1,
            grid=(n_grid,),
            in_specs=in_specs,
            out_specs=[pl.BlockSpec((tm, d), lambda i, pt: (i, 0)), pl.BlockSpec((tn, d), seq_row)],
            scratch_shapes=[pltpu.VMEM((tm, da), BF16), pltpu.VMEM((tm, da), BF16),
                            pltpu.VMEM((rows, MXU_TILE), BF16), pltpu.VMEM((rows, 1), F32),
                            pltpu.VMEM((rows, 1), F32), pltpu.VMEM((rows, 1), F32),
                            pltpu.VMEM((rows, MXU_TILE), F32), pltpu.VMEM((heads, LANES), F32)],
        ),
        out_shape=[jax.ShapeDtypeStruct((n, d), F32), jax.ShapeDtypeStruct((n_s, d), F32)],
        compiler_params=_cparams("arbitrary"),
        name="gmlp_attn_sample",
    )(page_table.reshape(-1), x, g, shift, scale, gate, wu, wv, vg, vb, ws, bs_t, wo,
      q, cn_pad, k_new, v_new, *([kc_t] * pages), *([vc_t] * pages), *([lfc_t] * pages))


def _post_kernel(op_ref, os_ref, wo_ref, xp_ref, xs_ref, gtp_ref, gts_ref, g_ref, shp_ref, shs_ref, scp_ref,
                 scs_ref, wr_ref, br_ref, x3_ref, h_ref, meta_ref, metat_ref, blk_ref, cout_ref, carry_scr,
                 *, n_experts, n_prompt_blocks):
    i = pl.program_id(0)
    tm = xp_ref.shape[0]
    is_p = i < n_prompt_blocks

    @pl.when(i == 0)
    def _():
        carry_scr[...] = jnp.zeros_like(carry_scr)

    o = jnp.where(is_p, op_ref[...], os_ref[...].astype(BF16))
    x = jnp.where(is_p, xp_ref[...], xs_ref[...])
    gate = jnp.where(is_p, gtp_ref[...], gts_ref[...])
    shift = jnp.where(is_p, shp_ref[...], shs_ref[...])
    scale = jnp.where(is_p, scp_ref[...], scs_ref[...])
    x3 = x + gate * _dot(o, wo_ref[...])
    x3_ref[...] = x3
    h = _norm_mod(x3, g_ref[...], shift, scale)
    h_hi = h.astype(BF16)
    h_ref[...] = h_hi
    h_lo = (h - h_hi.astype(F32)).astype(BF16)
    wr = wr_ref[...]
    w_hi = wr.astype(BF16)
    w_lo = (wr - w_hi.astype(F32)).astype(BF16)
    logits = _dot(h_hi, w_hi) + _dot(h_hi, w_lo) + _dot(h_lo, w_hi) + br_ref[...]
    lane_i = lax.broadcasted_iota(jnp.int32, (tm, LANES), 1)
    lane = lane_i.astype(F32)
    logits = jnp.where(lane_i < n_experts, logits, NEG)
    l1 = jnp.max(logits, axis=1, keepdims=True)
    i1 = jnp.min(jnp.where(logits == l1, lane, float(LANES)), axis=1, keepdims=True)
    rest = jnp.where(lane == i1, NEG, logits)
    l2 = jnp.max(rest, axis=1, keepdims=True)
    i2 = jnp.min(jnp.where(rest == l2, lane, float(LANES)), axis=1, keepdims=True)
    e = jnp.exp(l2 - l1)
    g1 = 1.0 / (1.0 + e)
    g2 = e / (1.0 + e)
    onehot = jnp.where((lane == i1) | (lane == i2), 1.0, 0.0)
    r = lax.broadcasted_iota(jnp.int32, (tm, tm), 0)
    c = lax.broadcasted_iota(jnp.int32, (tm, tm), 1)
    before = _ones_where(c < r)
    carry = carry_scr[0:1, :]
    blk_ref[...] = carry
    prefix = _dot(before, onehot.astype(BF16)) + carry
    r1 = jnp.sum(jnp.where(lane == i1, prefix, 0.0), axis=1, keepdims=True)
    r2 = jnp.sum(jnp.where(lane == i2, prefix, 0.0), axis=1, keepdims=True)
    cols = (i1, i2, g1, g2, r1, r2)
    meta = jnp.zeros((tm, LANES), F32)
    for k, col in enumerate(cols):
        meta = jnp.where(lane_i == k, col, meta)
    meta_ref[...] = meta
    metat_ref[...] = meta.T[0:SUBLANES, :]
    carry = carry + jnp.sum(onehot, axis=0, keepdims=True)
    carry_scr[...] = jnp.broadcast_to(carry, carry_scr.shape)
    cout_ref[...] = carry


def _post_call(o_p, o_s, wo, x_p, x_s, gate_p, gate_s, g, shift_p, shift_s, scale_p, scale_s, wr_pad, br_pad,
               *, tm, blocks_per_seq, n_experts):
    n_p, d = x_p.shape
    n_s = x_s.shape[0]
    npb = n_p // tm
    nb = npb + n_s // tm
    n = n_p + n_s
    const2 = lambda i: (0, 0)
    row = lambda i: (i, 0)
    p_row = lambda i: (jnp.minimum(i, npb - 1), 0)
    s_row = lambda i: (jnp.maximum(i - npb, 0), 0)
    p_mod = pl.BlockSpec((None, 1, d), lambda i: (jnp.minimum(i, npb - 1) // blocks_per_seq, 0, 0))
    s_mod = pl.BlockSpec((None, tm, d), lambda i: (jnp.maximum(i - npb, 0), 0, 0))
    return pl.pallas_call(
        functools.partial(_post_kernel, n_experts=n_experts, n_prompt_blocks=npb),
        grid=(nb,),
        in_specs=[
            pl.BlockSpec((tm, d), p_row), pl.BlockSpec((tm, d), s_row),
            pl.BlockSpec((d, d), const2),
            pl.BlockSpec((tm, d), p_row), pl.BlockSpec((tm, d), s_row),
            p_mod, s_mod,
            pl.BlockSpec((1, d), const2),
            p_mod, s_mod, p_mod, s_mod,
            pl.BlockSpec((d, LANES), const2),
            pl.BlockSpec((1, LANES), const2),
        ],
        out_specs=[
            pl.BlockSpec((tm, d), row), pl.BlockSpec((tm, d), row), pl.BlockSpec((tm, LANES), row),
            pl.BlockSpec((None, SUBLANES, tm), lambda i: (i, 0, 0)),
            pl.BlockSpec((None, 1, LANES), lambda i: (i, 0, 0)),
            pl.BlockSpec((1, LANES), const2),
        ],
        out_shape=[
            jax.ShapeDtypeStruct((n, d), F32), jax.ShapeDtypeStruct((n, d), BF16),
            jax.ShapeDtypeStruct((n, LANES), F32),
            jax.ShapeDtypeStruct((nb, SUBLANES, tm), F32),
            jax.ShapeDtypeStruct((nb, 1, LANES), F32),
            jax.ShapeDtypeStruct((1, LANES), F32),
        ],
        scratch_shapes=[pltpu.VMEM((SUBLANES, LANES), F32)],
        compiler_params=_cparams("arbitrary"),
        name="post_attn_router",
    )(o_p, o_s, wo, x_p, x_s, gate_p, gate_s, g, shift_p, shift_s, scale_p, scale_s, wr_pad, br_pad)


def _sorted_pos(idx, rank, start_ref, n_experts):
    pos = rank
    for e in range(n_experts):
        pos = pos + jnp.where(idx == float(e), start_ref[e].astype(F32), 0.0)
    return pos


def _dispatch_kernel(lo_ref, hi_ref, start_ref, mt_ref, h_ref, xs_ref, gs_ref, acc_scr, gacc_scr,
                     *, tb, n_experts):
    j = pl.program_id(0)
    sub = xs_ref.shape[0]
    nb = mt_ref.shape[0]
    lo = lo_ref[j]
    hi = hi_ref[j]

    @pl.when(hi <= lo)
    def _():
        xs_ref[...] = jnp.zeros_like(xs_ref)
        gs_ref[...] = jnp.zeros_like(gs_ref)

    def hits(b, width):
        dest = (j * sub + lax.broadcasted_iota(jnp.int32, (sub, width * tb), 0)).astype(F32)
        mts = [mt_ref[b + w] for w in range(width)]
        field = lambda r: jnp.concatenate([m[r:r + 1, :] for m in mts], axis=1)
        hit0 = _sorted_pos(field(0), field(4), start_ref, n_experts) == dest
        hit1 = _sorted_pos(field(1), field(5), start_ref, n_experts) == dest
        gate = jnp.where(hit0, field(2), 0.0) + jnp.where(hit1, field(3), 0.0)
        return _ones_where(hit0 | hit1), gate

    n_even = nb - nb % 2

    def pair(m, carry):
        b = 2 * m
        oh, gate = hits(b, 2)
        rows = pl.ds(pl.multiple_of(b * tb, 2 * tb), 2 * tb)
        acc_scr[...] += _dot(oh, h_ref[rows, :])
        gacc_scr[...] += gate
        return carry

    @pl.when(hi > lo)
    def _():
        acc_scr[...] = jnp.zeros_like(acc_scr)
        gacc_scr[...] = jnp.zeros_like(gacc_scr)
        lax.fori_loop(lo // 2, (jnp.minimum(hi, n_even) + 1) // 2, pair, 0)

        if nb % 2:
            @pl.when(hi == nb)
            def _():
                oh, gate = hits(nb - 1, 1)
                acc_scr[...] += _dot(oh, h_ref[(nb - 1) * tb:nb * tb, :])
                gacc_scr[:, 0:tb] += gate

        xs_ref[...] = acc_scr[...].astype(BF16)
        gs_ref[...] = jnp.broadcast_to(jnp.sum(gacc_scr[...], axis=1, keepdims=True), gs_ref.shape)


def _dispatch_call(tile_lo, tile_hi, start, meta_t, h, *, n_sorted, n_experts):
    n, d = h.shape
    nb, _, tb = meta_t.shape
    whole = lambda nd: (lambda j, lo, hi, st: (0,) * nd)
    once = pl.Buffered(1)
    return pl.pallas_call(
        functools.partial(_dispatch_kernel, tb=tb, n_experts=n_experts),
        grid_spec=pltpu.PrefetchScalarGridSpec(
            num_scalar_prefetch=3,
            grid=(n_sorted // SUB,),
            in_specs=[
                pl.BlockSpec((nb, SUBLANES, tb), whole(3), pipeline_mode=once),
                pl.BlockSpec((n, d), whole(2), pipeline_mode=once),
            ],
            out_specs=[pl.BlockSpec((SUB, d), lambda j, lo, hi, st: (j, 0)),
                       pl.BlockSpec((SUB, LANES), lambda j, lo, hi, st: (j, 0))],
            scratch_shapes=[pltpu.VMEM((SUB, d), F32), pltpu.VMEM((SUB, 2 * tb), F32)],
        ),
        out_shape=[jax.ShapeDtypeStruct((n_sorted, d), BF16), jax.ShapeDtypeStruct((n_sorted, LANES), F32)],
        compiler_params=_cparams("arbitrary"),
        name="moe_dispatch",
    )(tile_lo, tile_hi, start, meta_t, h)


def _moe_kernel(ex_ref, nsub_ref, x_ref, gs_ref, wg_ref, wu_ref, wd_ref, y_ref, acc_scr):
    s = pl.program_id(0)
    j = pl.program_id(1)
    nsub = nsub_ref[s]
    last = j == pl.num_programs(1) - 1

    n_full = (nsub * SUB) // FFN_ROWS
    odd = nsub * SUB - n_full * FFN_ROWS > 0

    def tile_rows(t):
        return pl.ds(pl.multiple_of(t * SUB, SUB), SUB)

    def step_rows(t):
        return pl.ds(pl.multiple_of(t * FFN_ROWS, FFN_ROWS), FFN_ROWS)

    @pl.when(j == 0)
    def _():
        def zero(t, carry):
            acc_scr[tile_rows(t), :] = jnp.zeros((SUB, acc_scr.shape[1]), F32)
            return carry
        lax.fori_loop(0, nsub, zero, 0)

    def ffn(rows):
        x = x_ref[rows, :]
        a = _silu(_dot(x, wg_ref[...].astype(BF16))) * _dot(x, wu_ref[...].astype(BF16))
        acc_scr[rows, :] += _dot(a.astype(BF16), wd_ref[...].astype(BF16))

    def body(t, carry):
        ffn(step_rows(t))
        return carry

    lax.fori_loop(0, n_full, body, 0)

    @pl.when(odd)
    def _():
        ffn(tile_rows(nsub - 1))

    @pl.when(last)
    def _():
        def emit(t, carry):
            rows = tile_rows(t)
            y_ref[rows, :] = (acc_scr[rows, :] * gs_ref[rows, 0:1]).astype(BF16)
            return carry
        lax.fori_loop(0, nsub, emit, 0)

        def zero(t, carry):
            y_ref[tile_rows(t), :] = jnp.zeros((SUB, y_ref.shape[1]), BF16)
            return carry
        lax.fori_loop(nsub, y_ref.shape[0] // SUB, zero, 0)


def _moe_call(sup_expert, sup_nsub, xs, gs, w_gu, w_down, *, ts):
    n_sorted, d = xs.shape
    f = w_down.shape[1]
    tf = min(TF_FFN, f)
    nf = f // tf
    n_super = n_sorted // ts

    def jj(s, j, nsub):
        return jnp.where(nsub[s] > 0, j, nf - 1)

    return pl.pallas_call(
        _moe_kernel,
        grid_spec=pltpu.PrefetchScalarGridSpec(
            num_scalar_prefetch=2,
            grid=(n_super, nf),
            in_specs=[
                pl.BlockSpec((ts, d), lambda s, j, ex, ns: (s, 0)),
                pl.BlockSpec((ts, LANES), lambda s, j, ex, ns: (s, 0)),
                pl.BlockSpec((None, d, tf), lambda s, j, ex, ns: (ex[s], 0, jj(s, j, ns))),
                pl.BlockSpec((None, d, tf), lambda s, j, ex, ns: (ex[s], 0, nf + jj(s, j, ns))),
                pl.BlockSpec((None, tf, d), lambda s, j, ex, ns: (ex[s], jj(s, j, ns), 0)),
            ],
            out_specs=pl.BlockSpec((ts, d), lambda s, j, ex, ns: (s, 0)),
            scratch_shapes=[pltpu.VMEM((ts, d), F32)],
        ),
        out_shape=jax.ShapeDtypeStruct((n_sorted, d), BF16),
        compiler_params=_cparams("arbitrary", "arbitrary"),
        name="moe_ffn",
    )(sup_expert, sup_nsub, xs, gs, w_gu, w_gu, w_down)


def _combine_kernel(tile_ref, need2_ref, start_ref, meta_ref, x_ref, gtp_ref, gts_ref, gf_ref, *rest,
                    n_experts, n_prompt_blocks):
    y_refs = rest[0:2 * n_experts]
    op_ref, os_ref, acc_scr = rest[2 * n_experts:]
    b = pl.program_id(0)
    tb = acc_scr.shape[0]
    meta = meta_ref[...]
    col = lax.broadcasted_iota(jnp.int32, (tb, SUB), 1)

    def window(e, w):
        first = start_ref[e].astype(F32)
        p0 = jnp.where(meta[:, 0:1] == float(e), meta[:, 4:5] + first, -1.0)
        p1 = jnp.where(meta[:, 1:2] == float(e), meta[:, 5:6] + first, -1.0)
        src = ((tile_ref[b * n_experts + e] + w) * SUB + col).astype(F32)
        return _ones_where((p0 == src) | (p1 == src))

    total = _dot(window(0, 0), y_refs[0][...])
    for e in range(1, n_experts):
        total = total + _dot(window(e, 0), y_refs[2 * e][...])
    acc_scr[...] = total
    for e in range(n_experts):
        @pl.when(need2_ref[b * n_experts + e] > 0)
        def _(e=e):
            acc_scr[...] += _dot(window(e, 1), y_refs[2 * e + 1][...])

    is_p = b < n_prompt_blocks
    x4 = x_ref[...] + jnp.where(is_p, gtp_ref[...], gts_ref[...]) * acc_scr[...]
    ms = jnp.mean(x4 * x4, axis=-1, keepdims=True)
    y = x4 * lax.rsqrt(ms + EPS) * gf_ref[...]

    @pl.when(is_p)
    def _():
        op_ref[...] = y

    @pl.when(jnp.logical_not(is_p))
    def _():
        os_ref[...] = y


def _combine_call(win_tile, win_need2, start, meta, x, gate_p, gate_s, g_final, ys, *, n_p, tb, blocks_per_seq,
                  n_experts):
    n, d = x.shape
    n_s = n - n_p
    npb = n_p // tb
    nb = n // tb

    def pb(b):
        return jnp.minimum(b, npb - 1)

    def sb(b):
        return jnp.maximum(b - npb, 0)

    def win_map(e, w):
        if w == 0:
            return lambda b, tile, need, st: (tile[b * n_experts + e], 0)
        return lambda b, tile, need, st: (jnp.where(need[b * n_experts + e] > 0, tile[b * n_experts + e] + 1, 0), 0)

    in_specs = [
        pl.BlockSpec((tb, LANES), lambda b, tile, need, st: (b, 0)),
        pl.BlockSpec((tb, d), lambda b, tile, need, st: (b, 0)),
        pl.BlockSpec((None, 1, d), lambda b, tile, need, st: (pb(b) // blocks_per_seq, 0, 0)),
        pl.BlockSpec((None, tb, d), lambda b, tile, need, st: (sb(b), 0, 0)),
        pl.BlockSpec((1, d), lambda b, tile, need, st: (0, 0)),
    ]
    in_specs += [pl.BlockSpec((SUB, d), win_map(e, w)) for e in range(n_experts) for w in range(2)]
    return pl.pallas_call(
        functools.partial(_combine_kernel, n_experts=n_experts, n_prompt_blocks=npb),
        grid_spec=pltpu.PrefetchScalarGridSpec(
            num_scalar_prefetch=3,
            grid=(nb,),
            in_specs=in_specs,
            out_specs=[pl.BlockSpec((tb, d), lambda b, tile, need, st: (pb(b), 0)),
                       pl.BlockSpec((tb, d), lambda b, tile, need, st: (sb(b), 0))],
            scratch_shapes=[pltpu.VMEM((tb, d), F32)],
        ),
        out_shape=[jax.ShapeDtypeStruct((n_p, d), F32), jax.ShapeDtypeStruct((n_s, d), F32)],
        compiler_params=_cparams("arbitrary"),
        name="moe_combine",
    )(win_tile, win_need2, start, meta, x, gate_p, gate_s, g_final, *([ys] * (2 * n_experts)))


def _routing_tables(blk_cnt, counts, *, n_experts, ts, n_super):
    nb = blk_cnt.shape[0]
    cnt = counts[0, :n_experts].astype(jnp.int32)
    n_sup_e = (cnt + ts - 1) // ts
    sup_start = jnp.cumsum(n_sup_e) - n_sup_e
    start = sup_start * ts

    s_ids = jnp.arange(n_super, dtype=jnp.int32)
    used = jnp.sum(n_sup_e)
    sup_end = sup_start + n_sup_e
    sup_e = jnp.sum((s_ids[:, None] >= sup_end[None, :]).astype(jnp.int32), axis=1)
    sup_e = jnp.clip(sup_e, 0, n_experts - 1)
    last_e = jnp.max(jnp.where(cnt > 0, jnp.arange(n_experts, dtype=jnp.int32), 0))
    sup_e = jnp.where(s_ids < used, sup_e, last_e)
    rows_in = jnp.clip(cnt[sup_e] - (s_ids - sup_start[sup_e]) * ts, 0, ts)
    sup_nsub = jnp.where(s_ids < used, (rows_in + SUB - 1) // SUB, 0).astype(jnp.int32)

    blk = blk_cnt[:, 0, :n_experts].astype(jnp.int32)
    blk_end = jnp.concatenate([blk[1:], cnt[None, :]], axis=0)
    n_tiles = n_super * (ts // SUB)
    t_ids = jnp.arange(n_tiles, dtype=jnp.int32)
    t_e = sup_e[t_ids // (ts // SUB)]
    t_r0 = t_ids * SUB - start[t_e]
    t_active = ((t_ids // (ts // SUB)) < used) & (t_r0 < cnt[t_e]) & (t_r0 >= 0)
    t_r1 = jnp.minimum(t_r0 + SUB, cnt[t_e])
    be = blk_end[:, t_e]
    bs = blk[:, t_e]
    overlap = (be > t_r0[None, :]) & (bs < t_r1[None, :])
    b_ids = jnp.arange(nb, dtype=jnp.int32)[:, None]
    lo = jnp.min(jnp.where(overlap, b_ids, nb), axis=0)
    hi = jnp.max(jnp.where(overlap, b_ids + 1, 0), axis=0)
    tile_lo = jnp.where(t_active, lo, 0).astype(jnp.int32)
    tile_hi = jnp.where(t_active, hi, 0).astype(jnp.int32)

    first = start[None, :] + blk
    n_be = blk_end - blk
    win_tile = (first // SUB).astype(jnp.int32)
    win_need2 = ((first + n_be) > (win_tile + 1) * SUB).astype(jnp.int32)

    return dict(start=start.astype(jnp.int32), sup_e=sup_e, sup_nsub=sup_nsub, tile_lo=tile_lo,
                tile_hi=tile_hi, win_tile=win_tile.reshape(-1), win_need2=win_need2.reshape(-1))


def kernel(x_prompt, x_sample, c_prompt, c_sample, cache_k, cache_v, cache_logf, page_table, norm_mix_g, norm_ffn_g, final_norm_g, ada_w, ada_b, gmlp_w_in, gmlp_v_g, gmlp_v_b, gmlp_w_s, gmlp_b_s, gmlp_w_out, fox_w_in, fox_b_f, fox_w_out, ffn_w_gu, ffn_w_down, moe_w_r, moe_b_r, moe_w_gu, moe_w_down):
    n_seq_p, t_len, d = x_prompt.shape
    n_seq_s, t_new, _ = x_sample.shape
    n_p = n_seq_p * t_len
    n_s = n_seq_s * t_new
    heads = N_HEADS
    hd = d // heads
    n_experts = moe_w_r.shape[-1]
    da = gmlp_w_out.shape[1]

    xp = x_prompt.reshape(n_p, d)
    xs = x_sample.reshape(n_s, d)

    mod = _ada_call(jnp.concatenate([c_prompt, c_sample], axis=0), ada_w, ada_b)

    def mods(layer):
        mp = [mod[layer, :n_seq_p, c * d:(c + 1) * d].reshape(n_seq_p, 1, d) for c in range(6)]
        ms = [jnp.repeat(mod[layer, n_seq_p:, c * d:(c + 1) * d], t_new, axis=0).reshape(1, n_s, d)
              for c in range(6)]
        return mp, ms

    row = lambda a: a.reshape(1, -1)

    mp, ms = mods(0)
    w_in = gmlp_w_in[0].astype(BF16)
    wu, wv = w_in[:, :da], w_in[:, da:]
    wo = gmlp_w_out[0].astype(BF16)
    vg, vb = row(gmlp_v_g[0]), row(gmlp_v_b[0])
    lc = min(GMLP_CHUNK, t_len)
    mp1, ms1 = mods(1)
    w_t = fox_w_in[0].T.astype(BF16)
    wqkv_t, wf_t = w_t[:3 * d], w_t[3 * d:]
    b_f = fox_b_f[0].reshape(heads, 1)

    reps = n_s // t_new
    ws_s = jnp.zeros((gmlp_w_s.shape[1], LANES, LANES), F32).at[:, :t_new, :t_new].set(
        gmlp_w_s[0][:, :t_new, :t_new])
    bs_s = jnp.tile(gmlp_b_s[0][:, :t_new], (1, reps)).T
    xs, v_rows = _gmlp_call(xs, row(norm_mix_g[0]), ms[0], ms[1], ms[2], wu, wv, vg, vb, ws_s, bs_s, wo,
                            tm=n_s, tiles_per_seq=1, period=t_new, emit_v=True)
    xs = _ffn_call(xs, row(norm_ffn_g[0]), ms[3], ms[4], ms[5], ffn_w_gu[0], ffn_w_down[0],
                   tm=n_s, tiles_per_seq=1)
    q_s, k_s, v_s, lft_s, cumt_s = _foxproj_call(
        xs, row(norm_mix_g[1]), ms1[0], ms1[1], wqkv_t, wf_t, b_f,
        tm=n_s, tiles_per_seq=1, transposed=False, period=t_new)

    page_size = cache_k.shape[2]
    n_phys = cache_k.shape[1]
    kc_t = jnp.transpose(cache_k[0], (0, 2, 3, 1)).reshape(n_phys, d, page_size)
    vc_t = jnp.transpose(cache_v[0], (0, 2, 3, 1)).reshape(n_phys, d, page_size)
    lfc_t = jnp.transpose(cache_logf[0], (0, 2, 1))
    cn = cumt_s[0].reshape(heads, n_seq_s, t_new).transpose(1, 0, 2)
    cn_pad = jnp.zeros((n_seq_s, heads, LANES), F32).at[:, :, :t_new].set(cn)
    xp, o_s = _gmlp_attn_call(xp, row(norm_mix_g[0]), mp[0], mp[1], mp[2], wu, wv, vg, vb,
                              gmlp_w_s[0][:, :lc, :lc], gmlp_b_s[0][:, :lc].T, wo,
                              page_table, q_s, cn_pad, k_s, v_s, kc_t, vc_t, lfc_t,
                              t_len=t_len, tn=t_new, heads=heads)
    tm = min(TM_FFN, t_len)
    xp = _ffn_call(xp, row(norm_ffn_g[0]), mp[3], mp[4], mp[5], ffn_w_gu[0], ffn_w_down[0],
                   tm=tm, tiles_per_seq=t_len // tm)

    mp, ms = mp1, ms1
    tm = min(TM_PROJ, t_len)
    qt_p, kt_p, kb_p, vt_p, vtb_p, lft_p, cumt_p = _foxproj_call(
        xp, row(norm_mix_g[1]), mp[0], mp[1], wqkv_t, wf_t, b_f,
        tm=tm, tiles_per_seq=t_len // tm, transposed=True, period=tm)
    nq, tq = qt_p.shape[1], qt_p.shape[3]
    o_p = _attn_call(qt_p, kb_p, vtb_p, cumt_p.reshape(n_seq_p, heads // 2, 2, nq, tq), t_len=t_len, heads=heads)

    wo_f = fox_w_out[0].astype(BF16)
    wr_pad = jnp.zeros((d, LANES), F32).at[:, :n_experts].set(moe_w_r[0])
    br_pad = jnp.zeros((1, LANES), F32).at[0, :n_experts].set(moe_b_r[0])
    tb = min(TM_POST, t_len)
    ms_blk = [m.reshape(n_s // tb, tb, d) for m in ms]
    x3, h_all, meta, meta_t, blk_cnt, counts = _post_call(
        o_p, o_s, wo_f, xp, xs, mp[2], ms_blk[2], row(norm_ffn_g[1]), mp[3], ms_blk[3], mp[4], ms_blk[4],
        wr_pad, br_pad, tm=tb, blocks_per_seq=t_len // tb, n_experts=n_experts)

    n_tok = n_p + n_s
    ts = TS_MOE
    n_super = (TOP_K * n_tok) // ts + n_experts + 1
    rt = _routing_tables(blk_cnt, counts, n_experts=n_experts, ts=ts, n_super=n_super)
    xs_sorted, g_sorted = _dispatch_call(rt["tile_lo"], rt["tile_hi"], rt["start"], meta_t, h_all,
                                         n_sorted=n_super * ts, n_experts=n_experts)
    ys_sorted = _moe_call(rt["sup_e"], rt["sup_nsub"], xs_sorted, g_sorted, moe_w_gu[0], moe_w_down[0], ts=ts)
    y_p, y_s = _combine_call(rt["win_tile"], rt["win_need2"], rt["start"], meta, x3, mp[5], ms_blk[5],
                             row(final_norm_g), ys_sorted,
                             n_p=n_p, tb=tb, blocks_per_seq=t_len // tb, n_experts=n_experts)

    y_prompt = y_p.reshape(n_seq_p, t_len, d)
    y_sample = y_s.reshape(n_seq_s, t_new, d)
    state_a_v_sample = v_rows.reshape(1, n_seq_s, t_new, da)
    k_prompt = kt_p.reshape(1, n_seq_p, heads, hd, t_len).transpose(0, 1, 4, 2, 3)
    v_prompt = vt_p.reshape(1, n_seq_p, heads, hd, t_len).transpose(0, 1, 4, 2, 3)
    logf_prompt = lft_p.transpose(0, 2, 1)[None]
    k_sample = k_s.reshape(1, n_seq_s, t_new, heads, hd)
    v_sample = v_s.reshape(1, n_seq_s, t_new, heads, hd)
    logf_sample = lft_s[0].T.reshape(1, n_seq_s, t_new, heads)
    return (y_prompt, y_sample, state_a_v_sample, k_prompt, v_prompt, logf_prompt, k_sample, v_sample, logf_sample)
```

```python
import functools

import jax
import jax.numpy as jnp
from jax import lax
from jax.experimental import pallas as pl
from jax.experimental.pallas import tpu as pltpu

F32 = jnp.float32
BF16 = jnp.bfloat16

N_HEADS = 16
GMLP_GROUPS = 8
GMLP_CHUNK = 128
TOP_K = 2
EPS = 1e-6
NEG = -1e30
LOG2E = 1.4426950408889634

LANES = 128
SUBLANES = 8
MXU_TILE = 256
VMEM_LIMIT = 56 * 1024 * 1024

TM_FFN = 1024
TF_FFN = 512
TM_PROJ = 512
TQ_ATTN = 512
TM_POST = 256
SUB = 256
FFN_ROWS = 2 * SUB
TS_MOE = 2560
PAGES_PER_STEP = 16


def _cparams(*sem):
    return pltpu.CompilerParams(dimension_semantics=sem, vmem_limit_bytes=VMEM_LIMIT)


def _dot(a, b):
    return jnp.dot(a, b, preferred_element_type=F32)


def _dot_nt(a, b):
    return lax.dot_general(a, b, (((1,), (1,)), ((), ())), preferred_element_type=F32)


def _norm_mod(x, g, shift, scale):
    ms = jnp.mean(x * x, axis=-1, keepdims=True)
    y = x * lax.rsqrt(ms + EPS) * g
    return y * (1.0 + scale) + shift


def _gelu(x):
    return 0.5 * x * (1.0 + lax.erf(x * (2.0 ** -0.5)))


def _silu(x):
    return x * jax.nn.sigmoid(x)


def _ones_where(cond):
    return jnp.where(cond, 1.0, 0.0).astype(BF16)


def _shift_div(x, c):
    assert c & (c - 1) == 0
    return lax.shift_right_logical(x, c.bit_length() - 1)


def _split3(x):
    hi = x.astype(BF16)
    r = x - hi.astype(F32)
    mid = r.astype(BF16)
    lo = (r - mid.astype(F32)).astype(BF16)
    return hi, mid, lo


def _ada_kernel(c_ref, w_ref, b_ref, o_ref):
    s = _silu(c_ref[...]).astype(BF16)
    o_ref[...] = _dot(s, w_ref[...].astype(BF16)) + b_ref[...]


def _ada_call(c_all, ada_w, ada_b):
    n_layers, d, d6 = ada_w.shape
    r = c_all.shape[0]
    tn = min(d6, 1536)
    return pl.pallas_call(
        _ada_kernel,
        grid=(n_layers, d6 // tn),
        in_specs=[
            pl.BlockSpec((r, d), lambda l, j: (0, 0)),
            pl.BlockSpec((None, d, tn), lambda l, j: (l, 0, j)),
            pl.BlockSpec((None, 1, tn), lambda l, j: (l, 0, j)),
        ],
        out_specs=pl.BlockSpec((None, r, tn), lambda l, j: (l, 0, j)),
        out_shape=jax.ShapeDtypeStruct((n_layers, r, d6), F32),
        compiler_params=_cparams("arbitrary", "arbitrary"),
        name="ada",
    )(c_all, ada_w, ada_b.reshape(n_layers, 1, d6))


def _mod_spec(mod, tiles_per_seq):
    _, rows, d = mod.shape
    return pl.BlockSpec((None, rows, d), lambda i, *_: (i // tiles_per_seq, 0, 0))


def _gmlp_kernel(x_ref, g_ref, sh_ref, sc_ref, gt_ref, wu_ref, wv_ref, vg_ref, vb_ref, ws_ref, bs_ref,
                 wo_ref, *rest, lc, period, groups, emit_v):
    if emit_v:
        o_ref, v_ref, vn_scr, out_scr = rest
    else:
        o_ref, vn_scr, out_scr = rest
    x = x_ref[...]
    tm = x.shape[0]
    h = _norm_mod(x, g_ref[...], sh_ref[...], sc_ref[...]).astype(BF16)
    v = _gelu(_dot(h, wv_ref[...]))
    mu = jnp.mean(v, axis=-1, keepdims=True)
    vc = v - mu
    var = jnp.mean(vc * vc, axis=-1, keepdims=True)
    vn = vc * lax.rsqrt(var + EPS) * vg_ref[...] + vb_ref[...]
    if emit_v:
        v_ref[...] = vn
    vn_scr[...] = vn.astype(BF16)
    gd = vn.shape[1] // groups
    r = lax.broadcasted_iota(jnp.int32, (lc, lc), 0)
    c = lax.broadcasted_iota(jnp.int32, (lc, lc), 1)
    mask = c <= r
    if period < lc:
        blk = ~(period - 1)
        mask = mask & ((r & blk) == (c & blk))
        ri = lax.broadcasted_iota(jnp.int32, (lc, LANES), 0)
        ci = lax.broadcasted_iota(jnp.int32, (lc, LANES), 1)
        sel = _ones_where((ri & (period - 1)) == ci)
    for g in range(groups):
        u = _gelu(_dot(h, wu_ref[:, g * gd:(g + 1) * gd]))
        if period < lc:
            rows_of_block = _dot(sel, ws_ref[g].astype(BF16)).astype(BF16)
            ws_full = _dot_nt(rows_of_block, sel)
        else:
            ws_full = ws_ref[g]
        wsm = jnp.where(mask, ws_full, 0.0).astype(BF16)
        bcol = bs_ref[:, g:g + 1]
        for ci in range(tm // lc):
            rows = slice(ci * lc, (ci + 1) * lc)
            mixed = _dot(wsm, vn_scr[rows, g * gd:(g + 1) * gd]) + bcol
            out_scr[rows, g * gd:(g + 1) * gd] = (u[rows] * mixed).astype(BF16)
    o_ref[...] = x + gt_ref[...] * _dot(out_scr[...], wo_ref[...])


def _gmlp_call(x, g, shift, scale, gate, wu, wv, vg, vb, ws, bs_t, wo, *, tm, tiles_per_seq, period, emit_v):
    n, d = x.shape
    da = wu.shape[1]
    groups = ws.shape[0]
    lc = bs_t.shape[0]
    const2 = lambda i: (0, 0)
    in_specs = [
        pl.BlockSpec((tm, d), lambda i: (i, 0)),
        pl.BlockSpec((1, d), const2),
        _mod_spec(shift, tiles_per_seq), _mod_spec(scale, tiles_per_seq), _mod_spec(gate, tiles_per_seq),
        pl.BlockSpec((d, da), const2), pl.BlockSpec((d, da), const2),
        pl.BlockSpec((1, da), const2), pl.BlockSpec((1, da), const2),
        pl.BlockSpec(ws.shape, lambda i: (0, 0, 0)),
        pl.BlockSpec((lc, groups), const2),
        pl.BlockSpec((da, d), const2),
    ]
    out_specs = [pl.BlockSpec((tm, d), lambda i: (i, 0))]
    out_shape = [jax.ShapeDtypeStruct((n, d), F32)]
    if emit_v:
        out_specs.append(pl.BlockSpec((tm, da), lambda i: (i, 0)))
        out_shape.append(jax.ShapeDtypeStruct((n, da), F32))
    return pl.pallas_call(
        functools.partial(_gmlp_kernel, lc=lc, period=period, groups=groups, emit_v=emit_v),
        grid=(n // tm,),
        in_specs=in_specs, out_specs=out_specs, out_shape=out_shape,
        scratch_shapes=[pltpu.VMEM((tm, da), BF16), pltpu.VMEM((tm, da), BF16)],
        compiler_params=_cparams("arbitrary"),
        name="gmlp",
    )(x, g, shift, scale, gate, wu, wv, vg, vb, ws, bs_t, wo)


def _ffn_kernel(x_ref, g_ref, sh_ref, sc_ref, gt_ref, wg_ref, wu_ref, wd_ref, o_ref, h_scr, acc_scr):
    j = pl.program_id(1)

    @pl.when(j == 0)
    def _():
        h_scr[...] = _norm_mod(x_ref[...], g_ref[...], sh_ref[...], sc_ref[...]).astype(BF16)
        acc_scr[...] = jnp.zeros_like(acc_scr)

    h = h_scr[...]
    a = _silu(_dot(h, wg_ref[...].astype(BF16))) * _dot(h, wu_ref[...].astype(BF16))
    acc_scr[...] += _dot(a.astype(BF16), wd_ref[...].astype(BF16))

    @pl.when(j == pl.num_programs(1) - 1)
    def _():
        o_ref[...] = x_ref[...] + gt_ref[...] * acc_scr[...]


def _ffn_call(x, g, shift, scale, gate, w_gu, w_down, *, tm, tiles_per_seq):
    n, d = x.shape
    f = w_down.shape[0]
    tf = min(TF_FFN, f)
    nf = f // tf
    const2 = lambda i, j: (0, 0)
    return pl.pallas_call(
        _ffn_kernel,
        grid=(n // tm, nf),
        in_specs=[
            pl.BlockSpec((tm, d), lambda i, j: (i, 0)),
            pl.BlockSpec((1, d), const2),
            _mod_spec(shift, tiles_per_seq), _mod_spec(scale, tiles_per_seq), _mod_spec(gate, tiles_per_seq),
            pl.BlockSpec((d, tf), lambda i, j: (0, j)),
            pl.BlockSpec((d, tf), lambda i, j: (0, nf + j)),
            pl.BlockSpec((tf, d), lambda i, j: (j, 0)),
        ],
        out_specs=pl.BlockSpec((tm, d), lambda i, j: (i, 0)),
        out_shape=jax.ShapeDtypeStruct((n, d), F32),
        scratch_shapes=[pltpu.VMEM((tm, d), BF16), pltpu.VMEM((tm, d), F32)],
        compiler_params=_cparams("arbitrary", "arbitrary"),
        name="ffn",
    )(x, g, shift, scale, gate, w_gu, w_gu, w_down)


def _log_sigmoid(x):
    return jnp.minimum(x, 0.0) - jnp.log(1.0 + jnp.exp(-jnp.abs(x)))


def _foxproj_kernel(x_ref, g_ref, sh_ref, sc_ref, w_ref, wf_ref, bf_ref, *rest, transposed, period,
                    tiles_per_seq, q_scale):
    if transposed:
        q_ref, k_ref, kb_ref, v_ref, vb_ref, lf_ref, cum_ref, carry_scr = rest
    else:
        q_ref, k_ref, v_ref, lf_ref, cum_ref, carry_scr = rest
    i = pl.program_id(0)
    x = x_ref[...]
    tm, d = x.shape
    h = _norm_mod(x, g_ref[...], sh_ref[...], sc_ref[...]).astype(BF16)
    if transposed:
        tq = q_ref.shape[-1]
        qt = (_dot_nt(w_ref[0:d, :], h) * q_scale).astype(BF16)
        for jq in range(tm // tq):
            q_ref[jq] = qt[:, jq * tq:(jq + 1) * tq]
        k_ref[...] = _dot_nt(w_ref[d:2 * d, :], h)
        kb_ref[...] = _dot_nt(h, w_ref[d:2 * d, :]).astype(BF16)
        vt = _dot_nt(w_ref[2 * d:3 * d, :], h)
        v_ref[...] = vt
        vb_ref[...] = vt.astype(BF16)
    else:
        q_ref[...] = _dot_nt(h, w_ref[0:d, :]) * q_scale
        k_ref[...] = _dot_nt(h, w_ref[d:2 * d, :])
        v_ref[...] = _dot_nt(h, w_ref[2 * d:3 * d, :])
    logf = _log_sigmoid(_dot_nt(wf_ref[...], h) + bf_ref[...])
    lf_ref[...] = logf
    s = lax.broadcasted_iota(jnp.int32, (tm, tm), 0)
    t = lax.broadcasted_iota(jnp.int32, (tm, tm), 1)
    upper = s <= t
    if period < tm:
        blk = ~(period - 1)
        upper = upper & ((s & blk) == (t & blk))
    upper = _ones_where(upper)
    hi, mid, lo = _split3(logf)
    cum = _dot(hi, upper) + _dot(mid, upper) + _dot(lo, upper)
    if tiles_per_seq > 1:
        @pl.when(i % tiles_per_seq == 0)
        def _():
            carry_scr[...] = jnp.zeros_like(carry_scr)
        cum = cum + carry_scr[:, 0:1]
        carry_scr[...] = jnp.broadcast_to(cum[:, tm - 1:tm], carry_scr.shape)
    cum_ref[...] = cum


def _foxproj_call(x, g, shift, scale, w_t, wf_t, b_f, *, tm, tiles_per_seq, transposed, period):
    n, d = x.shape
    heads = wf_t.shape[0]
    n_seq = n // (tm * tiles_per_seq)
    t_len = tm * tiles_per_seq
    const2 = lambda i: (0, 0)
    row = lambda i: (i, 0)
    seq_t = lambda i: (i // tiles_per_seq, 0, i % tiles_per_seq)
    in_specs = [
        pl.BlockSpec((tm, d), row),
        pl.BlockSpec((1, d), const2),
        _mod_spec(shift, tiles_per_seq), _mod_spec(scale, tiles_per_seq),
        pl.BlockSpec((3 * d, d), const2),
        pl.BlockSpec((heads, d), const2),
        pl.BlockSpec((heads, 1), const2),
    ]
    lf_spec = pl.BlockSpec((None, heads, tm), seq_t)
    lf_shape = jax.ShapeDtypeStruct((n_seq, heads, t_len), F32)
    if transposed:
        tq = min(TQ_ATTN, tm)
        kv_spec = pl.BlockSpec((None, d, tm), seq_t)
        q_spec = pl.BlockSpec((None, tm // tq, d, tq), lambda i: (i // tiles_per_seq, i % tiles_per_seq, 0, 0))
        out_specs = [q_spec, kv_spec, pl.BlockSpec((tm, d), row), kv_spec, kv_spec, lf_spec, lf_spec]
        out_shape = [jax.ShapeDtypeStruct((n_seq, t_len // tq, d, tq), BF16),
                     jax.ShapeDtypeStruct((n_seq, d, t_len), F32), jax.ShapeDtypeStruct((n, d), BF16),
                     jax.ShapeDtypeStruct((n_seq, d, t_len), F32), jax.ShapeDtypeStruct((n_seq, d, t_len), BF16),
                     lf_shape, lf_shape]
    else:
        out_specs = [pl.BlockSpec((tm, d), row)] * 3 + [lf_spec, lf_spec]
        out_shape = [jax.ShapeDtypeStruct((n, d), F32)] * 3 + [lf_shape, lf_shape]
    return pl.pallas_call(
        functools.partial(_foxproj_kernel, transposed=transposed, period=period, tiles_per_seq=tiles_per_seq,
                          q_scale=float(d // heads) ** -0.5 * (LOG2E if transposed else 1.0)),
        grid=(n // tm,),
        in_specs=in_specs, out_specs=out_specs, out_shape=out_shape,
        scratch_shapes=[pltpu.VMEM((heads, LANES), F32)],
        compiler_params=_cparams("arbitrary"),
        name="foxproj",
    )(x, g, shift, scale, w_t, wf_t, b_f)


def _attn_kernel(qt_ref, k_ref, vt_ref, cum_ref, o_ref, ck_scr, m_scr, l_scr, acc_scr, *, hd):
    nq, _, tq = qt_ref.shape
    lane = lax.broadcasted_iota(jnp.int32, (tq, LANES), 1)
    key = lax.broadcasted_iota(jnp.int32, (tq, tq), 0)
    qry = lax.broadcasted_iota(jnp.int32, (tq, tq), 1)
    eye = key == qry
    causal = key <= qry
    m_scr[...] = jnp.full_like(m_scr, NEG)
    l_scr[...] = jnp.zeros_like(l_scr)
    acc_scr[...] = jnp.zeros_like(acc_scr)

    def update(hh, qi, kh, vh, masked):
        t = _dot(kh, qt_ref[qi]) - ck_scr[...]
        if masked:
            t = jnp.where(causal, t, NEG)
        cq = cum_ref[hh, qi:qi + 1, :] * LOG2E
        m_old = m_scr[hh, qi]
        m_new = jnp.maximum(m_old, jnp.max(t, axis=0, keepdims=True) + cq)
        alpha = jnp.exp2(m_old - m_new)
        p = jnp.exp2(t + (cq - m_new))
        l_scr[hh, qi] = alpha * l_scr[hh, qi] + jnp.sum(p, axis=0, keepdims=True)
        acc_scr[hh, qi] = alpha * acc_scr[hh, qi] + _dot(vh, p.astype(BF16))
        m_scr[hh, qi] = m_new

    for ki in range(nq):
        keys = slice(ki * tq, (ki + 1) * tq)
        k_blk = k_ref[keys, :]
        for hh in range(2):
            kh = jnp.where((lane >= hh * hd) & (lane < (hh + 1) * hd), k_blk, jnp.zeros_like(k_blk))
            vh = vt_ref[hh * hd:(hh + 1) * hd, keys]
            col = jnp.sum(jnp.where(eye, cum_ref[hh, ki:ki + 1, :], 0.0), axis=1, keepdims=True)
            ck_scr[...] = jnp.broadcast_to(col * LOG2E, ck_scr.shape)
            update(hh, ki, kh, vh, True)
            for qi in range(ki + 1, nq):
                update(hh, qi, kh, vh, False)

    for qi in range(nq):
        ot = jnp.concatenate([acc_scr[0, qi] / l_scr[0, qi], acc_scr[1, qi] / l_scr[1, qi]], axis=0)
        o_ref[qi * tq:(qi + 1) * tq, :] = ot.T.astype(o_ref.dtype)


def _attn_call(qt, k, vt, cum5, *, t_len, heads):
    n_seq, nq, d, tq = qt.shape
    hd = d // heads
    assert 2 * hd == LANES
    return pl.pallas_call(
        functools.partial(_attn_kernel, hd=hd),
        grid=(n_seq, heads // 2),
        in_specs=[
            pl.BlockSpec((None, nq, LANES, tq), lambda b, p: (b, 0, p, 0)),
            pl.BlockSpec((t_len, LANES), lambda b, p: (b, p)),
            pl.BlockSpec((None, LANES, t_len), lambda b, p: (b, p, 0)),
            pl.BlockSpec((None, None, 2, nq, tq), lambda b, p: (b, p, 0, 0, 0)),
        ],
        out_specs=pl.BlockSpec((t_len, LANES), lambda b, p: (b, p)),
        out_shape=jax.ShapeDtypeStruct((n_seq * t_len, d), BF16),
        scratch_shapes=[pltpu.VMEM((tq, tq), F32), pltpu.VMEM((2, nq, 1, tq), F32),
                        pltpu.VMEM((2, nq, 1, tq), F32), pltpu.VMEM((2, nq, hd, tq), F32)],
        compiler_params=_cparams("arbitrary", "arbitrary"),
        name="attn_prompt",
    )(qt, k, vt, cum5)


def _attn_sample_body(j, n_j, co_work, q_ref, cn_ref, kn_ref, vn_ref, *rest, pages, hd, page_size):
    k_refs = rest[0:pages]
    v_refs = rest[pages:2 * pages]
    lf_refs = rest[2 * pages:3 * pages]
    o_ref, qbd_scr, cn_scr, m_scr, l_scr, acc_scr, suf_scr = rest[3 * pages:]
    tn, d = q_ref.shape
    heads = d // hd
    rows = heads * tn
    row = lax.broadcasted_iota(jnp.int32, (rows, LANES), 0)
    lane = lax.broadcasted_iota(jnp.int32, (rows, LANES), 1)
    q_of_row = row & (tn - 1)

    def rep(a):
        return jnp.broadcast_to(a[:, None, :], (heads, tn, a.shape[-1])).reshape(rows, a.shape[-1])

    gw = qbd_scr.shape[1]
    gr = (gw // hd) * tn
    n_groups = d // gw
    g_rows = lambda g: slice(g * gr, (g + 1) * gr)
    g_cols = lambda g: slice(g * gw, (g + 1) * gw)
    rr = lax.broadcasted_iota(jnp.int32, (gr, gw), 0)
    cc = lax.broadcasted_iota(jnp.int32, (gr, gw), 1)
    own_head = _shift_div(rr, tn) == _shift_div(cc, hd)

    @pl.when(j == 0)
    def _():
        q = q_ref[...]
        for g in range(n_groups):
            qrep = jnp.broadcast_to(q[None, :, g_cols(g)], (gw // hd, tn, gw)).reshape(gr, gw)
            qbd_scr[g_rows(g), :] = jnp.where(own_head, qrep, 0.0).astype(BF16)
        cn_scr[...] = jnp.sum(jnp.where(lane == q_of_row, rep(cn_ref[...]), 0.0), axis=1, keepdims=True)
        m_scr[...] = jnp.full_like(m_scr, NEG)
        l_scr[...] = jnp.zeros_like(l_scr)
        acc_scr[...] = jnp.zeros_like(acc_scr)
        suf_scr[...] = jnp.zeros_like(suf_scr)

    co_work()

    s_idx = lax.broadcasted_iota(jnp.int32, (page_size, page_size), 0)
    t_idx = lax.broadcasted_iota(jnp.int32, (page_size, page_size), 1)
    later = _ones_where(s_idx > t_idx)
    lf_all = jnp.concatenate([lf_refs[i][...] for i in range(pages)], axis=0)
    hi, mid, lo = _split3(lf_all)
    suf_all = _dot(hi, later) + _dot(mid, later) + _dot(lo, later)
    tot_all = suf_all[:, 0:1] + lf_all[:, 0:1]
    carry = suf_scr[:, 0:1]
    sufs = [None] * pages
    for i in reversed(range(pages)):
        sufs[i] = suf_all[i * heads:(i + 1) * heads, :] + carry
        carry = carry + tot_all[i * heads:(i + 1) * heads, :]
    suf_scr[...] = jnp.broadcast_to(carry, suf_scr.shape)

    def page_pair(refs, i, g):
        return jnp.concatenate([refs[i][g_cols(g), :].astype(BF16), refs[i + 1][g_cols(g), :].astype(BF16)], axis=1)

    scores = []
    for i in range(0, pages, 2):
        qk = [_dot(qbd_scr[g_rows(g), :], page_pair(k_refs, i, g)) for g in range(n_groups)]
        scores.append(jnp.concatenate(qk, axis=0) + rep(jnp.concatenate([sufs[i], sufs[i + 1]], axis=1)))
    s = jnp.concatenate(scores, axis=1) + cn_scr[...]
    m_old = m_scr[...]
    m_new = jnp.maximum(m_old, jnp.max(s, axis=1, keepdims=True))
    alpha = jnp.exp(m_old - m_new)
    p = jnp.exp(s - m_new)
    l_scr[...] = alpha * l_scr[...] + jnp.sum(p, axis=1, keepdims=True)
    pb = p.astype(BF16)
    for g in range(n_groups):
        pv = _dot_nt(pb[g_rows(g), 0:2 * page_size], page_pair(v_refs, 0, g))
        for i in range(2, pages, 2):
            pv = pv + _dot_nt(pb[g_rows(g), i * page_size:(i + 2) * page_size], page_pair(v_refs, i, g))
        acc_scr[g_rows(g), :] = alpha[g_rows(g), :] * acc_scr[g_rows(g), :] + pv
    m_scr[...] = m_new

    @pl.when(j == n_j - 1)
    def _():
        pad = jnp.zeros((LANES - tn, d), BF16)
        k_new = jnp.concatenate([kn_ref[...].astype(BF16), pad], axis=0)
        v_new = jnp.concatenate([vn_ref[...].astype(BF16), pad], axis=0)
        qk = [_dot_nt(qbd_scr[g_rows(g), :], k_new[:, g_cols(g)]) for g in range(n_groups)]
        s = jnp.concatenate(qk, axis=0) + cn_scr[...] - rep(cn_ref[...])
        s = jnp.where(lane <= q_of_row, s, NEG)
        m_old = m_scr[...]
        m_new = jnp.maximum(m_old, jnp.max(s, axis=1, keepdims=True))
        alpha = jnp.exp(m_old - m_new)
        pb = jnp.exp(s - m_new)
        l_new = alpha * l_scr[...] + jnp.sum(pb, axis=1, keepdims=True)
        pb = pb.astype(BF16)
        for g in range(n_groups):
            o = alpha[g_rows(g), :] * acc_scr[g_rows(g), :] + _dot(pb[g_rows(g), :], v_new[:, g_cols(g)])
            o = jnp.where(own_head, o / l_new[g_rows(g), :], 0.0)
            o_ref[:, g_cols(g)] = jnp.sum(o.reshape(gw // hd, tn, gw), axis=0)


N_GMLP_IN = 12


def _gmlp_attn_kernel(pt_ref, *refs, n_steps, pages, hd, page_size, lc, groups):
    n_attn_in = 4 + 3 * pages
    gmlp_in = refs[:N_GMLP_IN]
    attn_in = refs[N_GMLP_IN:N_GMLP_IN + n_attn_in]
    x1_ref, o_ref, vn_scr, out_scr, *attn_scr = refs[N_GMLP_IN + n_attn_in:]
    i = pl.program_id(0)
    mixer = functools.partial(_gmlp_kernel, *gmlp_in, x1_ref, vn_scr, out_scr, lc=lc, period=lc, groups=groups,
                              emit_v=False)
    _attn_sample_body(i % n_steps, n_steps, mixer, *attn_in, o_ref, *attn_scr,
                      pages=pages, hd=hd, page_size=page_size)


def _gmlp_attn_call(x, g, shift, scale, gate, wu, wv, vg, vb, ws, bs_t, wo,
                    page_table, q, cn_pad, k_new, v_new, kc_t, vc_t, lfc_t, *, t_len, tn, heads):
    n, d = x.shape
    da = wu.shape[1]
    groups = ws.shape[0]
    lc = bs_t.shape[0]
    n_s = q.shape[0]
    n_seq, n_pages = page_table.shape
    page_size = kc_t.shape[-1]
    pages = min(PAGES_PER_STEP, n_pages)
    n_steps = n_pages // pages
    hd = d // heads
    n_grid = n_seq * n_steps
    tm = n // n_grid
    assert tm * n_grid == n and tm % lc == 0 and t_len % tm == 0, (n, n_grid, lc)
    tiles_per_seq = t_len // tm
    once = pl.Buffered(1)
    const2 = lambda i, pt: (0, 0)
    in_specs = [
        pl.BlockSpec((tm, d), lambda i, pt: (i, 0)),
        pl.BlockSpec((1, d), const2),
        _mod_spec(shift, tiles_per_seq), _mod_spec(scale, tiles_per_seq), _mod_spec(gate, tiles_per_seq),
        pl.BlockSpec((d, da), const2, pipeline_mode=once), pl.BlockSpec((d, da), const2, pipeline_mode=once),
        pl.BlockSpec((1, da), const2), pl.BlockSpec((1, da), const2),
        pl.BlockSpec(ws.shape, lambda i, pt: (0, 0, 0)),
        pl.BlockSpec((lc, groups), const2),
        pl.BlockSpec((da, d), const2, pipeline_mode=once),
    ]
    assert len(in_specs) == N_GMLP_IN

    def page_map(ip):
        def index(i, pt):
            return (pt[(i // n_steps) * n_pages + (n_steps - 1 - i % n_steps) * pages + ip], 0, 0)
        return index

    seq_row = lambda i, pt: (i // n_steps, 0)
    in_specs += [
        pl.BlockSpec((tn, d), seq_row),
        pl.BlockSpec((None, heads, LANES), lambda i, pt: (i // n_steps, 0, 0)),
        pl.BlockSpec((tn, d), seq_row), pl.BlockSpec((tn, d), seq_row),
    ]
    in_specs += [pl.BlockSpec((None, d, page_size), page_map(ip)) for ip in range(pages)]
    in_specs += [pl.BlockSpec((None, d, page_size), page_map(ip)) for ip in range(pages)]
    in_specs += [pl.BlockSpec((None, heads, page_size), page_map(ip)) for ip in range(pages)]
    rows = heads * tn
    return pl.pallas_call(
        functools.partial(_gmlp_attn_kernel, n_steps=n_steps, pages=pages, hd=hd, page_size=page_size,
                          lc=lc, groups=groups),
        grid_spec=pltpu.PrefetchScalarGridSpec(
            num_scalar_prefetch=1,
            grid=(n_grid,),
            in_specs=in_specs,
            out_specs=[pl.BlockSpec((tm, d), lambda i, pt: (i, 0)), pl.BlockSpec((tn, d), seq_row)],
            scratch_shapes=[pltpu.VMEM((tm, da), BF16), pltpu.VMEM((tm, da), BF16),
                            pltpu.VMEM((rows, MXU_TILE), BF16), pltpu.VMEM((rows, 1), F32),
                            pltpu.VMEM((rows, 1), F32), pltpu.VMEM((rows, 1), F32),
                            pltpu.VMEM((rows, MXU_TILE), F32), pltpu.VMEM((heads, LANES), F32)],
        ),
        out_shape=[jax.ShapeDtypeStruct((n, d), F32), jax.ShapeDtypeStruct((n_s, d), F32)],
        compiler_params=_cparams("arbitrary"),
        name="gmlp_attn_sample",
    )(page_table.reshape(-1), x, g, shift, scale, gate, wu, wv, vg, vb, ws, bs_t, wo,
      q, cn_pad, k_new, v_new, *([kc_t] * pages), *([vc_t] * pages), *([lfc_t] * pages))


def _post_kernel(op_ref, os_ref, wo_ref, xp_ref, xs_ref, gtp_ref, gts_ref, g_ref, shp_ref, shs_ref, scp_ref,
                 scs_ref, wr_ref, br_ref, x3_ref, h_ref, meta_ref, metat_ref, blk_ref, cout_ref, carry_scr,
                 *, n_experts, n_prompt_blocks):
    i = pl.program_id(0)
    tm = xp_ref.shape[0]
    is_p = i < n_prompt_blocks

    @pl.when(i == 0)
    def _():
        carry_scr[...] = jnp.zeros_like(carry_scr)

    o = jnp.where(is_p, op_ref[...], os_ref[...].astype(BF16))
    x = jnp.where(is_p, xp_ref[...], xs_ref[...])
    gate = jnp.where(is_p, gtp_ref[...], gts_ref[...])
    shift = jnp.where(is_p, shp_ref[...], shs_ref[...])
    scale = jnp.where(is_p, scp_ref[...], scs_ref[...])
    x3 = x + gate * _dot(o, wo_ref[...])
    x3_ref[...] = x3
    h = _norm_mod(x3, g_ref[...], shift, scale)
    h_hi = h.astype(BF16)
    h_ref[...] = h_hi
    h_lo = (h - h_hi.astype(F32)).astype(BF16)
    wr = wr_ref[...]
    w_hi = wr.astype(BF16)
    w_lo = (wr - w_hi.astype(F32)).astype(BF16)
    logits = _dot(h_hi, w_hi) + _dot(h_hi, w_lo) + _dot(h_lo, w_hi) + br_ref[...]
    lane_i = lax.broadcasted_iota(jnp.int32, (tm, LANES), 1)
    lane = lane_i.astype(F32)
    logits = jnp.where(lane_i < n_experts, logits, NEG)
    l1 = jnp.max(logits, axis=1, keepdims=True)
    i1 = jnp.min(jnp.where(logits == l1, lane, float(LANES)), axis=1, keepdims=True)
    rest = jnp.where(lane == i1, NEG, logits)
    l2 = jnp.max(rest, axis=1, keepdims=True)
    i2 = jnp.min(jnp.where(rest == l2, lane, float(LANES)), axis=1, keepdims=True)
    e = jnp.exp(l2 - l1)
    g1 = 1.0 / (1.0 + e)
    g2 = e / (1.0 + e)
    onehot = jnp.where((lane == i1) | (lane == i2), 1.0, 0.0)
    r = lax.broadcasted_iota(jnp.int32, (tm, tm), 0)
    c = lax.broadcasted_iota(jnp.int32, (tm, tm), 1)
    before = _ones_where(c < r)
    carry = carry_scr[0:1, :]
    blk_ref[...] = carry
    prefix = _dot(before, onehot.astype(BF16)) + carry
    r1 = jnp.sum(jnp.where(lane == i1, prefix, 0.0), axis=1, keepdims=True)
    r2 = jnp.sum(jnp.where(lane == i2, prefix, 0.0), axis=1, keepdims=True)
    cols = (i1, i2, g1, g2, r1, r2)
    meta = jnp.zeros((tm, LANES), F32)
    for k, col in enumerate(cols):
        meta = jnp.where(lane_i == k, col, meta)
    meta_ref[...] = meta
    metat_ref[...] = meta.T[0:SUBLANES, :]
    carry = carry + jnp.sum(onehot, axis=0, keepdims=True)
    carry_scr[...] = jnp.broadcast_to(carry, carry_scr.shape)
    cout_ref[...] = carry


def _post_call(o_p, o_s, wo, x_p, x_s, gate_p, gate_s, g, shift_p, shift_s, scale_p, scale_s, wr_pad, br_pad,
               *, tm, blocks_per_seq, n_experts):
    n_p, d = x_p.shape
    n_s = x_s.shape[0]
    npb = n_p // tm
    nb = npb + n_s // tm
    n = n_p + n_s
    const2 = lambda i: (0, 0)
    row = lambda i: (i, 0)
    p_row = lambda i: (jnp.minimum(i, npb - 1), 0)
    s_row = lambda i: (jnp.maximum(i - npb, 0), 0)
    p_mod = pl.BlockSpec((None, 1, d), lambda i: (jnp.minimum(i, npb - 1) // blocks_per_seq, 0, 0))
    s_mod = pl.BlockSpec((None, tm, d), lambda i: (jnp.maximum(i - npb, 0), 0, 0))
    return pl.pallas_call(
        functools.partial(_post_kernel, n_experts=n_experts, n_prompt_blocks=npb),
        grid=(nb,),
        in_specs=[
            pl.BlockSpec((tm, d), p_row), pl.BlockSpec((tm, d), s_row),
            pl.BlockSpec((d, d), const2),
            pl.BlockSpec((tm, d), p_row), pl.BlockSpec((tm, d), s_row),
            p_mod, s_mod,
            pl.BlockSpec((1, d), const2),
            p_mod, s_mod, p_mod, s_mod,
            pl.BlockSpec((d, LANES), const2),
            pl.BlockSpec((1, LANES), const2),
        ],
        out_specs=[
            pl.BlockSpec((tm, d), row), pl.BlockSpec((tm, d), row), pl.BlockSpec((tm, LANES), row),
            pl.BlockSpec((None, SUBLANES, tm), lambda i: (i, 0, 0)),
            pl.BlockSpec((None, 1, LANES), lambda i: (i, 0, 0)),
            pl.BlockSpec((1, LANES), const2),
        ],
        out_shape=[
            jax.ShapeDtypeStruct((n, d), F32), jax.ShapeDtypeStruct((n, d), BF16),
            jax.ShapeDtypeStruct((n, LANES), F32),
            jax.ShapeDtypeStruct((nb, SUBLANES, tm), F32),
            jax.ShapeDtypeStruct((nb, 1, LANES), F32),
            jax.ShapeDtypeStruct((1, LANES), F32),
        ],
        scratch_shapes=[pltpu.VMEM((SUBLANES, LANES), F32)],
        compiler_params=_cparams("arbitrary"),
        name="post_attn_router",
    )(o_p, o_s, wo, x_p, x_s, gate_p, gate_s, g, shift_p, shift_s, scale_p, scale_s, wr_pad, br_pad)


def _sorted_pos(idx, rank, start_ref, n_experts):
    pos = rank
    for e in range(n_experts):
        pos = pos + jnp.where(idx == float(e), start_ref[e].astype(F32), 0.0)
    return pos


def _dispatch_kernel(lo_ref, hi_ref, start_ref, mt_ref, h_ref, xs_ref, gs_ref, acc_scr, gacc_scr,
                     *, tb, n_experts):
    j = pl.program_id(0)
    sub = xs_ref.shape[0]
    nb = mt_ref.shape[0]
    lo = lo_ref[j]
    hi = hi_ref[j]

    @pl.when(hi <= lo)
    def _():
        xs_ref[...] = jnp.zeros_like(xs_ref)
        gs_ref[...] = jnp.zeros_like(gs_ref)

    def hits(b, width):
        dest = (j * sub + lax.broadcasted_iota(jnp.int32, (sub, width * tb), 0)).astype(F32)
        mts = [mt_ref[b + w] for w in range(width)]
        field = lambda r: jnp.concatenate([m[r:r + 1, :] for m in mts], axis=1)
        hit0 = _sorted_pos(field(0), field(4), start_ref, n_experts) == dest
        hit1 = _sorted_pos(field(1), field(5), start_ref, n_experts) == dest
        gate = jnp.where(hit0, field(2), 0.0) + jnp.where(hit1, field(3), 0.0)
        return _ones_where(hit0 | hit1), gate

    n_even = nb - nb % 2

    def pair(m, carry):
        b = 2 * m
        oh, gate = hits(b, 2)
        rows = pl.ds(pl.multiple_of(b * tb, 2 * tb), 2 * tb)
        acc_scr[...] += _dot(oh, h_ref[rows, :])
        gacc_scr[...] += gate
        return carry

    @pl.when(hi > lo)
    def _():
        acc_scr[...] = jnp.zeros_like(acc_scr)
        gacc_scr[...] = jnp.zeros_like(gacc_scr)
        lax.fori_loop(lo // 2, (jnp.minimum(hi, n_even) + 1) // 2, pair, 0)

        if nb % 2:
            @pl.when(hi == nb)
            def _():
                oh, gate = hits(nb - 1, 1)
                acc_scr[...] += _dot(oh, h_ref[(nb - 1) * tb:nb * tb, :])
                gacc_scr[:, 0:tb] += gate

        xs_ref[...] = acc_scr[...].astype(BF16)
        gs_ref[...] = jnp.broadcast_to(jnp.sum(gacc_scr[...], axis=1, keepdims=True), gs_ref.shape)


def _dispatch_call(tile_lo, tile_hi, start, meta_t, h, *, n_sorted, n_experts):
    n, d = h.shape
    nb, _, tb = meta_t.shape
    whole = lambda nd: (lambda j, lo, hi, st: (0,) * nd)
    once = pl.Buffered(1)
    return pl.pallas_call(
        functools.partial(_dispatch_kernel, tb=tb, n_experts=n_experts),
        grid_spec=pltpu.PrefetchScalarGridSpec(
            num_scalar_prefetch=3,
            grid=(n_sorted // SUB,),
            in_specs=[
                pl.BlockSpec((nb, SUBLANES, tb), whole(3), pipeline_mode=once),
                pl.BlockSpec((n, d), whole(2), pipeline_mode=once),
            ],
            out_specs=[pl.BlockSpec((SUB, d), lambda j, lo, hi, st: (j, 0)),
                       pl.BlockSpec((SUB, LANES), lambda j, lo, hi, st: (j, 0))],
            scratch_shapes=[pltpu.VMEM((SUB, d), F32), pltpu.VMEM((SUB, 2 * tb), F32)],
        ),
        out_shape=[jax.ShapeDtypeStruct((n_sorted, d), BF16), jax.ShapeDtypeStruct((n_sorted, LANES), F32)],
        compiler_params=_cparams("arbitrary"),
        name="moe_dispatch",
    )(tile_lo, tile_hi, start, meta_t, h)


def _moe_kernel(ex_ref, nsub_ref, x_ref, gs_ref, wg_ref, wu_ref, wd_ref, y_ref, acc_scr):
    s = pl.program_id(0)
    j = pl.program_id(1)
    nsub = nsub_ref[s]
    last = j == pl.num_programs(1) - 1

    n_full = (nsub * SUB) // FFN_ROWS
    odd = nsub * SUB - n_full * FFN_ROWS > 0

    def tile_rows(t):
        return pl.ds(pl.multiple_of(t * SUB, SUB), SUB)

    def step_rows(t):
        return pl.ds(pl.multiple_of(t * FFN_ROWS, FFN_ROWS), FFN_ROWS)

    @pl.when(j == 0)
    def _():
        def zero(t, carry):
            acc_scr[tile_rows(t), :] = jnp.zeros((SUB, acc_scr.shape[1]), F32)
            return carry
        lax.fori_loop(0, nsub, zero, 0)

    def ffn(rows):
        x = x_ref[rows, :]
        a = _silu(_dot(x, wg_ref[...].astype(BF16))) * _dot(x, wu_ref[...].astype(BF16))
        acc_scr[rows, :] += _dot(a.astype(BF16), wd_ref[...].astype(BF16))

    def body(t, carry):
        ffn(step_rows(2 * t))
        ffn(step_rows(2 * t + 1))
        return carry

    lax.fori_loop(0, n_full // 2, body, 0)

    @pl.when(n_full % 2 == 1)
    def _():
        ffn(step_rows(n_full - 1))

    @pl.when(odd)
    def _():
        ffn(tile_rows(nsub - 1))

    @pl.when(last)
    def _():
        def emit(t, carry):
            rows = tile_rows(t)
            y_ref[rows, :] = (acc_scr[rows, :] * gs_ref[rows, 0:1]).astype(BF16)
            return carry
        lax.fori_loop(0, nsub, emit, 0)

        def zero(t, carry):
            y_ref[tile_rows(t), :] = jnp.zeros((SUB, y_ref.shape[1]), BF16)
            return carry
        lax.fori_loop(nsub, y_ref.shape[0] // SUB, zero, 0)


def _moe_call(sup_expert, sup_nsub, xs, gs, w_gu, w_down, *, ts):
    n_sorted, d = xs.shape
    f = w_down.shape[1]
    tf = min(TF_FFN, f)
    nf = f // tf
    n_super = n_sorted // ts

    def jj(s, j, nsub):
        return jnp.where(nsub[s] > 0, j, nf - 1)

    return pl.pallas_call(
        _moe_kernel,
        grid_spec=pltpu.PrefetchScalarGridSpec(
            num_scalar_prefetch=2,
            grid=(n_super, nf),
            in_specs=[
                pl.BlockSpec((ts, d), lambda s, j, ex, ns: (s, 0)),
                pl.BlockSpec((ts, LANES), lambda s, j, ex, ns: (s, 0)),
                pl.BlockSpec((None, d, tf), lambda s, j, ex, ns: (ex[s], 0, jj(s, j, ns))),
                pl.BlockSpec((None, d, tf), lambda s, j, ex, ns: (ex[s], 0, nf + jj(s, j, ns))),
                pl.BlockSpec((None, tf, d), lambda s, j, ex, ns: (ex[s], jj(s, j, ns), 0)),
            ],
            out_specs=pl.BlockSpec((ts, d), lambda s, j, ex, ns: (s, 0)),
            scratch_shapes=[pltpu.VMEM((ts, d), F32)],
        ),
        out_shape=jax.ShapeDtypeStruct((n_sorted, d), BF16),
        compiler_params=_cparams("arbitrary", "arbitrary"),
        name="moe_ffn",
    )(sup_expert, sup_nsub, xs, gs, w_gu, w_gu, w_down)


def _combine_kernel(tile_ref, need2_ref, start_ref, meta_ref, x_ref, gtp_ref, gts_ref, gf_ref, *rest,
                    n_experts, n_prompt_blocks):
    y_refs = rest[0:2 * n_experts]
    op_ref, os_ref, acc_scr = rest[2 * n_experts:]
    b = pl.program_id(0)
    tb = acc_scr.shape[0]
    meta = meta_ref[...]
    col = lax.broadcasted_iota(jnp.int32, (tb, SUB), 1)

    def window(e, w):
        first = start_ref[e].astype(F32)
        p0 = jnp.where(meta[:, 0:1] == float(e), meta[:, 4:5] + first, -1.0)
        p1 = jnp.where(meta[:, 1:2] == float(e), meta[:, 5:6] + first, -1.0)
        src = ((tile_ref[b * n_experts + e] + w) * SUB + col).astype(F32)
        return _ones_where((p0 == src) | (p1 == src))

    total = _dot(window(0, 0), y_refs[0][...])
    for e in range(1, n_experts):
        total = total + _dot(window(e, 0), y_refs[2 * e][...])
    acc_scr[...] = total
    for e in range(n_experts):
        @pl.when(need2_ref[b * n_experts + e] > 0)
        def _(e=e):
            acc_scr[...] += _dot(window(e, 1), y_refs[2 * e + 1][...])

    is_p = b < n_prompt_blocks
    x4 = x_ref[...] + jnp.where(is_p, gtp_ref[...], gts_ref[...]) * acc_scr[...]
    ms = jnp.mean(x4 * x4, axis=-1, keepdims=True)
    y = x4 * lax.rsqrt(ms + EPS) * gf_ref[...]

    @pl.when(is_p)
    def _():
        op_ref[...] = y

    @pl.when(jnp.logical_not(is_p))
    def _():
        os_ref[...] = y


def _combine_call(win_tile, win_need2, start, meta, x, gate_p, gate_s, g_final, ys, *, n_p, tb, blocks_per_seq,
                  n_experts):
    n, d = x.shape
    n_s = n - n_p
    npb = n_p // tb
    nb = n // tb

    def pb(b):
        return jnp.minimum(b, npb - 1)

    def sb(b):
        return jnp.maximum(b - npb, 0)

    def win_map(e, w):
        if w == 0:
            return lambda b, tile, need, st: (tile[b * n_experts + e], 0)
        return lambda b, tile, need, st: (jnp.where(need[b * n_experts + e] > 0, tile[b * n_experts + e] + 1, 0), 0)

    in_specs = [
        pl.BlockSpec((tb, LANES), lambda b, tile, need, st: (b, 0)),
        pl.BlockSpec((tb, d), lambda b, tile, need, st: (b, 0)),
        pl.BlockSpec((None, 1, d), lambda b, tile, need, st: (pb(b) // blocks_per_seq, 0, 0)),
        pl.BlockSpec((None, tb, d), lambda b, tile, need, st: (sb(b), 0, 0)),
        pl.BlockSpec((1, d), lambda b, tile, need, st: (0, 0)),
    ]
    in_specs += [pl.BlockSpec((SUB, d), win_map(e, w)) for e in range(n_experts) for w in range(2)]
    return pl.pallas_call(
        functools.partial(_combine_kernel, n_experts=n_experts, n_prompt_blocks=npb),
        grid_spec=pltpu.PrefetchScalarGridSpec(
            num_scalar_prefetch=3,
            grid=(nb,),
            in_specs=in_specs,
            out_specs=[pl.BlockSpec((tb, d), lambda b, tile, need, st: (pb(b), 0)),
                       pl.BlockSpec((tb, d), lambda b, tile, need, st: (sb(b), 0))],
            scratch_shapes=[pltpu.VMEM((tb, d), F32)],
        ),
        out_shape=[jax.ShapeDtypeStruct((n_p, d), F32), jax.ShapeDtypeStruct((n_s, d), F32)],
        compiler_params=_cparams("arbitrary"),
        name="moe_combine",
    )(win_tile, win_need2, start, meta, x, gate_p, gate_s, g_final, *([ys] * (2 * n_experts)))


def _routing_tables(blk_cnt, counts, *, n_experts, ts, n_super):
    nb = blk_cnt.shape[0]
    cnt = counts[0, :n_experts].astype(jnp.int32)
    n_sup_e = (cnt + ts - 1) // ts
    sup_start = jnp.cumsum(n_sup_e) - n_sup_e
    start = sup_start * ts

    s_ids = jnp.arange(n_super, dtype=jnp.int32)
    used = jnp.sum(n_sup_e)
    sup_end = sup_start + n_sup_e
    sup_e = jnp.sum((s_ids[:, None] >= sup_end[None, :]).astype(jnp.int32), axis=1)
    sup_e = jnp.clip(sup_e, 0, n_experts - 1)
    last_e = jnp.max(jnp.where(cnt > 0, jnp.arange(n_experts, dtype=jnp.int32), 0))
    sup_e = jnp.where(s_ids < used, sup_e, last_e)
    rows_in = jnp.clip(cnt[sup_e] - (s_ids - sup_start[sup_e]) * ts, 0, ts)
    sup_nsub = jnp.where(s_ids < used, (rows_in + SUB - 1) // SUB, 0).astype(jnp.int32)

    blk = blk_cnt[:, 0, :n_experts].astype(jnp.int32)
    blk_end = jnp.concatenate([blk[1:], cnt[None, :]], axis=0)
    n_tiles = n_super * (ts // SUB)
    t_ids = jnp.arange(n_tiles, dtype=jnp.int32)
    t_e = sup_e[t_ids // (ts // SUB)]
    t_r0 = t_ids * SUB - start[t_e]
    t_active = ((t_ids // (ts // SUB)) < used) & (t_r0 < cnt[t_e]) & (t_r0 >= 0)
    t_r1 = jnp.minimum(t_r0 + SUB, cnt[t_e])
    be = blk_end[:, t_e]
    bs = blk[:, t_e]
    overlap = (be > t_r0[None, :]) & (bs < t_r1[None, :])
    b_ids = jnp.arange(nb, dtype=jnp.int32)[:, None]
    lo = jnp.min(jnp.where(overlap, b_ids, nb), axis=0)
    hi = jnp.max(jnp.where(overlap, b_ids + 1, 0), axis=0)
    tile_lo = jnp.where(t_active, lo, 0).astype(jnp.int32)
    tile_hi = jnp.where(t_active, hi, 0).astype(jnp.int32)

    first = start[None, :] + blk
    n_be = blk_end - blk
    win_tile = (first // SUB).astype(jnp.int32)
    win_need2 = ((first + n_be) > (win_tile + 1) * SUB).astype(jnp.int32)

    return dict(start=start.astype(jnp.int32), sup_e=sup_e, sup_nsub=sup_nsub, tile_lo=tile_lo,
                tile_hi=tile_hi, win_tile=win_tile.reshape(-1), win_need2=win_need2.reshape(-1))


def kernel(x_prompt, x_sample, c_prompt, c_sample, cache_k, cache_v, cache_logf, page_table, norm_mix_g, norm_ffn_g, final_norm_g, ada_w, ada_b, gmlp_w_in, gmlp_v_g, gmlp_v_b, gmlp_w_s, gmlp_b_s, gmlp_w_out, fox_w_in, fox_b_f, fox_w_out, ffn_w_gu, ffn_w_down, moe_w_r, moe_b_r, moe_w_gu, moe_w_down):
    n_seq_p, t_len, d = x_prompt.shape
    n_seq_s, t_new, _ = x_sample.shape
    n_p = n_seq_p * t_len
    n_s = n_seq_s * t_new
    heads = N_HEADS
    hd = d // heads
    n_experts = moe_w_r.shape[-1]
    da = gmlp_w_out.shape[1]

    xp = x_prompt.reshape(n_p, d)
    xs = x_sample.reshape(n_s, d)

    mod = _ada_call(jnp.concatenate([c_prompt, c_sample], axis=0), ada_w, ada_b)

    def mods(layer):
        mp = [mod[layer, :n_seq_p, c * d:(c + 1) * d].reshape(n_seq_p, 1, d) for c in range(6)]
        ms = [jnp.repeat(mod[layer, n_seq_p:, c * d:(c + 1) * d], t_new, axis=0).reshape(1, n_s, d)
              for c in range(6)]
        return mp, ms

    row = lambda a: a.reshape(1, -1)

    mp, ms = mods(0)
    w_in = gmlp_w_in[0].astype(BF16)
    wu, wv = w_in[:, :da], w_in[:, da:]
    wo = gmlp_w_out[0].astype(BF16)
    vg, vb = row(gmlp_v_g[0]), row(gmlp_v_b[0])
    lc = min(GMLP_CHUNK, t_len)
    mp1, ms1 = mods(1)
    w_t = fox_w_in[0].T.astype(BF16)
    wqkv_t, wf_t = w_t[:3 * d], w_t[3 * d:]
    b_f = fox_b_f[0].reshape(heads, 1)

    reps = n_s // t_new
    ws_s = jnp.zeros((gmlp_w_s.shape[1], LANES, LANES), F32).at[:, :t_new, :t_new].set(
        gmlp_w_s[0][:, :t_new, :t_new])
    bs_s = jnp.tile(gmlp_b_s[0][:, :t_new], (1, reps)).T
    xs, v_rows = _gmlp_call(xs, row(norm_mix_g[0]), ms[0], ms[1], ms[2], wu, wv, vg, vb, ws_s, bs_s, wo,
                            tm=n_s, tiles_per_seq=1, period=t_new, emit_v=True)
    xs = _ffn_call(xs, row(norm_ffn_g[0]), ms[3], ms[4], ms[5], ffn_w_gu[0], ffn_w_down[0],
                   tm=n_s, tiles_per_seq=1)
    q_s, k_s, v_s, lft_s, cumt_s = _foxproj_call(
        xs, row(norm_mix_g[1]), ms1[0], ms1[1], wqkv_t, wf_t, b_f,
        tm=n_s, tiles_per_seq=1, transposed=False, period=t_new)

    page_size = cache_k.shape[2]
    n_phys = cache_k.shape[1]
    kc_t = jnp.transpose(cache_k[0], (0, 2, 3, 1)).reshape(n_phys, d, page_size)
    vc_t = jnp.transpose(cache_v[0], (0, 2, 3, 1)).reshape(n_phys, d, page_size)
    lfc_t = jnp.transpose(cache_logf[0], (0, 2, 1))
    cn = cumt_s[0].reshape(heads, n_seq_s, t_new).transpose(1, 0, 2)
    cn_pad = jnp.zeros((n_seq_s, heads, LANES), F32).at[:, :, :t_new].set(cn)
    xp, o_s = _gmlp_attn_call(xp, row(norm_mix_g[0]), mp[0], mp[1], mp[2], wu, wv, vg, vb,
                              gmlp_w_s[0][:, :lc, :lc], gmlp_b_s[0][:, :lc].T, wo,
                              page_table, q_s, cn_pad, k_s, v_s, kc_t, vc_t, lfc_t,
                              t_len=t_len, tn=t_new, heads=heads)
    tm = min(TM_FFN, t_len)
    xp = _ffn_call(xp, row(norm_ffn_g[0]), mp[3], mp[4], mp[5], ffn_w_gu[0], ffn_w_down[0],
                   tm=tm, tiles_per_seq=t_len // tm)

    mp, ms = mp1, ms1
    tm = min(TM_PROJ, t_len)
    qt_p, kt_p, kb_p, vt_p, vtb_p, lft_p, cumt_p = _foxproj_call(
        xp, row(norm_mix_g[1]), mp[0], mp[1], wqkv_t, wf_t, b_f,
        tm=tm, tiles_per_seq=t_len // tm, transposed=True, period=tm)
    nq, tq = qt_p.shape[1], qt_p.shape[3]
    o_p = _attn_call(qt_p, kb_p, vtb_p, cumt_p.reshape(n_seq_p, heads // 2, 2, nq, tq), t_len=t_len, heads=heads)

    wo_f = fox_w_out[0].astype(BF16)
    wr_pad = jnp.zeros((d, LANES), F32).at[:, :n_experts].set(moe_w_r[0])
    br_pad = jnp.zeros((1, LANES), F32).at[0, :n_experts].set(moe_b_r[0])
    tb = min(TM_POST, t_len)
    ms_blk = [m.reshape(n_s // tb, tb, d) for m in ms]
    x3, h_all, meta, meta_t, blk_cnt, counts = _post_call(
        o_p, o_s, wo_f, xp, xs, mp[2], ms_blk[2], row(norm_ffn_g[1]), mp[3], ms_blk[3], mp[4], ms_blk[4],
        wr_pad, br_pad, tm=tb, blocks_per_seq=t_len // tb, n_experts=n_experts)

    n_tok = n_p + n_s
    ts = TS_MOE
    n_super = (TOP_K * n_tok) // ts + n_experts + 1
    rt = _routing_tables(blk_cnt, counts, n_experts=n_experts, ts=ts, n_super=n_super)
    xs_sorted, g_sorted = _dispatch_call(rt["tile_lo"], rt["tile_hi"], rt["start"], meta_t, h_all,
                                         n_sorted=n_super * ts, n_experts=n_experts)
    ys_sorted = _moe_call(rt["sup_e"], rt["sup_nsub"], xs_sorted, g_sorted, moe_w_gu[0], moe_w_down[0], ts=ts)
    y_p, y_s = _combine_call(rt["win_tile"], rt["win_need2"], rt["start"], meta, x3, mp[5], ms_blk[5],
                             row(final_norm_g), ys_sorted,
                             n_p=n_p, tb=tb, blocks_per_seq=t_len // tb, n_experts=n_experts)

    y_prompt = y_p.reshape(n_seq_p, t_len, d)
    y_sample = y_s.reshape(n_seq_s, t_new, d)
    state_a_v_sample = v_rows.reshape(1, n_seq_s, t_new, da)
    k_prompt = kt_p.reshape(1, n_seq_p, heads, hd, t_len).transpose(0, 1, 4, 2, 3)
    v_prompt = vt_p.reshape(1, n_seq_p, heads, hd, t_len).transpose(0, 1, 4, 2, 3)
    logf_prompt = lft_p.transpose(0, 2, 1)[None]
    k_sample = k_s.reshape(1, n_seq_s, t_new, heads, hd)
    v_sample = v_s.reshape(1, n_seq_s, t_new, heads, hd)
    logf_sample = lft_s[0].T.reshape(1, n_seq_s, t_new, heads)
    return (y_prompt, y_sample, state_a_v_sample, k_prompt, v_prompt, logf_prompt, k_sample, v_sample, logf_sample)
```

```python
import functools

import jax
import jax.numpy as jnp
from jax import lax
from jax.experimental import pallas as pl
from jax.experimental.pallas import tpu as pltpu

F32 = jnp.float32
BF16 = jnp.bfloat16

N_HEADS = 16
GMLP_GROUPS = 8
GMLP_CHUNK = 128
TOP_K = 2
EPS = 1e-6
NEG = -1e30
LOG2E = 1.4426950408889634

LANES = 128
SUBLANES = 8
MXU_TILE = 256
VMEM_LIMIT = 56 * 1024 * 1024

TM_FFN = 1024
TF_FFN = 512
TM_PROJ = 512
TQ_ATTN = 512
TM_POST = 256
SUB = 256
FFN_ROWS = 2 * SUB
TS_MOE = 2560
PAGES_PER_STEP = 16


def _cparams(*sem):
    return pltpu.CompilerParams(dimension_semantics=sem, vmem_limit_bytes=VMEM_LIMIT)


def _dot(a, b):
    return jnp.dot(a, b, preferred_element_type=F32)


def _dot_nt(a, b):
    return lax.dot_general(a, b, (((1,), (1,)), ((), ())), preferred_element_type=F32)


def _norm_mod(x, g, shift, scale):
    ms = jnp.mean(x * x, axis=-1, keepdims=True)
    y = x * lax.rsqrt(ms + EPS) * g
    return y * (1.0 + scale) + shift


def _gelu(x):
    return 0.5 * x * (1.0 + lax.erf(x * (2.0 ** -0.5)))


def _silu(x):
    return x * jax.nn.sigmoid(x)


def _ones_where(cond):
    return jnp.where(cond, 1.0, 0.0).astype(BF16)


def _shift_div(x, c):
    assert c & (c - 1) == 0
    return lax.shift_right_logical(x, c.bit_length() - 1)


def _split3(x):
    hi = x.astype(BF16)
    r = x - hi.astype(F32)
    mid = r.astype(BF16)
    lo = (r - mid.astype(F32)).astype(BF16)
    return hi, mid, lo


def _ada_kernel(c_ref, w_ref, b_ref, o_ref):
    s = _silu(c_ref[...]).astype(BF16)
    o_ref[...] = _dot(s, w_ref[...].astype(BF16)) + b_ref[...]


def _ada_call(c_all, ada_w, ada_b):
    n_layers, d, d6 = ada_w.shape
    r = c_all.shape[0]
    tn = min(d6, 1536)
    return pl.pallas_call(
        _ada_kernel,
        grid=(n_layers, d6 // tn),
        in_specs=[
            pl.BlockSpec((r, d), lambda l, j: (0, 0)),
            pl.BlockSpec((None, d, tn), lambda l, j: (l, 0, j)),
            pl.BlockSpec((None, 1, tn), lambda l, j: (l, 0, j)),
        ],
        out_specs=pl.BlockSpec((None, r, tn), lambda l, j: (l, 0, j)),
        out_shape=jax.ShapeDtypeStruct((n_layers, r, d6), F32),
        compiler_params=_cparams("arbitrary", "arbitrary"),
        name="ada",
    )(c_all, ada_w, ada_b.reshape(n_layers, 1, d6))


def _mod_spec(mod, tiles_per_seq):
    _, rows, d = mod.shape
    return pl.BlockSpec((None, rows, d), lambda i, *_: (i // tiles_per_seq, 0, 0))


def _gmlp_kernel(x_ref, g_ref, sh_ref, sc_ref, gt_ref, wu_ref, wv_ref, vg_ref, vb_ref, ws_ref, bs_ref,
                 wo_ref, *rest, lc, period, groups, emit_v):
    if emit_v:
        o_ref, v_ref, vn_scr, out_scr = rest
    else:
        o_ref, vn_scr, out_scr = rest
    x = x_ref[...]
    tm = x.shape[0]
    h = _norm_mod(x, g_ref[...], sh_ref[...], sc_ref[...]).astype(BF16)
    v = _gelu(_dot(h, wv_ref[...]))
    mu = jnp.mean(v, axis=-1, keepdims=True)
    vc = v - mu
    var = jnp.mean(vc * vc, axis=-1, keepdims=True)
    vn = vc * lax.rsqrt(var + EPS) * vg_ref[...] + vb_ref[...]
    if emit_v:
        v_ref[...] = vn
    vn_scr[...] = vn.astype(BF16)
    gd = vn.shape[1] // groups
    r = lax.broadcasted_iota(jnp.int32, (lc, lc), 0)
    c = lax.broadcasted_iota(jnp.int32, (lc, lc), 1)
    mask = c <= r
    if period < lc:
        blk = ~(period - 1)
        mask = mask & ((r & blk) == (c & blk))
        ri = lax.broadcasted_iota(jnp.int32, (lc, LANES), 0)
        ci = lax.broadcasted_iota(jnp.int32, (lc, LANES), 1)
        sel = _ones_where((ri & (period - 1)) == ci)
    for g in range(groups):
        u = _gelu(_dot(h, wu_ref[:, g * gd:(g + 1) * gd]))
        if period < lc:
            rows_of_block = _dot(sel, ws_ref[g].astype(BF16)).astype(BF16)
            ws_full = _dot_nt(rows_of_block, sel)
        else:
            ws_full = ws_ref[g]
        wsm = jnp.where(mask, ws_full, 0.0).astype(BF16)
        bcol = bs_ref[:, g:g + 1]
        for ci in range(tm // lc):
            rows = slice(ci * lc, (ci + 1) * lc)
            mixed = _dot(wsm, vn_scr[rows, g * gd:(g + 1) * gd]) + bcol
            out_scr[rows, g * gd:(g + 1) * gd] = (u[rows] * mixed).astype(BF16)
    o_ref[...] = x + gt_ref[...] * _dot(out_scr[...], wo_ref[...])


def _gmlp_call(x, g, shift, scale, gate, wu, wv, vg, vb, ws, bs_t, wo, *, tm, tiles_per_seq, period, emit_v):
    n, d = x.shape
    da = wu.shape[1]
    groups = ws.shape[0]
    lc = bs_t.shape[0]
    const2 = lambda i: (0, 0)
    in_specs = [
        pl.BlockSpec((tm, d), lambda i: (i, 0)),
        pl.BlockSpec((1, d), const2),
        _mod_spec(shift, tiles_per_seq), _mod_spec(scale, tiles_per_seq), _mod_spec(gate, tiles_per_seq),
        pl.BlockSpec((d, da), const2), pl.BlockSpec((d, da), const2),
        pl.BlockSpec((1, da), const2), pl.BlockSpec((1, da), const2),
        pl.BlockSpec(ws.shape, lambda i: (0, 0, 0)),
        pl.BlockSpec((lc, groups), const2),
        pl.BlockSpec((da, d), const2),
    ]
    out_specs = [pl.BlockSpec((tm, d), lambda i: (i, 0))]
    out_shape = [jax.ShapeDtypeStruct((n, d), F32)]
    if emit_v:
        out_specs.append(pl.BlockSpec((tm, da), lambda i: (i, 0)))
        out_shape.append(jax.ShapeDtypeStruct((n, da), F32))
    return pl.pallas_call(
        functools.partial(_gmlp_kernel, lc=lc, period=period, groups=groups, emit_v=emit_v),
        grid=(n // tm,),
        in_specs=in_specs, out_specs=out_specs, out_shape=out_shape,
        scratch_shapes=[pltpu.VMEM((tm, da), BF16), pltpu.VMEM((tm, da), BF16)],
        compiler_params=_cparams("arbitrary"),
        name="gmlp",
    )(x, g, shift, scale, gate, wu, wv, vg, vb, ws, bs_t, wo)


def _ffn_kernel(x_ref, g_ref, sh_ref, sc_ref, gt_ref, wg_ref, wu_ref, wd_ref, o_ref, h_scr, acc_scr):
    j = pl.program_id(1)

    @pl.when(j == 0)
    def _():
        h_scr[...] = _norm_mod(x_ref[...], g_ref[...], sh_ref[...], sc_ref[...]).astype(BF16)
        acc_scr[...] = jnp.zeros_like(acc_scr)

    h = h_scr[...]
    a = _silu(_dot(h, wg_ref[...].astype(BF16))) * _dot(h, wu_ref[...].astype(BF16))
    acc_scr[...] += _dot(a.astype(BF16), wd_ref[...].astype(BF16))

    @pl.when(j == pl.num_programs(1) - 1)
    def _():
        o_ref[...] = x_ref[...] + gt_ref[...] * acc_scr[...]


def _ffn_call(x, g, shift, scale, gate, w_gu, w_down, *, tm, tiles_per_seq):
    n, d = x.shape
    f = w_down.shape[0]
    tf = min(TF_FFN, f)
    nf = f // tf
    const2 = lambda i, j: (0, 0)
    return pl.pallas_call(
        _ffn_kernel,
        grid=(n // tm, nf),
        in_specs=[
            pl.BlockSpec((tm, d), lambda i, j: (i, 0)),
            pl.BlockSpec((1, d), const2),
            _mod_spec(shift, tiles_per_seq), _mod_spec(scale, tiles_per_seq), _mod_spec(gate, tiles_per_seq),
            pl.BlockSpec((d, tf), lambda i, j: (0, j)),
            pl.BlockSpec((d, tf), lambda i, j: (0, nf + j)),
            pl.BlockSpec((tf, d), lambda i, j: (j, 0)),
        ],
        out_specs=pl.BlockSpec((tm, d), lambda i, j: (i, 0)),
        out_shape=jax.ShapeDtypeStruct((n, d), F32),
        scratch_shapes=[pltpu.VMEM((tm, d), BF16), pltpu.VMEM((tm, d), F32)],
        compiler_params=_cparams("arbitrary", "arbitrary"),
        name="ffn",
    )(x, g, shift, scale, gate, w_gu, w_gu, w_down)


def _log_sigmoid(x):
    return jnp.minimum(x, 0.0) - jnp.log(1.0 + jnp.exp(-jnp.abs(x)))


def _foxproj_kernel(x_ref, g_ref, sh_ref, sc_ref, w_ref, wf_ref, bf_ref, *rest, transposed, period,
                    tiles_per_seq, q_scale):
    if transposed:
        q_ref, k_ref, kb_ref, v_ref, vb_ref, lf_ref, cum_ref, carry_scr = rest
    else:
        q_ref, k_ref, v_ref, lf_ref, cum_ref, carry_scr = rest
    i = pl.program_id(0)
    x = x_ref[...]
    tm, d = x.shape
    h = _norm_mod(x, g_ref[...], sh_ref[...], sc_ref[...]).astype(BF16)
    if transposed:
        tq = q_ref.shape[-1]
        qt = (_dot_nt(w_ref[0:d, :], h) * q_scale).astype(BF16)
        for jq in range(tm // tq):
            q_ref[jq] = qt[:, jq * tq:(jq + 1) * tq]
        kt = _dot_nt(w_ref[d:2 * d, :], h)
        k_ref[...] = kt
        kb_ref[...] = kt.astype(BF16)
        vt = _dot_nt(w_ref[2 * d:3 * d, :], h)
        v_ref[...] = vt
        vb_ref[...] = vt.astype(BF16)
    else:
        q_ref[...] = _dot_nt(h, w_ref[0:d, :]) * q_scale
        k_ref[...] = _dot_nt(h, w_ref[d:2 * d, :])
        v_ref[...] = _dot_nt(h, w_ref[2 * d:3 * d, :])
    logf = _log_sigmoid(_dot_nt(wf_ref[...], h) + bf_ref[...])
    lf_ref[...] = logf
    s = lax.broadcasted_iota(jnp.int32, (tm, tm), 0)
    t = lax.broadcasted_iota(jnp.int32, (tm, tm), 1)
    upper = s <= t
    if period < tm:
        blk = ~(period - 1)
        upper = upper & ((s & blk) == (t & blk))
    upper = _ones_where(upper)
    hi, mid, lo = _split3(logf)
    cum = _dot(hi, upper) + _dot(mid, upper) + _dot(lo, upper)
    if tiles_per_seq > 1:
        @pl.when(i % tiles_per_seq == 0)
        def _():
            carry_scr[...] = jnp.zeros_like(carry_scr)
        cum = cum + carry_scr[:, 0:1]
        carry_scr[...] = jnp.broadcast_to(cum[:, tm - 1:tm], carry_scr.shape)
    cum_ref[...] = cum


def _foxproj_call(x, g, shift, scale, w_t, wf_t, b_f, *, tm, tiles_per_seq, transposed, period):
    n, d = x.shape
    heads = wf_t.shape[0]
    n_seq = n // (tm * tiles_per_seq)
    t_len = tm * tiles_per_seq
    const2 = lambda i: (0, 0)
    row = lambda i: (i, 0)
    seq_t = lambda i: (i // tiles_per_seq, 0, i % tiles_per_seq)
    in_specs = [
        pl.BlockSpec((tm, d), row),
        pl.BlockSpec((1, d), const2),
        _mod_spec(shift, tiles_per_seq), _mod_spec(scale, tiles_per_seq),
        pl.BlockSpec((3 * d, d), const2),
        pl.BlockSpec((heads, d), const2),
        pl.BlockSpec((heads, 1), const2),
    ]
    lf_spec = pl.BlockSpec((None, heads, tm), seq_t)
    lf_shape = jax.ShapeDtypeStruct((n_seq, heads, t_len), F32)
    if transposed:
        tq = min(TQ_ATTN, tm)
        kv_spec = pl.BlockSpec((None, d, tm), seq_t)
        q_spec = pl.BlockSpec((None, tm // tq, d, tq), lambda i: (i // tiles_per_seq, i % tiles_per_seq, 0, 0))
        out_specs = [q_spec, kv_spec, kv_spec, kv_spec, kv_spec, lf_spec, lf_spec]
        out_shape = [jax.ShapeDtypeStruct((n_seq, t_len // tq, d, tq), BF16),
                     jax.ShapeDtypeStruct((n_seq, d, t_len), F32), jax.ShapeDtypeStruct((n_seq, d, t_len), BF16),
                     jax.ShapeDtypeStruct((n_seq, d, t_len), F32), jax.ShapeDtypeStruct((n_seq, d, t_len), BF16),
                     lf_shape, lf_shape]
    else:
        out_specs = [pl.BlockSpec((tm, d), row)] * 3 + [lf_spec, lf_spec]
        out_shape = [jax.ShapeDtypeStruct((n, d), F32)] * 3 + [lf_shape, lf_shape]
    return pl.pallas_call(
        functools.partial(_foxproj_kernel, transposed=transposed, period=period, tiles_per_seq=tiles_per_seq,
                          q_scale=float(d // heads) ** -0.5 * (LOG2E if transposed else 1.0)),
        grid=(n // tm,),
        in_specs=in_specs, out_specs=out_specs, out_shape=out_shape,
        scratch_shapes=[pltpu.VMEM((heads, LANES), F32)],
        compiler_params=_cparams("arbitrary"),
        name="foxproj",
    )(x, g, shift, scale, w_t, wf_t, b_f)


def _attn_kernel(qt_ref, k_ref, vt_ref, cum_ref, o_ref, ck_scr, m_scr, l_scr, acc_scr, *, hd):
    nq, _, tq = qt_ref.shape
    dim = lax.broadcasted_iota(jnp.int32, (LANES, tq), 0)
    key = lax.broadcasted_iota(jnp.int32, (tq, tq), 0)
    qry = lax.broadcasted_iota(jnp.int32, (tq, tq), 1)
    eye = key == qry
    causal = key <= qry
    m_scr[...] = jnp.full_like(m_scr, NEG)
    l_scr[...] = jnp.zeros_like(l_scr)
    acc_scr[...] = jnp.zeros_like(acc_scr)

    def update(hh, qi, kh, vh, masked):
        t = lax.dot_general(kh, qt_ref[qi], (((0,), (0,)), ((), ())), preferred_element_type=F32) - ck_scr[...]
        if masked:
            t = jnp.where(causal, t, NEG)
        cq = cum_ref[hh, qi:qi + 1, :] * LOG2E
        m_old = m_scr[hh, qi]
        m_new = jnp.maximum(m_old, jnp.max(t, axis=0, keepdims=True) + cq)
        alpha = jnp.exp2(m_old - m_new)
        p = jnp.exp2(t + (cq - m_new))
        l_scr[hh, qi] = alpha * l_scr[hh, qi] + jnp.sum(p, axis=0, keepdims=True)
        acc_scr[hh, qi] = alpha * acc_scr[hh, qi] + _dot(vh, p.astype(BF16))
        m_scr[hh, qi] = m_new

    for ki in range(nq):
        keys = slice(ki * tq, (ki + 1) * tq)
        k_blk = k_ref[:, keys]
        for hh in range(2):
            kh = jnp.where((dim >= hh * hd) & (dim < (hh + 1) * hd), k_blk, jnp.zeros_like(k_blk))
            vh = vt_ref[hh * hd:(hh + 1) * hd, keys]
            col = jnp.sum(jnp.where(eye, cum_ref[hh, ki:ki + 1, :], 0.0), axis=1, keepdims=True)
            ck_scr[...] = jnp.broadcast_to(col * LOG2E, ck_scr.shape)
            update(hh, ki, kh, vh, True)
            for qi in range(ki + 1, nq):
                update(hh, qi, kh, vh, False)

    for qi in range(nq):
        ot = jnp.concatenate([acc_scr[0, qi] / l_scr[0, qi], acc_scr[1, qi] / l_scr[1, qi]], axis=0)
        o_ref[qi * tq:(qi + 1) * tq, :] = ot.T.astype(o_ref.dtype)


def _attn_call(qt, k, vt, cum5, *, t_len, heads):
    n_seq, nq, d, tq = qt.shape
    hd = d // heads
    assert 2 * hd == LANES
    return pl.pallas_call(
        functools.partial(_attn_kernel, hd=hd),
        grid=(n_seq, heads // 2),
        in_specs=[
            pl.BlockSpec((None, nq, LANES, tq), lambda b, p: (b, 0, p, 0)),
            pl.BlockSpec((None, LANES, t_len), lambda b, p: (b, p, 0)),
            pl.BlockSpec((None, LANES, t_len), lambda b, p: (b, p, 0)),
            pl.BlockSpec((None, None, 2, nq, tq), lambda b, p: (b, p, 0, 0, 0)),
        ],
        out_specs=pl.BlockSpec((t_len, LANES), lambda b, p: (b, p)),
        out_shape=jax.ShapeDtypeStruct((n_seq * t_len, d), BF16),
        scratch_shapes=[pltpu.VMEM((tq, tq), F32), pltpu.VMEM((2, nq, 1, tq), F32),
                        pltpu.VMEM((2, nq, 1, tq), F32), pltpu.VMEM((2, nq, hd, tq), F32)],
        compiler_params=_cparams("arbitrary", "arbitrary"),
        name="attn_prompt",
    )(qt, k, vt, cum5)


def _attn_sample_body(j, n_j, co_work, q_ref, cn_ref, kn_ref, vn_ref, *rest, pages, hd, page_size):
    k_refs = rest[0:pages]
    v_refs = rest[pages:2 * pages]
    lf_refs = rest[2 * pages:3 * pages]
    o_ref, qbd_scr, cn_scr, m_scr, l_scr, acc_scr, suf_scr = rest[3 * pages:]
    tn, d = q_ref.shape
    heads = d // hd
    rows = heads * tn
    row = lax.broadcasted_iota(jnp.int32, (rows, LANES), 0)
    lane = lax.broadcasted_iota(jnp.int32, (rows, LANES), 1)
    q_of_row = row & (tn - 1)

    def rep(a):
        return jnp.broadcast_to(a[:, None, :], (heads, tn, a.shape[-1])).reshape(rows, a.shape[-1])

    gw = qbd_scr.shape[1]
    gr = (gw // hd) * tn
    n_groups = d // gw
    g_rows = lambda g: slice(g * gr, (g + 1) * gr)
    g_cols = lambda g: slice(g * gw, (g + 1) * gw)
    rr = lax.broadcasted_iota(jnp.int32, (gr, gw), 0)
    cc = lax.broadcasted_iota(jnp.int32, (gr, gw), 1)
    own_head = _shift_div(rr, tn) == _shift_div(cc, hd)

    @pl.when(j == 0)
    def _():
        q = q_ref[...]
        for g in range(n_groups):
            qrep = jnp.broadcast_to(q[None, :, g_cols(g)], (gw // hd, tn, gw)).reshape(gr, gw)
            qbd_scr[g_rows(g), :] = jnp.where(own_head, qrep, 0.0).astype(BF16)
        cn_scr[...] = jnp.sum(jnp.where(lane == q_of_row, rep(cn_ref[...]), 0.0), axis=1, keepdims=True)
        m_scr[...] = jnp.full_like(m_scr, NEG)
        l_scr[...] = jnp.zeros_like(l_scr)
        acc_scr[...] = jnp.zeros_like(acc_scr)
        suf_scr[...] = jnp.zeros_like(suf_scr)

    co_work()

    s_idx = lax.broadcasted_iota(jnp.int32, (page_size, page_size), 0)
    t_idx = lax.broadcasted_iota(jnp.int32, (page_size, page_size), 1)
    later = _ones_where(s_idx > t_idx)
    lf_all = jnp.concatenate([lf_refs[i][...] for i in range(pages)], axis=0)
    hi, mid, lo = _split3(lf_all)
    suf_all = _dot(hi, later) + _dot(mid, later) + _dot(lo, later)
    tot_all = suf_all[:, 0:1] + lf_all[:, 0:1]
    carry = suf_scr[:, 0:1]
    sufs = [None] * pages
    for i in reversed(range(pages)):
        sufs[i] = suf_all[i * heads:(i + 1) * heads, :] + carry
        carry = carry + tot_all[i * heads:(i + 1) * heads, :]
    suf_scr[...] = jnp.broadcast_to(carry, suf_scr.shape)

    def page_pair(refs, i, g):
        return jnp.concatenate([refs[i][g_cols(g), :].astype(BF16), refs[i + 1][g_cols(g), :].astype(BF16)], axis=1)

    scores = []
    for i in range(0, pages, 2):
        qk = [_dot(qbd_scr[g_rows(g), :], page_pair(k_refs, i, g)) for g in range(n_groups)]
        scores.append(jnp.concatenate(qk, axis=0) + rep(jnp.concatenate([sufs[i], sufs[i + 1]], axis=1)))
    s = jnp.concatenate(scores, axis=1) + cn_scr[...]
    m_old = m_scr[...]
    m_new = jnp.maximum(m_old, jnp.max(s, axis=1, keepdims=True))
    alpha = jnp.exp(m_old - m_new)
    p = jnp.exp(s - m_new)
    l_scr[...] = alpha * l_scr[...] + jnp.sum(p, axis=1, keepdims=True)
    pb = p.astype(BF16)
    for g in range(n_groups):
        pv = _dot_nt(pb[g_rows(g), 0:2 * page_size], page_pair(v_refs, 0, g))
        for i in range(2, pages, 2):
            pv = pv + _dot_nt(pb[g_rows(g), i * page_size:(i + 2) * page_size], page_pair(v_refs, i, g))
        acc_scr[g_rows(g), :] = alpha[g_rows(g), :] * acc_scr[g_rows(g), :] + pv
    m_scr[...] = m_new

    @pl.when(j == n_j - 1)
    def _():
        pad = jnp.zeros((LANES - tn, d), BF16)
        k_new = jnp.concatenate([kn_ref[...].astype(BF16), pad], axis=0)
        v_new = jnp.concatenate([vn_ref[...].astype(BF16), pad], axis=0)
        qk = [_dot_nt(qbd_scr[g_rows(g), :], k_new[:, g_cols(g)]) for g in range(n_groups)]
        s = jnp.concatenate(qk, axis=0) + cn_scr[...] - rep(cn_ref[...])
        s = jnp.where(lane <= q_of_row, s, NEG)
        m_old = m_scr[...]
        m_new = jnp.maximum(m_old, jnp.max(s, axis=1, keepdims=True))
        alpha = jnp.exp(m_old - m_new)
        pb = jnp.exp(s - m_new)
        l_new = alpha * l_scr[...] + jnp.sum(pb, axis=1, keepdims=True)
        pb = pb.astype(BF16)
        for g in range(n_groups):
            o = alpha[g_rows(g), :] * acc_scr[g_rows(g), :] + _dot(pb[g_rows(g), :], v_new[:, g_cols(g)])
            o = jnp.where(own_head, o / l_new[g_rows(g), :], 0.0)
            o_ref[:, g_cols(g)] = jnp.sum(o.reshape(gw // hd, tn, gw), axis=0)


N_GMLP_IN = 12


def _gmlp_attn_kernel(pt_ref, *refs, n_steps, pages, hd, page_size, lc, groups):
    n_attn_in = 4 + 3 * pages
    gmlp_in = refs[:N_GMLP_IN]
    attn_in = refs[N_GMLP_IN:N_GMLP_IN + n_attn_in]
    x1_ref, o_ref, vn_scr, out_scr, *attn_scr = refs[N_GMLP_IN + n_attn_in:]
    i = pl.program_id(0)
    mixer = functools.partial(_gmlp_kernel, *gmlp_in, x1_ref, vn_scr, out_scr, lc=lc, period=lc, groups=groups,
                              emit_v=False)
    _attn_sample_body(i % n_steps, n_steps, mixer, *attn_in, o_ref, *attn_scr,
                      pages=pages, hd=hd, page_size=page_size)


def _gmlp_attn_call(x, g, shift, scale, gate, wu, wv, vg, vb, ws, bs_t, wo,
                    page_table, q, cn_pad, k_new, v_new, kc_t, vc_t, lfc_t, *, t_len, tn, heads):
    n, d = x.shape
    da = wu.shape[1]
    groups = ws.shape[0]
    lc = bs_t.shape[0]
    n_s = q.shape[0]
    n_seq, n_pages = page_table.shape
    page_size = kc_t.shape[-1]
    pages = min(PAGES_PER_STEP, n_pages)
    n_steps = n_pages // pages
    hd = d // heads
    n_grid = n_seq * n_steps
    tm = n // n_grid
    assert tm * n_grid == n and tm % lc == 0 and t_len % tm == 0, (n, n_grid, lc)
    tiles_per_seq = t_len // tm
    once = pl.Buffered(1)
    const2 = lambda i, pt: (0, 0)
    in_specs = [
        pl.BlockSpec((tm, d), lambda i, pt: (i, 0)),
        pl.BlockSpec((1, d), const2),
        _mod_spec(shift, tiles_per_seq), _mod_spec(scale, tiles_per_seq), _mod_spec(gate, tiles_per_seq),
        pl.BlockSpec((d, da), const2, pipeline_mode=once), pl.BlockSpec((d, da), const2, pipeline_mode=once),
        pl.BlockSpec((1, da), const2), pl.BlockSpec((1, da), const2),
        pl.BlockSpec(ws.shape, lambda i, pt: (0, 0, 0)),
        pl.BlockSpec((lc, groups), const2),
        pl.BlockSpec((da, d), const2, pipeline_mode=once),
    ]
    assert len(in_specs) == N_GMLP_IN

    def page_map(ip):
        def index(i, pt):
            return (pt[(i // n_steps) * n_pages + (n_steps - 1 - i % n_steps) * pages + ip], 0, 0)
        return index

    seq_row = lambda i, pt: (i // n_steps, 0)
    in_specs += [
        pl.BlockSpec((tn, d), seq_row),
        pl.BlockSpec((None, heads, LANES), lambda i, pt: (i // n_steps, 0, 0)),
        pl.BlockSpec((tn, d), seq_row), pl.BlockSpec((tn, d), seq_row),
    ]
    in_specs += [pl.BlockSpec((None, d, page_size), page_map(ip)) for ip in range(pages)]
    in_specs += [pl.BlockSpec((None, d, page_size), page_map(ip)) for ip in range(pages)]
    in_specs += [pl.BlockSpec((None, heads, page_size), page_map(ip)) for ip in range(pages)]
    rows = heads * tn
    return pl.pallas_call(
        functools.partial(_gmlp_attn_kernel, n_steps=n_steps, pages=pages, hd=hd, page_size=page_size,
                          lc=lc, groups=groups),
        grid_spec=pltpu.PrefetchScalarGridSpec(
            num_scalar_prefetch=1,
            grid=(n_grid,),
            in_specs=in_specs,
            out_specs=[pl.BlockSpec((tm, d), lambda i, pt: (i, 0)), pl.BlockSpec((tn, d), seq_row)],
            scratch_shapes=[pltpu.VMEM((tm, da), BF16), pltpu.VMEM((tm, da), BF16),
                            pltpu.VMEM((rows, MXU_TILE), BF16), pltpu.VMEM((rows, 1), F32),
                            pltpu.VMEM((rows, 1), F32), pltpu.VMEM((rows, 1), F32),
                            pltpu.VMEM((rows, MXU_TILE), F32), pltpu.VMEM((heads, LANES), F32)],
        ),
        out_shape=[jax.ShapeDtypeStruct((n, d), F32), jax.ShapeDtypeStruct((n_s, d), F32)],
        compiler_params=_cparams("arbitrary"),
        name="gmlp_attn_sample",
    )(page_table.reshape(-1), x, g, shift, scale, gate, wu, wv, vg, vb, ws, bs_t, wo,
      q, cn_pad, k_new, v_new, *([kc_t] * pages), *([vc_t] * pages), *([lfc_t] * pages))


def _post_kernel(op_ref, os_ref, wo_ref, xp_ref, xs_ref, gtp_ref, gts_ref, g_ref, shp_ref, shs_ref, scp_ref,
                 scs_ref, wr_ref, br_ref, x3_ref, h_ref, meta_ref, metat_ref, blk_ref, cout_ref, carry_scr,
                 *, n_experts, n_prompt_blocks):
    i = pl.program_id(0)
    tm = xp_ref.shape[0]
    is_p = i < n_prompt_blocks

    @pl.when(i == 0)
    def _():
        carry_scr[...] = jnp.zeros_like(carry_scr)

    o = jnp.where(is_p, op_ref[...], os_ref[...].astype(BF16))
    x = jnp.where(is_p, xp_ref[...], xs_ref[...])
    gate = jnp.where(is_p, gtp_ref[...], gts_ref[...])
    shift = jnp.where(is_p, shp_ref[...], shs_ref[...])
    scale = jnp.where(is_p, scp_ref[...], scs_ref[...])
    x3 = x + gate * _dot(o, wo_ref[...])
    x3_ref[...] = x3
    h = _norm_mod(x3, g_ref[...], shift, scale)
    h_hi = h.astype(BF16)
    h_ref[...] = h_hi
    h_lo = (h - h_hi.astype(F32)).astype(BF16)
    wr = wr_ref[...]
    w_hi = wr.astype(BF16)
    w_lo = (wr - w_hi.astype(F32)).astype(BF16)
    logits = _dot(h_hi, w_hi) + _dot(h_hi, w_lo) + _dot(h_lo, w_hi) + br_ref[...]
    lane_i = lax.broadcasted_iota(jnp.int32, (tm, LANES), 1)
    lane = lane_i.astype(F32)
    logits = jnp.where(lane_i < n_experts, logits, NEG)
    l1 = jnp.max(logits, axis=1, keepdims=True)
    i1 = jnp.min(jnp.where(logits == l1, lane, float(LANES)), axis=1, keepdims=True)
    rest = jnp.where(lane == i1, NEG, logits)
    l2 = jnp.max(rest, axis=1, keepdims=True)
    i2 = jnp.min(jnp.where(rest == l2, lane, float(LANES)), axis=1, keepdims=True)
    e = jnp.exp(l2 - l1)
    g1 = 1.0 / (1.0 + e)
    g2 = e / (1.0 + e)
    onehot = jnp.where((lane == i1) | (lane == i2), 1.0, 0.0)
    r = lax.broadcasted_iota(jnp.int32, (tm, tm), 0)
    c = lax.broadcasted_iota(jnp.int32, (tm, tm), 1)
    before = _ones_where(c < r)
    carry = carry_scr[0:1, :]
    blk_ref[...] = carry
    prefix = _dot(before, onehot.astype(BF16)) + carry
    r1 = jnp.sum(jnp.where(lane == i1, prefix, 0.0), axis=1, keepdims=True)
    r2 = jnp.sum(jnp.where(lane == i2, prefix, 0.0), axis=1, keepdims=True)
    cols = (i1, i2, g1, g2, r1, r2)
    meta = jnp.zeros((tm, LANES), F32)
    for k, col in enumerate(cols):
        meta = jnp.where(lane_i == k, col, meta)
    meta_ref[...] = meta
    metat_ref[...] = meta.T[0:SUBLANES, :]
    carry = carry + jnp.sum(onehot, axis=0, keepdims=True)
    carry_scr[...] = jnp.broadcast_to(carry, carry_scr.shape)
    cout_ref[...] = carry


def _post_call(o_p, o_s, wo, x_p, x_s, gate_p, gate_s, g, shift_p, shift_s, scale_p, scale_s, wr_pad, br_pad,
               *, tm, blocks_per_seq, n_experts):
    n_p, d = x_p.shape
    n_s = x_s.shape[0]
    npb = n_p // tm
    nb = npb + n_s // tm
    n = n_p + n_s
    const2 = lambda i: (0, 0)
    row = lambda i: (i, 0)
    p_row = lambda i: (jnp.minimum(i, npb - 1), 0)
    s_row = lambda i: (jnp.maximum(i - npb, 0), 0)
    p_mod = pl.BlockSpec((None, 1, d), lambda i: (jnp.minimum(i, npb - 1) // blocks_per_seq, 0, 0))
    s_mod = pl.BlockSpec((None, tm, d), lambda i: (jnp.maximum(i - npb, 0), 0, 0))
    return pl.pallas_call(
        functools.partial(_post_kernel, n_experts=n_experts, n_prompt_blocks=npb),
        grid=(nb,),
        in_specs=[
            pl.BlockSpec((tm, d), p_row), pl.BlockSpec((tm, d), s_row),
            pl.BlockSpec((d, d), const2),
            pl.BlockSpec((tm, d), p_row), pl.BlockSpec((tm, d), s_row),
            p_mod, s_mod,
            pl.BlockSpec((1, d), const2),
            p_mod, s_mod, p_mod, s_mod,
            pl.BlockSpec((d, LANES), const2),
            pl.BlockSpec((1, LANES), const2),
        ],
        out_specs=[
            pl.BlockSpec((tm, d), row), pl.BlockSpec((tm, d), row), pl.BlockSpec((tm, LANES), row),
            pl.BlockSpec((None, SUBLANES, tm), lambda i: (i, 0, 0)),
            pl.BlockSpec((None, 1, LANES), lambda i: (i, 0, 0)),
            pl.BlockSpec((1, LANES), const2),
        ],
        out_shape=[
            jax.ShapeDtypeStruct((n, d), F32), jax.ShapeDtypeStruct((n, d), BF16),
            jax.ShapeDtypeStruct((n, LANES), F32),
            jax.ShapeDtypeStruct((nb, SUBLANES, tm), F32),
            jax.ShapeDtypeStruct((nb, 1, LANES), F32),
            jax.ShapeDtypeStruct((1, LANES), F32),
        ],
        scratch_shapes=[pltpu.VMEM((SUBLANES, LANES), F32)],
        compiler_params=_cparams("arbitrary"),
        name="post_attn_router",
    )(o_p, o_s, wo, x_p, x_s, gate_p, gate_s, g, shift_p, shift_s, scale_p, scale_s, wr_pad, br_pad)


def _sorted_pos(idx, rank, start_ref, n_experts):
    pos = rank
    for e in range(n_experts):
        pos = pos + jnp.where(idx == float(e), start_ref[e].astype(F32), 0.0)
    return pos


def _dispatch_kernel(lo_ref, hi_ref, start_ref, mt_ref, h_ref, xs_ref, gs_ref, acc_scr, gacc_scr,
                     *, tb, n_experts):
    j = pl.program_id(0)
    sub = xs_ref.shape[0]
    nb = mt_ref.shape[0]
    lo = lo_ref[j]
    hi = hi_ref[j]

    @pl.when(hi <= lo)
    def _():
        xs_ref[...] = jnp.zeros_like(xs_ref)
        gs_ref[...] = jnp.zeros_like(gs_ref)

    def hits(b, width):
        dest = (j * sub + lax.broadcasted_iota(jnp.int32, (sub, width * tb), 0)).astype(F32)
        mts = [mt_ref[b + w] for w in range(width)]
        field = lambda r: jnp.concatenate([m[r:r + 1, :] for m in mts], axis=1)
        hit0 = _sorted_pos(field(0), field(4), start_ref, n_experts) == dest
        hit1 = _sorted_pos(field(1), field(5), start_ref, n_experts) == dest
        gate = jnp.where(hit0, field(2), 0.0) + jnp.where(hit1, field(3), 0.0)
        return _ones_where(hit0 | hit1), gate

    n_even = nb - nb % 2

    def pair(m, carry):
        b = 2 * m
        oh, gate = hits(b, 2)
        rows = pl.ds(pl.multiple_of(b * tb, 2 * tb), 2 * tb)
        acc_scr[...] += _dot(oh, h_ref[rows, :])
        gacc_scr[...] += gate
        return carry

    @pl.when(hi > lo)
    def _():
        acc_scr[...] = jnp.zeros_like(acc_scr)
        gacc_scr[...] = jnp.zeros_like(gacc_scr)
        lax.fori_loop(lo // 2, (jnp.minimum(hi, n_even) + 1) // 2, pair, 0)

        if nb % 2:
            @pl.when(hi == nb)
            def _():
                oh, gate = hits(nb - 1, 1)
                acc_scr[...] += _dot(oh, h_ref[(nb - 1) * tb:nb * tb, :])
                gacc_scr[:, 0:tb] += gate

        xs_ref[...] = acc_scr[...].astype(BF16)
        gs_ref[...] = jnp.broadcast_to(jnp.sum(gacc_scr[...], axis=1, keepdims=True), gs_ref.shape)


def _dispatch_call(tile_lo, tile_hi, start, meta_t, h, *, n_sorted, n_experts):
    n, d = h.shape
    nb, _, tb = meta_t.shape
    whole = lambda nd: (lambda j, lo, hi, st: (0,) * nd)
    once = pl.Buffered(1)
    return pl.pallas_call(
        functools.partial(_dispatch_kernel, tb=tb, n_experts=n_experts),
        grid_spec=pltpu.PrefetchScalarGridSpec(
            num_scalar_prefetch=3,
            grid=(n_sorted // SUB,),
            in_specs=[
                pl.BlockSpec((nb, SUBLANES, tb), whole(3), pipeline_mode=once),
                pl.BlockSpec((n, d), whole(2), pipeline_mode=once),
            ],
            out_specs=[pl.BlockSpec((SUB, d), lambda j, lo, hi, st: (j, 0)),
                       pl.BlockSpec((SUB, LANES), lambda j, lo, hi, st: (j, 0))],
            scratch_shapes=[pltpu.VMEM((SUB, d), F32), pltpu.VMEM((SUB, 2 * tb), F32)],
        ),
        out_shape=[jax.ShapeDtypeStruct((n_sorted, d), BF16), jax.ShapeDtypeStruct((n_sorted, LANES), F32)],
        compiler_params=_cparams("arbitrary"),
        name="moe_dispatch",
    )(tile_lo, tile_hi, start, meta_t, h)


def _moe_kernel(ex_ref, nsub_ref, x_ref, gs_ref, wg_ref, wu_ref, wd_ref, y_ref, acc_scr):
    s = pl.program_id(0)
    j = pl.program_id(1)
    nsub = nsub_ref[s]
    last = j == pl.num_programs(1) - 1

    n_full = (nsub * SUB) // FFN_ROWS
    odd = nsub * SUB - n_full * FFN_ROWS > 0

    def tile_rows(t):
        return pl.ds(pl.multiple_of(t * SUB, SUB), SUB)

    def step_rows(t):
        return pl.ds(pl.multiple_of(t * FFN_ROWS, FFN_ROWS), FFN_ROWS)

    @pl.when(j == 0)
    def _():
        def zero(t, carry):
            acc_scr[tile_rows(t), :] = jnp.zeros((SUB, acc_scr.shape[1]), F32)
            return carry
        lax.fori_loop(0, nsub, zero, 0)

    def ffn(rows):
        x = x_ref[rows, :]
        a = _silu(_dot(x, wg_ref[...].astype(BF16))) * _dot(x, wu_ref[...].astype(BF16))
        acc_scr[rows, :] += _dot(a.astype(BF16), wd_ref[...].astype(BF16))

    def body(t, carry):
        ffn(step_rows(2 * t))
        ffn(step_rows(2 * t + 1))
        return carry

    lax.fori_loop(0, n_full // 2, body, 0)

    @pl.when(n_full % 2 == 1)
    def _():
        ffn(step_rows(n_full - 1))

    @pl.when(odd)
    def _():
        ffn(tile_rows(nsub - 1))

    @pl.when(last)
    def _():
        def emit(t, carry):
            rows = tile_rows(t)
            y_ref[rows, :] = (acc_scr[rows, :] * gs_ref[rows, 0:1]).astype(BF16)
            return carry
        lax.fori_loop(0, nsub, emit, 0)

        def zero(t, carry):
            y_ref[tile_rows(t), :] = jnp.zeros((SUB, y_ref.shape[1]), BF16)
            return carry
        lax.fori_loop(nsub, y_ref.shape[0] // SUB, zero, 0)


def _moe_call(sup_expert, sup_nsub, xs, gs, w_gu, w_down, *, ts):
    n_sorted, d = xs.shape
    f = w_down.shape[1]
    tf = min(TF_FFN, f)
    nf = f // tf
    n_super = n_sorted // ts

    def jj(s, j, nsub):
        return jnp.where(nsub[s] > 0, j, nf - 1)

    return pl.pallas_call(
        _moe_kernel,
        grid_spec=pltpu.PrefetchScalarGridSpec(
            num_scalar_prefetch=2,
            grid=(n_super, nf),
            in_specs=[
                pl.BlockSpec((ts, d), lambda s, j, ex, ns: (s, 0)),
                pl.BlockSpec((ts, LANES), lambda s, j, ex, ns: (s, 0)),
                pl.BlockSpec((None, d, tf), lambda s, j, ex, ns: (ex[s], 0, jj(s, j, ns))),
                pl.BlockSpec((None, d, tf), lambda s, j, ex, ns: (ex[s], 0, nf + jj(s, j, ns))),
                pl.BlockSpec((None, tf, d), lambda s, j, ex, ns: (ex[s], jj(s, j, ns), 0)),
            ],
            out_specs=pl.BlockSpec((ts, d), lambda s, j, ex, ns: (s, 0)),
            scratch_shapes=[pltpu.VMEM((ts, d), F32)],
        ),
        out_shape=jax.ShapeDtypeStruct((n_sorted, d), BF16),
        compiler_params=_cparams("arbitrary", "arbitrary"),
        name="moe_ffn",
    )(sup_expert, sup_nsub, xs, gs, w_gu, w_gu, w_down)


def _combine_kernel(tile_ref, need2_ref, start_ref, meta_ref, x_ref, gtp_ref, gts_ref, gf_ref, *rest,
                    n_experts, n_prompt_blocks):
    y_refs = rest[0:2 * n_experts]
    op_ref, os_ref, acc_scr = rest[2 * n_experts:]
    b = pl.program_id(0)
    tb = acc_scr.shape[0]
    meta = meta_ref[...]
    col = lax.broadcasted_iota(jnp.int32, (tb, SUB), 1)

    def window(e, w):
        first = start_ref[e].astype(F32)
        p0 = jnp.where(meta[:, 0:1] == float(e), meta[:, 4:5] + first, -1.0)
        p1 = jnp.where(meta[:, 1:2] == float(e), meta[:, 5:6] + first, -1.0)
        src = ((tile_ref[b * n_experts + e] + w) * SUB + col).astype(F32)
        return _ones_where((p0 == src) | (p1 == src))

    total = _dot(window(0, 0), y_refs[0][...])
    for e in range(1, n_experts):
        total = total + _dot(window(e, 0), y_refs[2 * e][...])
    acc_scr[...] = total
    for e in range(n_experts):
        @pl.when(need2_ref[b * n_experts + e] > 0)
        def _(e=e):
            acc_scr[...] += _dot(window(e, 1), y_refs[2 * e + 1][...])

    is_p = b < n_prompt_blocks
    x4 = x_ref[...] + jnp.where(is_p, gtp_ref[...], gts_ref[...]) * acc_scr[...]
    ms = jnp.mean(x4 * x4, axis=-1, keepdims=True)
    y = x4 * lax.rsqrt(ms + EPS) * gf_ref[...]

    @pl.when(is_p)
    def _():
        op_ref[...] = y

    @pl.when(jnp.logical_not(is_p))
    def _():
        os_ref[...] = y


def _combine_call(win_tile, win_need2, start, meta, x, gate_p, gate_s, g_final, ys, *, n_p, tb, blocks_per_seq,
                  n_experts):
    n, d = x.shape
    n_s = n - n_p
    npb = n_p // tb
    nb = n // tb

    def pb(b):
        return jnp.minimum(b, npb - 1)

    def sb(b):
        return jnp.maximum(b - npb, 0)

    def win_map(e, w):
        if w == 0:
            return lambda b, tile, need, st: (tile[b * n_experts + e], 0)
        return lambda b, tile, need, st: (jnp.where(need[b * n_experts + e] > 0, tile[b * n_experts + e] + 1, 0), 0)

    in_specs = [
        pl.BlockSpec((tb, LANES), lambda b, tile, need, st: (b, 0)),
        pl.BlockSpec((tb, d), lambda b, tile, need, st: (b, 0)),
        pl.BlockSpec((None, 1, d), lambda b, tile, need, st: (pb(b) // blocks_per_seq, 0, 0)),
        pl.BlockSpec((None, tb, d), lambda b, tile, need, st: (sb(b), 0, 0)),
        pl.BlockSpec((1, d), lambda b, tile, need, st: (0, 0)),
    ]
    in_specs += [pl.BlockSpec((SUB, d), win_map(e, w)) for e in range(n_experts) for w in range(2)]
    return pl.pallas_call(
        functools.partial(_combine_kernel, n_experts=n_experts, n_prompt_blocks=npb),
        grid_spec=pltpu.PrefetchScalarGridSpec(
            num_scalar_prefetch=3,
            grid=(nb,),
            in_specs=in_specs,
            out_specs=[pl.BlockSpec((tb, d), lambda b, tile, need, st: (pb(b), 0)),
                       pl.BlockSpec((tb, d), lambda b, tile, need, st: (sb(b), 0))],
            scratch_shapes=[pltpu.VMEM((tb, d), F32)],
        ),
        out_shape=[jax.ShapeDtypeStruct((n_p, d), F32), jax.ShapeDtypeStruct((n_s, d), F32)],
        compiler_params=_cparams("arbitrary"),
        name="moe_combine",
    )(win_tile, win_need2, start, meta, x, gate_p, gate_s, g_final, *([ys] * (2 * n_experts)))


def _routing_tables(blk_cnt, counts, *, n_experts, ts, n_super):
    nb = blk_cnt.shape[0]
    cnt = counts[0, :n_experts].astype(jnp.int32)
    n_sup_e = (cnt + ts - 1) // ts
    sup_start = jnp.cumsum(n_sup_e) - n_sup_e
    start = sup_start * ts

    s_ids = jnp.arange(n_super, dtype=jnp.int32)
    used = jnp.sum(n_sup_e)
    sup_end = sup_start + n_sup_e
    sup_e = jnp.sum((s_ids[:, None] >= sup_end[None, :]).astype(jnp.int32), axis=1)
    sup_e = jnp.clip(sup_e, 0, n_experts - 1)
    last_e = jnp.max(jnp.where(cnt > 0, jnp.arange(n_experts, dtype=jnp.int32), 0))
    sup_e = jnp.where(s_ids < used, sup_e, last_e)
    rows_in = jnp.clip(cnt[sup_e] - (s_ids - sup_start[sup_e]) * ts, 0, ts)
    sup_nsub = jnp.where(s_ids < used, (rows_in + SUB - 1) // SUB, 0).astype(jnp.int32)

    blk = blk_cnt[:, 0, :n_experts].astype(jnp.int32)
    blk_end = jnp.concatenate([blk[1:], cnt[None, :]], axis=0)
    n_tiles = n_super * (ts // SUB)
    t_ids = jnp.arange(n_tiles, dtype=jnp.int32)
    t_e = sup_e[t_ids // (ts // SUB)]
    t_r0 = t_ids * SUB - start[t_e]
    t_active = ((t_ids // (ts // SUB)) < used) & (t_r0 < cnt[t_e]) & (t_r0 >= 0)
    t_r1 = jnp.minimum(t_r0 + SUB, cnt[t_e])
    be = blk_end[:, t_e]
    bs = blk[:, t_e]
    overlap = (be > t_r0[None, :]) & (bs < t_r1[None, :])
    b_ids = jnp.arange(nb, dtype=jnp.int32)[:, None]
    lo = jnp.min(jnp.where(overlap, b_ids, nb), axis=0)
    hi = jnp.max(jnp.where(overlap, b_ids + 1, 0), axis=0)
    tile_lo = jnp.where(t_active, lo, 0).astype(jnp.int32)
    tile_hi = jnp.where(t_active, hi, 0).astype(jnp.int32)

    first = start[None, :] + blk
    n_be = blk_end - blk
    win_tile = (first // SUB).astype(jnp.int32)
    win_need2 = ((first + n_be) > (win_tile + 1) * SUB).astype(jnp.int32)

    return dict(start=start.astype(jnp.int32), sup_e=sup_e, sup_nsub=sup_nsub, tile_lo=tile_lo,
                tile_hi=tile_hi, win_tile=win_tile.reshape(-1), win_need2=win_need2.reshape(-1))


def kernel(x_prompt, x_sample, c_prompt, c_sample, cache_k, cache_v, cache_logf, page_table, norm_mix_g, norm_ffn_g, final_norm_g, ada_w, ada_b, gmlp_w_in, gmlp_v_g, gmlp_v_b, gmlp_w_s, gmlp_b_s, gmlp_w_out, fox_w_in, fox_b_f, fox_w_out, ffn_w_gu, ffn_w_down, moe_w_r, moe_b_r, moe_w_gu, moe_w_down):
    n_seq_p, t_len, d = x_prompt.shape
    n_seq_s, t_new, _ = x_sample.shape
    n_p = n_seq_p * t_len
    n_s = n_seq_s * t_new
    heads = N_HEADS
    hd = d // heads
    n_experts = moe_w_r.shape[-1]
    da = gmlp_w_out.shape[1]

    xp = x_prompt.reshape(n_p, d)
    xs = x_sample.reshape(n_s, d)

    mod = _ada_call(jnp.concatenate([c_prompt, c_sample], axis=0), ada_w, ada_b)

    def mods(layer):
        mp = [mod[layer, :n_seq_p, c * d:(c + 1) * d].reshape(n_seq_p, 1, d) for c in range(6)]
        ms = [jnp.repeat(mod[layer, n_seq_p:, c * d:(c + 1) * d], t_new, axis=0).reshape(1, n_s, d)
              for c in range(6)]
        return mp, ms

    row = lambda a: a.reshape(1, -1)

    mp, ms = mods(0)
    w_in = gmlp_w_in[0].astype(BF16)
    wu, wv = w_in[:, :da], w_in[:, da:]
    wo = gmlp_w_out[0].astype(BF16)
    vg, vb = row(gmlp_v_g[0]), row(gmlp_v_b[0])
    lc = min(GMLP_CHUNK, t_len)
    mp1, ms1 = mods(1)
    w_t = fox_w_in[0].T.astype(BF16)
    wqkv_t, wf_t = w_t[:3 * d], w_t[3 * d:]
    b_f = fox_b_f[0].reshape(heads, 1)

    reps = n_s // t_new
    ws_s = jnp.zeros((gmlp_w_s.shape[1], LANES, LANES), F32).at[:, :t_new, :t_new].set(
        gmlp_w_s[0][:, :t_new, :t_new])
    bs_s = jnp.tile(gmlp_b_s[0][:, :t_new], (1, reps)).T
    xs, v_rows = _gmlp_call(xs, row(norm_mix_g[0]), ms[0], ms[1], ms[2], wu, wv, vg, vb, ws_s, bs_s, wo,
                            tm=n_s, tiles_per_seq=1, period=t_new, emit_v=True)
    xs = _ffn_call(xs, row(norm_ffn_g[0]), ms[3], ms[4], ms[5], ffn_w_gu[0], ffn_w_down[0],
                   tm=n_s, tiles_per_seq=1)
    q_s, k_s, v_s, lft_s, cumt_s = _foxproj_call(
        xs, row(norm_mix_g[1]), ms1[0], ms1[1], wqkv_t, wf_t, b_f,
        tm=n_s, tiles_per_seq=1, transposed=False, period=t_new)

    page_size = cache_k.shape[2]
    n_phys = cache_k.shape[1]
    kc_t = jnp.transpose(cache_k[0], (0, 2, 3, 1)).reshape(n_phys, d, page_size)
    vc_t = jnp.transpose(cache_v[0], (0, 2, 3, 1)).reshape(n_phys, d, page_size)
    lfc_t = jnp.transpose(cache_logf[0], (0, 2, 1))
    cn = cumt_s[0].reshape(heads, n_seq_s, t_new).transpose(1, 0, 2)
    cn_pad = jnp.zeros((n_seq_s, heads, LANES), F32).at[:, :, :t_new].set(cn)
    xp, o_s = _gmlp_attn_call(xp, row(norm_mix_g[0]), mp[0], mp[1], mp[2], wu, wv, vg, vb,
                              gmlp_w_s[0][:, :lc, :lc], gmlp_b_s[0][:, :lc].T, wo,
                              page_table, q_s, cn_pad, k_s, v_s, kc_t, vc_t, lfc_t,
                              t_len=t_len, tn=t_new, heads=heads)
    tm = min(TM_FFN, t_len)
    xp = _ffn_call(xp, row(norm_ffn_g[0]), mp[3], mp[4], mp[5], ffn_w_gu[0], ffn_w_down[0],
                   tm=tm, tiles_per_seq=t_len // tm)

    mp, ms = mp1, ms1
    tm = min(TM_PROJ, t_len)
    qt_p, kt_p, kb_p, vt_p, vtb_p, lft_p, cumt_p = _foxproj_call(
        xp, row(norm_mix_g[1]), mp[0], mp[1], wqkv_t, wf_t, b_f,
        tm=tm, tiles_per_seq=t_len // tm, transposed=True, period=tm)
    nq, tq = qt_p.shape[1], qt_p.shape[3]
    o_p = _attn_call(qt_p, kb_p, vtb_p, cumt_p.reshape(n_seq_p, heads // 2, 2, nq, tq), t_len=t_len, heads=heads)

    wo_f = fox_w_out[0].astype(BF16)
    wr_pad = jnp.zeros((d, LANES), F32).at[:, :n_experts].set(moe_w_r[0])
    br_pad = jnp.zeros((1, LANES), F32).at[0, :n_experts].set(moe_b_r[0])
    tb = min(TM_POST, t_len)
    ms_blk = [m.reshape(n_s // tb, tb, d) for m in ms]
    x3, h_all, meta, meta_t, blk_cnt, counts = _post_call(
        o_p, o_s, wo_f, xp, xs, mp[2], ms_blk[2], row(norm_ffn_g[1]), mp[3], ms_blk[3], mp[4], ms_blk[4],
        wr_pad, br_pad, tm=tb, blocks_per_seq=t_len // tb, n_experts=n_experts)

    n_tok = n_p + n_s
    ts = TS_MOE
    n_super = (TOP_K * n_tok) // ts + n_experts + 1
    rt = _routing_tables(blk_cnt, counts, n_experts=n_experts, ts=ts, n_super=n_super)
    xs_sorted, g_sorted = _dispatch_call(rt["tile_lo"], rt["tile_hi"], rt["start"], meta_t, h_all,
                                         n_sorted=n_super * ts, n_experts=n_experts)
    ys_sorted = _moe_call(rt["sup_e"], rt["sup_nsub"], xs_sorted, g_sorted, moe_w_gu[0], moe_w_down[0], ts=ts)
    y_p, y_s = _combine_call(rt["win_tile"], rt["win_need2"], rt["start"], meta, x3, mp[5], ms_blk[5],
                             row(final_norm_g), ys_sorted,
                             n_p=n_p, tb=tb, blocks_per_seq=t_len // tb, n_experts=n_experts)

    y_prompt = y_p.reshape(n_seq_p, t_len, d)
    y_sample = y_s.reshape(n_seq_s, t_new, d)
    state_a_v_sample = v_rows.reshape(1, n_seq_s, t_new, da)
    k_prompt = kt_p.reshape(1, n_seq_p, heads, hd, t_len).transpose(0, 1, 4, 2, 3)
    v_prompt = vt_p.reshape(1, n_seq_p, heads, hd, t_len).transpose(0, 1, 4, 2, 3)
    logf_prompt = lft_p.transpose(0, 2, 1)[None]
    k_sample = k_s.reshape(1, n_seq_s, t_new, heads, hd)
    v_sample = v_s.reshape(1, n_seq_s, t_new, heads, hd)
    logf_sample = lft_s[0].T.reshape(1, n_seq_s, t_new, heads)
    return (y_prompt, y_sample, state_a_v_sample, k_prompt, v_prompt, logf_prompt, k_sample, v_sample, logf_sample)
```

```python
import functools

import jax
import jax.numpy as jnp
from jax import lax
from jax.experimental import pallas as pl
from jax.experimental.pallas import tpu as pltpu

F32 = jnp.float32
BF16 = jnp.bfloat16

N_HEADS = 16
GMLP_GROUPS = 8
GMLP_CHUNK = 128
TOP_K = 2
EPS = 1e-6
NEG = -1e30
LOG2E = 1.4426950408889634

LANES = 128
SUBLANES = 8
MXU_TILE = 256
VMEM_LIMIT = 56 * 1024 * 1024

TM_FFN = 1024
TF_FFN = 512
TM_PROJ = 512
TQ_ATTN = 512
TM_POST = 256
SUB = 256
FFN_ROWS = 2 * SUB
TS_MOE = 2560
PAGES_PER_STEP = 16


def _cparams(*sem):
    return pltpu.CompilerParams(dimension_semantics=sem, vmem_limit_bytes=VMEM_LIMIT)


def _dot(a, b):
    return jnp.dot(a, b, preferred_element_type=F32)


def _dot_nt(a, b):
    return lax.dot_general(a, b, (((1,), (1,)), ((), ())), preferred_element_type=F32)


def _norm_mod(x, g, shift, scale):
    ms = jnp.mean(x * x, axis=-1, keepdims=True)
    y = x * lax.rsqrt(ms + EPS) * g
    return y * (1.0 + scale) + shift


def _gelu(x):
    return 0.5 * x * (1.0 + lax.erf(x * (2.0 ** -0.5)))


def _silu(x):
    return x * jax.nn.sigmoid(x)


def _ones_where(cond):
    return jnp.where(cond, 1.0, 0.0).astype(BF16)


def _shift_div(x, c):
    assert c & (c - 1) == 0
    return lax.shift_right_logical(x, c.bit_length() - 1)


def _split3(x):
    hi = x.astype(BF16)
    r = x - hi.astype(F32)
    mid = r.astype(BF16)
    lo = (r - mid.astype(F32)).astype(BF16)
    return hi, mid, lo


def _ada_kernel(c_ref, w_ref, b_ref, o_ref):
    s = _silu(c_ref[...]).astype(BF16)
    o_ref[...] = _dot(s, w_ref[...].astype(BF16)) + b_ref[...]


def _ada_call(c_all, ada_w, ada_b):
    n_layers, d, d6 = ada_w.shape
    r = c_all.shape[0]
    tn = min(d6, 1536)
    return pl.pallas_call(
        _ada_kernel,
        grid=(n_layers, d6 // tn),
        in_specs=[
            pl.BlockSpec((r, d), lambda l, j: (0, 0)),
            pl.BlockSpec((None, d, tn), lambda l, j: (l, 0, j)),
            pl.BlockSpec((None, 1, tn), lambda l, j: (l, 0, j)),
        ],
        out_specs=pl.BlockSpec((None, r, tn), lambda l, j: (l, 0, j)),
        out_shape=jax.ShapeDtypeStruct((n_layers, r, d6), F32),
        compiler_params=_cparams("arbitrary", "arbitrary"),
        name="ada",
    )(c_all, ada_w, ada_b.reshape(n_layers, 1, d6))


def _mod_spec(mod, tiles_per_seq):
    _, rows, d = mod.shape
    return pl.BlockSpec((None, rows, d), lambda i, *_: (i // tiles_per_seq, 0, 0))


def _gmlp_kernel(x_ref, g_ref, sh_ref, sc_ref, gt_ref, wu_ref, wv_ref, vg_ref, vb_ref, ws_ref, bs_ref,
                 wo_ref, *rest, lc, period, groups, emit_v):
    if emit_v:
        o_ref, v_ref, vn_scr, out_scr = rest
    else:
        o_ref, vn_scr, out_scr = rest
    x = x_ref[...]
    tm = x.shape[0]
    h = _norm_mod(x, g_ref[...], sh_ref[...], sc_ref[...]).astype(BF16)
    v = _gelu(_dot(h, wv_ref[...]))
    mu = jnp.mean(v, axis=-1, keepdims=True)
    vc = v - mu
    var = jnp.mean(vc * vc, axis=-1, keepdims=True)
    vn = vc * lax.rsqrt(var + EPS) * vg_ref[...] + vb_ref[...]
    if emit_v:
        v_ref[...] = vn
    vn_scr[...] = vn.astype(BF16)
    gd = vn.shape[1] // groups
    r = lax.broadcasted_iota(jnp.int32, (lc, lc), 0)
    c = lax.broadcasted_iota(jnp.int32, (lc, lc), 1)
    mask = c <= r
    if period < lc:
        blk = ~(period - 1)
        mask = mask & ((r & blk) == (c & blk))
        ri = lax.broadcasted_iota(jnp.int32, (lc, LANES), 0)
        ci = lax.broadcasted_iota(jnp.int32, (lc, LANES), 1)
        sel = _ones_where((ri & (period - 1)) == ci)
    for g in range(groups):
        u = _gelu(_dot(h, wu_ref[:, g * gd:(g + 1) * gd]))
        if period < lc:
            rows_of_block = _dot(sel, ws_ref[g].astype(BF16)).astype(BF16)
            ws_full = _dot_nt(rows_of_block, sel)
        else:
            ws_full = ws_ref[g]
        wsm = jnp.where(mask, ws_full, 0.0).astype(BF16)
        bcol = bs_ref[:, g:g + 1]
        for ci in range(tm // lc):
            rows = slice(ci * lc, (ci + 1) * lc)
            mixed = _dot(wsm, vn_scr[rows, g * gd:(g + 1) * gd]) + bcol
            out_scr[rows, g * gd:(g + 1) * gd] = (u[rows] * mixed).astype(BF16)
    o_ref[...] = x + gt_ref[...] * _dot(out_scr[...], wo_ref[...])


def _gmlp_call(x, g, shift, scale, gate, wu, wv, vg, vb, ws, bs_t, wo, *, tm, tiles_per_seq, period, emit_v):
    n, d = x.shape
    da = wu.shape[1]
    groups = ws.shape[0]
    lc = bs_t.shape[0]
    const2 = lambda i: (0, 0)
    in_specs = [
        pl.BlockSpec((tm, d), lambda i: (i, 0)),
        pl.BlockSpec((1, d), const2),
        _mod_spec(shift, tiles_per_seq), _mod_spec(scale, tiles_per_seq), _mod_spec(gate, tiles_per_seq),
        pl.BlockSpec((d, da), const2), pl.BlockSpec((d, da), const2),
        pl.BlockSpec((1, da), const2), pl.BlockSpec((1, da), const2),
        pl.BlockSpec(ws.shape, lambda i: (0, 0, 0)),
        pl.BlockSpec((lc, groups), const2),
        pl.BlockSpec((da, d), const2),
    ]
    out_specs = [pl.BlockSpec((tm, d), lambda i: (i, 0))]
    out_shape = [jax.ShapeDtypeStruct((n, d), F32)]
    if emit_v:
        out_specs.append(pl.BlockSpec((tm, da), lambda i: (i, 0)))
        out_shape.append(jax.ShapeDtypeStruct((n, da), F32))
    return pl.pallas_call(
        functools.partial(_gmlp_kernel, lc=lc, period=period, groups=groups, emit_v=emit_v),
        grid=(n // tm,),
        in_specs=in_specs, out_specs=out_specs, out_shape=out_shape,
        scratch_shapes=[pltpu.VMEM((tm, da), BF16), pltpu.VMEM((tm, da), BF16)],
        compiler_params=_cparams("arbitrary"),
        name="gmlp",
    )(x, g, shift, scale, gate, wu, wv, vg, vb, ws, bs_t, wo)


def _ffn_kernel(x_ref, g_ref, sh_ref, sc_ref, gt_ref, wg_ref, wu_ref, wd_ref, o_ref, h_scr, acc_scr):
    j = pl.program_id(1)

    @pl.when(j == 0)
    def _():
        h_scr[...] = _norm_mod(x_ref[...], g_ref[...], sh_ref[...], sc_ref[...]).astype(BF16)
        acc_scr[...] = jnp.zeros_like(acc_scr)

    h = h_scr[...]
    a = _silu(_dot(h, wg_ref[...].astype(BF16))) * _dot(h, wu_ref[...].astype(BF16))
    acc_scr[...] += _dot(a.astype(BF16), wd_ref[...].astype(BF16))

    @pl.when(j == pl.num_programs(1) - 1)
    def _():
        o_ref[...] = x_ref[...] + gt_ref[...] * acc_scr[...]


def _ffn_call(x, g, shift, scale, gate, w_gu, w_down, *, tm, tiles_per_seq):
    n, d = x.shape
    f = w_down.shape[0]
    tf = min(TF_FFN, f)
    nf = f // tf
    const2 = lambda i, j: (0, 0)
    return pl.pallas_call(
        _ffn_kernel,
        grid=(n // tm, nf),
        in_specs=[
            pl.BlockSpec((tm, d), lambda i, j: (i, 0)),
            pl.BlockSpec((1, d), const2),
            _mod_spec(shift, tiles_per_seq), _mod_spec(scale, tiles_per_seq), _mod_spec(gate, tiles_per_seq),
            pl.BlockSpec((d, tf), lambda i, j: (0, j)),
            pl.BlockSpec((d, tf), lambda i, j: (0, nf + j)),
            pl.BlockSpec((tf, d), lambda i, j: (j, 0)),
        ],
        out_specs=pl.BlockSpec((tm, d), lambda i, j: (i, 0)),
        out_shape=jax.ShapeDtypeStruct((n, d), F32),
        scratch_shapes=[pltpu.VMEM((tm, d), BF16), pltpu.VMEM((tm, d), F32)],
        compiler_params=_cparams("arbitrary", "arbitrary"),
        name="ffn",
    )(x, g, shift, scale, gate, w_gu, w_gu, w_down)


def _log_sigmoid(x):
    return jnp.minimum(x, 0.0) - jnp.log(1.0 + jnp.exp(-jnp.abs(x)))


def _foxproj_kernel(x_ref, g_ref, sh_ref, sc_ref, w_ref, wf_ref, bf_ref, *rest, transposed, period,
                    tiles_per_seq, q_scale):
    if transposed:
        q_ref, k_ref, kb_ref, v_ref, vb_ref, lf_ref, cum_ref, carry_scr = rest
    else:
        q_ref, k_ref, v_ref, lf_ref, cum_ref, carry_scr = rest
    i = pl.program_id(0)
    x = x_ref[...]
    tm, d = x.shape
    h = _norm_mod(x, g_ref[...], sh_ref[...], sc_ref[...]).astype(BF16)
    if transposed:
        tq = q_ref.shape[-1]
        qt = (_dot_nt(w_ref[0:d, :], h) * q_scale).astype(BF16)
        for jq in range(tm // tq):
            q_ref[jq] = qt[:, jq * tq:(jq + 1) * tq]
        kt = _dot_nt(w_ref[d:2 * d, :], h)
        k_ref[...] = kt
        kb_ref[...] = kt.astype(BF16)
        vt = _dot_nt(w_ref[2 * d:3 * d, :], h)
        v_ref[...] = vt
        vb_ref[...] = vt.astype(BF16)
    else:
        q_ref[...] = _dot_nt(h, w_ref[0:d, :]) * q_scale
        k_ref[...] = _dot_nt(h, w_ref[d:2 * d, :])
        v_ref[...] = _dot_nt(h, w_ref[2 * d:3 * d, :])
    logf = _log_sigmoid(_dot_nt(wf_ref[...], h) + bf_ref[...])
    lf_ref[...] = logf
    s = lax.broadcasted_iota(jnp.int32, (tm, tm), 0)
    t = lax.broadcasted_iota(jnp.int32, (tm, tm), 1)
    upper = s <= t
    if period < tm:
        blk = ~(period - 1)
        upper = upper & ((s & blk) == (t & blk))
    upper = _ones_where(upper)
    hi, mid, lo = _split3(logf)
    cum = _dot(hi, upper) + _dot(mid, upper) + _dot(lo, upper)
    if tiles_per_seq > 1:
        @pl.when(i % tiles_per_seq == 0)
        def _():
            carry_scr[...] = jnp.zeros_like(carry_scr)
        cum = cum + carry_scr[:, 0:1]
        carry_scr[...] = jnp.broadcast_to(cum[:, tm - 1:tm], carry_scr.shape)
    cum_ref[...] = cum


def _foxproj_call(x, g, shift, scale, w_t, wf_t, b_f, *, tm, tiles_per_seq, transposed, period):
    n, d = x.shape
    heads = wf_t.shape[0]
    n_seq = n // (tm * tiles_per_seq)
    t_len = tm * tiles_per_seq
    const2 = lambda i: (0, 0)
    row = lambda i: (i, 0)
    seq_t = lambda i: (i // tiles_per_seq, 0, i % tiles_per_seq)
    in_specs = [
        pl.BlockSpec((tm, d), row),
        pl.BlockSpec((1, d), const2),
        _mod_spec(shift, tiles_per_seq), _mod_spec(scale, tiles_per_seq),
        pl.BlockSpec((3 * d, d), const2),
        pl.BlockSpec((heads, d), const2),
        pl.BlockSpec((heads, 1), const2),
    ]
    lf_spec = pl.BlockSpec((None, heads, tm), seq_t)
    lf_shape = jax.ShapeDtypeStruct((n_seq, heads, t_len), F32)
    if transposed:
        tq = min(TQ_ATTN, tm)
        kv_spec = pl.BlockSpec((None, d, tm), seq_t)
        q_spec = pl.BlockSpec((None, tm // tq, d, tq), lambda i: (i // tiles_per_seq, i % tiles_per_seq, 0, 0))
        out_specs = [q_spec, kv_spec, kv_spec, kv_spec, kv_spec, lf_spec, lf_spec]
        out_shape = [jax.ShapeDtypeStruct((n_seq, t_len // tq, d, tq), BF16),
                     jax.ShapeDtypeStruct((n_seq, d, t_len), F32), jax.ShapeDtypeStruct((n_seq, d, t_len), BF16),
                     jax.ShapeDtypeStruct((n_seq, d, t_len), F32), jax.ShapeDtypeStruct((n_seq, d, t_len), BF16),
                     lf_shape, lf_shape]
    else:
        out_specs = [pl.BlockSpec((tm, d), row)] * 3 + [lf_spec, lf_spec]
        out_shape = [jax.ShapeDtypeStruct((n, d), F32)] * 3 + [lf_shape, lf_shape]
    return pl.pallas_call(
        functools.partial(_foxproj_kernel, transposed=transposed, period=period, tiles_per_seq=tiles_per_seq,
                          q_scale=float(d // heads) ** -0.5 * (LOG2E if transposed else 1.0)),
        grid=(n // tm,),
        in_specs=in_specs, out_specs=out_specs, out_shape=out_shape,
        scratch_shapes=[pltpu.VMEM((heads, LANES), F32)],
        compiler_params=_cparams("arbitrary"),
        name="foxproj",
    )(x, g, shift, scale, w_t, wf_t, b_f)


def _attn_kernel(qt_ref, k_ref, vt_ref, cum_ref, o_ref, ck_scr, m_scr, l_scr, acc_scr, *, hd):
    nq, _, tq = qt_ref.shape
    dim = lax.broadcasted_iota(jnp.int32, (LANES, tq), 0)
    key = lax.broadcasted_iota(jnp.int32, (tq, tq), 0)
    qry = lax.broadcasted_iota(jnp.int32, (tq, tq), 1)
    eye = key == qry
    causal = key <= qry
    m_scr[...] = jnp.full_like(m_scr, NEG)
    l_scr[...] = jnp.zeros_like(l_scr)
    acc_scr[...] = jnp.zeros_like(acc_scr)

    def update(hh, qi, kh, vh, masked):
        t = lax.dot_general(kh, qt_ref[qi], (((0,), (0,)), ((), ())), preferred_element_type=F32) - ck_scr[...]
        if masked:
            t = jnp.where(causal, t, NEG)
        cq = cum_ref[hh, qi:qi + 1, :] * LOG2E
        m_old = m_scr[hh, qi]
        m_new = jnp.maximum(m_old, jnp.max(t, axis=0, keepdims=True) + cq)
        alpha = jnp.exp2(m_old - m_new)
        p = jnp.exp2(t + (cq - m_new))
        l_scr[hh, qi] = alpha * l_scr[hh, qi] + jnp.sum(p, axis=0, keepdims=True)
        acc_scr[hh, qi] = alpha * acc_scr[hh, qi] + _dot(vh, p.astype(BF16))
        m_scr[hh, qi] = m_new

    for ki in range(nq):
        keys = slice(ki * tq, (ki + 1) * tq)
        k_blk = k_ref[:, keys]
        for hh in range(2):
            kh = jnp.where((dim >= hh * hd) & (dim < (hh + 1) * hd), k_blk, jnp.zeros_like(k_blk))
            vh = vt_ref[hh * hd:(hh + 1) * hd, keys]
            col = jnp.sum(jnp.where(eye, cum_ref[hh, ki:ki + 1, :], 0.0), axis=1, keepdims=True)
            ck_scr[...] = jnp.broadcast_to(col * LOG2E, ck_scr.shape)
            update(hh, ki, kh, vh, True)
            for qi in range(ki + 1, nq):
                update(hh, qi, kh, vh, False)

    for qi in range(nq):
        ot = jnp.concatenate([acc_scr[0, qi] / l_scr[0, qi], acc_scr[1, qi] / l_scr[1, qi]], axis=0)
        o_ref[qi * tq:(qi + 1) * tq, :] = ot.T.astype(o_ref.dtype)


def _attn_call(qt, k, vt, cum5, *, t_len, heads):
    n_seq, nq, d, tq = qt.shape
    hd = d // heads
    assert 2 * hd == LANES
    return pl.pallas_call(
        functools.partial(_attn_kernel, hd=hd),
        grid=(n_seq, heads // 2),
        in_specs=[
            pl.BlockSpec((None, nq, LANES, tq), lambda b, p: (b, 0, p, 0)),
            pl.BlockSpec((None, LANES, t_len), lambda b, p: (b, p, 0)),
            pl.BlockSpec((None, LANES, t_len), lambda b, p: (b, p, 0)),
            pl.BlockSpec((None, None, 2, nq, tq), lambda b, p: (b, p, 0, 0, 0)),
        ],
        out_specs=pl.BlockSpec((t_len, LANES), lambda b, p: (b, p)),
        out_shape=jax.ShapeDtypeStruct((n_seq * t_len, d), BF16),
        scratch_shapes=[pltpu.VMEM((tq, tq), F32), pltpu.VMEM((2, nq, 1, tq), F32),
                        pltpu.VMEM((2, nq, 1, tq), F32), pltpu.VMEM((2, nq, hd, tq), F32)],
        compiler_params=_cparams("arbitrary", "arbitrary"),
        name="attn_prompt",
    )(qt, k, vt, cum5)


def _attn_sample_body(j, n_j, co_work, q_ref, cn_ref, kn_ref, vn_ref, *rest, pages, hd, page_size):
    k_refs = rest[0:pages]
    v_refs = rest[pages:2 * pages]
    lf_refs = rest[2 * pages:3 * pages]
    o_ref, qbd_scr, cn_scr, m_scr, l_scr, acc_scr, suf_scr = rest[3 * pages:]
    tn, d = q_ref.shape
    heads = d // hd
    rows = heads * tn
    row = lax.broadcasted_iota(jnp.int32, (rows, LANES), 0)
    lane = lax.broadcasted_iota(jnp.int32, (rows, LANES), 1)
    q_of_row = row & (tn - 1)

    def rep(a):
        return jnp.broadcast_to(a[:, None, :], (heads, tn, a.shape[-1])).reshape(rows, a.shape[-1])

    gw = qbd_scr.shape[1]
    gr = (gw // hd) * tn
    n_groups = d // gw
    g_rows = lambda g: slice(g * gr, (g + 1) * gr)
    g_cols = lambda g: slice(g * gw, (g + 1) * gw)
    rr = lax.broadcasted_iota(jnp.int32, (gr, gw), 0)
    cc = lax.broadcasted_iota(jnp.int32, (gr, gw), 1)
    own_head = _shift_div(rr, tn) == _shift_div(cc, hd)

    @pl.when(j == 0)
    def _():
        q = q_ref[...]
        for g in range(n_groups):
            qrep = jnp.broadcast_to(q[None, :, g_cols(g)], (gw // hd, tn, gw)).reshape(gr, gw)
            qbd_scr[g_rows(g), :] = jnp.where(own_head, qrep, 0.0).astype(BF16)
        cn_scr[...] = jnp.sum(jnp.where(lane == q_of_row, rep(cn_ref[...]), 0.0), axis=1, keepdims=True)
        m_scr[...] = jnp.full_like(m_scr, NEG)
        l_scr[...] = jnp.zeros_like(l_scr)
        acc_scr[...] = jnp.zeros_like(acc_scr)
        suf_scr[...] = jnp.zeros_like(suf_scr)

    co_work()

    s_idx = lax.broadcasted_iota(jnp.int32, (page_size, page_size), 0)
    t_idx = lax.broadcasted_iota(jnp.int32, (page_size, page_size), 1)
    later = _ones_where(s_idx > t_idx)
    lf_all = jnp.concatenate([lf_refs[i][...] for i in range(pages)], axis=0)
    hi, mid, lo = _split3(lf_all)
    suf_all = _dot(hi, later) + _dot(mid, later) + _dot(lo, later)
    tot_all = suf_all[:, 0:1] + lf_all[:, 0:1]
    carry = suf_scr[:, 0:1]
    sufs = [None] * pages
    for i in reversed(range(pages)):
        sufs[i] = suf_all[i * heads:(i + 1) * heads, :] + carry
        carry = carry + tot_all[i * heads:(i + 1) * heads, :]
    suf_scr[...] = jnp.broadcast_to(carry, suf_scr.shape)

    def page_pair(refs, i, g):
        return jnp.concatenate([refs[i][g_cols(g), :].astype(BF16), refs[i + 1][g_cols(g), :].astype(BF16)], axis=1)

    scores = []
    for i in range(0, pages, 2):
        qk = [_dot(qbd_scr[g_rows(g), :], page_pair(k_refs, i, g)) for g in range(n_groups)]
        scores.append(jnp.concatenate(qk, axis=0) + rep(jnp.concatenate([sufs[i], sufs[i + 1]], axis=1)))
    s = jnp.concatenate(scores, axis=1) + cn_scr[...]
    m_old = m_scr[...]
    m_new = jnp.maximum(m_old, jnp.max(s, axis=1, keepdims=True))
    alpha = jnp.exp(m_old - m_new)
    p = jnp.exp(s - m_new)
    l_scr[...] = alpha * l_scr[...] + jnp.sum(p, axis=1, keepdims=True)
    pb = p.astype(BF16)
    for g in range(n_groups):
        pv = _dot_nt(pb[g_rows(g), 0:2 * page_size], page_pair(v_refs, 0, g))
        for i in range(2, pages, 2):
            pv = pv + _dot_nt(pb[g_rows(g), i * page_size:(i + 2) * page_size], page_pair(v_refs, i, g))
        acc_scr[g_rows(g), :] = alpha[g_rows(g), :] * acc_scr[g_rows(g), :] + pv
    m_scr[...] = m_new

    @pl.when(j == n_j - 1)
    def _():
        pad = jnp.zeros((LANES - tn, d), BF16)
        k_new = jnp.concatenate([kn_ref[...].astype(BF16), pad], axis=0)
        v_new = jnp.concatenate([vn_ref[...].astype(BF16), pad], axis=0)
        qk = [_dot_nt(qbd_scr[g_rows(g), :], k_new[:, g_cols(g)]) for g in range(n_groups)]
        s = jnp.concatenate(qk, axis=0) + cn_scr[...] - rep(cn_ref[...])
        s = jnp.where(lane <= q_of_row, s, NEG)
        m_old = m_scr[...]
        m_new = jnp.maximum(m_old, jnp.max(s, axis=1, keepdims=True))
        alpha = jnp.exp(m_old - m_new)
        pb = jnp.exp(s - m_new)
        l_new = alpha * l_scr[...] + jnp.sum(pb, axis=1, keepdims=True)
        pb = pb.astype(BF16)
        for g in range(n_groups):
            o = alpha[g_rows(g), :] * acc_scr[g_rows(g), :] + _dot(pb[g_rows(g), :], v_new[:, g_cols(g)])
            o = jnp.where(own_head, o / l_new[g_rows(g), :], 0.0)
            o_ref[:, g_cols(g)] = jnp.sum(o.reshape(gw // hd, tn, gw), axis=0)


N_GMLP_IN = 12


def _gmlp_attn_kernel(pt_ref, *refs, n_steps, pages, hd, page_size, lc, groups):
    n_attn_in = 4 + 3 * pages
    gmlp_in = refs[:N_GMLP_IN]
    attn_in = refs[N_GMLP_IN:N_GMLP_IN + n_attn_in]
    x1_ref, o_ref, vn_scr, out_scr, *attn_scr = refs[N_GMLP_IN + n_attn_in:]
    i = pl.program_id(0)
    mixer = functools.partial(_gmlp_kernel, *gmlp_in, x1_ref, vn_scr, out_scr, lc=lc, period=lc, groups=groups,
                              emit_v=False)
    _attn_sample_body(i % n_steps, n_steps, mixer, *attn_in, o_ref, *attn_scr,
                      pages=pages, hd=hd, page_size=page_size)


def _gmlp_attn_call(x, g, shift, scale, gate, wu, wv, vg, vb, ws, bs_t, wo,
                    page_table, q, cn_pad, k_new, v_new, kc_t, vc_t, lfc_t, *, t_len, tn, heads):
    n, d = x.shape
    da = wu.shape[1]
    groups = ws.shape[0]
    lc = bs_t.shape[0]
    n_s = q.shape[0]
    n_seq, n_pages = page_table.shape
    page_size = kc_t.shape[-1]
    pages = min(PAGES_PER_STEP, n_pages)
    n_steps = n_pages // pages
    hd = d // heads
    n_grid = n_seq * n_steps
    tm = n // n_grid
    assert tm * n_grid == n and tm % lc == 0 and t_len % tm == 0, (n, n_grid, lc)
    tiles_per_seq = t_len // tm
    once = pl.Buffered(1)
    const2 = lambda i, pt: (0, 0)
    in_specs = [
        pl.BlockSpec((tm, d), lambda i, pt: (i, 0)),
        pl.BlockSpec((1, d), const2),
        _mod_spec(shift, tiles_per_seq), _mod_spec(scale, tiles_per_seq), _mod_spec(gate, tiles_per_seq),
        pl.BlockSpec((d, da), const2, pipeline_mode=once), pl.BlockSpec((d, da), const2, pipeline_mode=once),
        pl.BlockSpec((1, da), const2), pl.BlockSpec((1, da), const2),
        pl.BlockSpec(ws.shape, lambda i, pt: (0, 0, 0)),
        pl.BlockSpec((lc, groups), const2),
        pl.BlockSpec((da, d), const2, pipeline_mode=once),
    ]
    assert len(in_specs) == N_GMLP_IN

    def page_map(ip):
        def index(i, pt):
            return (pt[(i // n_steps) * n_pages + (n_steps - 1 - i % n_steps) * pages + ip], 0, 0)
        return index

    seq_row = lambda i, pt: (i // n_steps, 0)
    in_specs += [
        pl.BlockSpec((tn, d), seq_row),
        pl.BlockSpec((None, heads, LANES), lambda i, pt: (i // n_steps, 0, 0)),
        pl.BlockSpec((tn, d), seq_row), pl.BlockSpec((tn, d), seq_row),
    ]
    in_specs += [pl.BlockSpec((None, d, page_size), page_map(ip)) for ip in range(pages)]
    in_specs += [pl.BlockSpec((None, d, page_size), page_map(ip)) for ip in range(pages)]
    in_specs += [pl.BlockSpec((None, heads, page_size), page_map(ip)) for ip in range(pages)]
    rows = heads * tn
    return pl.pallas_call(
        functools.partial(_gmlp_attn_kernel, n_steps=n_steps, pages=pages, hd=hd, page_size=page_size,
                          lc=lc, groups=groups),
        grid_spec=pltpu.PrefetchScalarGridSpec(
            num_scalar_prefetch=1,
            grid=(n_grid,),
            in_specs=in_specs,
            out_specs=[pl.BlockSpec((tm, d), lambda i, pt: (i, 0)), pl.BlockSpec((tn, d), seq_row)],
            scratch_shapes=[pltpu.VMEM((tm, da), BF16), pltpu.VMEM((tm, da), BF16),
                            pltpu.VMEM((rows, MXU_TILE), BF16), pltpu.VMEM((rows, 1), F32),
                            pltpu.VMEM((rows, 1), F32), pltpu.VMEM((rows, 1), F32),
                            pltpu.VMEM((rows, MXU_TILE), F32), pltpu.VMEM((heads, LANES), F32)],
        ),
        out_shape=[jax.ShapeDtypeStruct((n, d), F32), jax.ShapeDtypeStruct((n_s, d), F32)],
        compiler_params=_cparams("arbitrary"),
        name="gmlp_attn_sample",
    )(page_table.reshape(-1), x, g, shift, scale, gate, wu, wv, vg, vb, ws, bs_t, wo,
      q, cn_pad, k_new, v_new, *([kc_t] * pages), *([vc_t] * pages), *([lfc_t] * pages))


def _post_kernel(op_ref, os_ref, wo_ref, xp_ref, xs_ref, gtp_ref, gts_ref, g_ref, shp_ref, shs_ref, scp_ref,
                 scs_ref, wr_ref, br_ref, x3_ref, h_ref, meta_ref, metat_ref, blk_ref, cout_ref, carry_scr,
                 *, n_experts, n_prompt_blocks):
    i = pl.program_id(0)
    is_p = i < n_prompt_blocks
    route = functools.partial(_route_block, wo_ref=wo_ref, g_ref=g_ref, wr_ref=wr_ref, br_ref=br_ref,
                              x3_ref=x3_ref, h_ref=h_ref, meta_ref=meta_ref, metat_ref=metat_ref,
                              blk_ref=blk_ref, cout_ref=cout_ref, carry_scr=carry_scr, n_experts=n_experts)

    @pl.when(i == 0)
    def _():
        carry_scr[...] = jnp.zeros_like(carry_scr)

    @pl.when(is_p)
    def _():
        route(op_ref[...], xp_ref[...], gtp_ref[...], shp_ref[...], scp_ref[...])

    @pl.when(jnp.logical_not(is_p))
    def _():
        route(os_ref[...].astype(BF16), xs_ref[...], gts_ref[...], shs_ref[...], scs_ref[...])


def _route_block(o, x, gate, shift, scale, wo_ref, g_ref, wr_ref, br_ref, x3_ref, h_ref, meta_ref, metat_ref,
                 blk_ref, cout_ref, carry_scr, *, n_experts):
    tm = x.shape[0]
    x3 = x + gate * _dot(o, wo_ref[...])
    x3_ref[...] = x3
    h = _norm_mod(x3, g_ref[...], shift, scale)
    h_hi = h.astype(BF16)
    h_ref[...] = h_hi
    h_lo = (h - h_hi.astype(F32)).astype(BF16)
    wr = wr_ref[...]
    w_hi = wr.astype(BF16)
    w_lo = (wr - w_hi.astype(F32)).astype(BF16)
    logits = _dot(h_hi, w_hi) + _dot(h_hi, w_lo) + _dot(h_lo, w_hi) + br_ref[...]
    lane_i = lax.broadcasted_iota(jnp.int32, (tm, LANES), 1)
    lane = lane_i.astype(F32)
    logits = jnp.where(lane_i < n_experts, logits, NEG)
    l1 = jnp.max(logits, axis=1, keepdims=True)
    i1 = jnp.min(jnp.where(logits == l1, lane, float(LANES)), axis=1, keepdims=True)
    rest = jnp.where(lane == i1, NEG, logits)
    l2 = jnp.max(rest, axis=1, keepdims=True)
    i2 = jnp.min(jnp.where(rest == l2, lane, float(LANES)), axis=1, keepdims=True)
    e = jnp.exp(l2 - l1)
    g1 = 1.0 / (1.0 + e)
    g2 = e / (1.0 + e)
    onehot = jnp.where((lane == i1) | (lane == i2), 1.0, 0.0)
    r = lax.broadcasted_iota(jnp.int32, (tm, tm), 0)
    c = lax.broadcasted_iota(jnp.int32, (tm, tm), 1)
    before = _ones_where(c < r)
    carry = carry_scr[0:1, :]
    blk_ref[...] = carry
    prefix = _dot(before, onehot.astype(BF16)) + carry
    r1 = jnp.sum(jnp.where(lane == i1, prefix, 0.0), axis=1, keepdims=True)
    r2 = jnp.sum(jnp.where(lane == i2, prefix, 0.0), axis=1, keepdims=True)
    cols = (i1, i2, g1, g2, r1, r2)
    meta = jnp.zeros((tm, LANES), F32)
    for k, col in enumerate(cols):
        meta = jnp.where(lane_i == k, col, meta)
    meta_ref[...] = meta
    metat_ref[...] = meta.T[0:SUBLANES, :]
    carry = carry + jnp.sum(onehot, axis=0, keepdims=True)
    carry_scr[...] = jnp.broadcast_to(carry, carry_scr.shape)
    cout_ref[...] = carry


def _post_call(o_p, o_s, wo, x_p, x_s, gate_p, gate_s, g, shift_p, shift_s, scale_p, scale_s, wr_pad, br_pad,
               *, tm, blocks_per_seq, n_experts):
    n_p, d = x_p.shape
    n_s = x_s.shape[0]
    npb = n_p // tm
    nb = npb + n_s // tm
    n = n_p + n_s
    const2 = lambda i: (0, 0)
    row = lambda i: (i, 0)
    p_row = lambda i: (jnp.minimum(i, npb - 1), 0)
    s_row = lambda i: (jnp.maximum(i - npb, 0), 0)
    p_mod = pl.BlockSpec((None, 1, d), lambda i: (jnp.minimum(i, npb - 1) // blocks_per_seq, 0, 0))
    s_mod = pl.BlockSpec((None, tm, d), lambda i: (jnp.maximum(i - npb, 0), 0, 0))
    return pl.pallas_call(
        functools.partial(_post_kernel, n_experts=n_experts, n_prompt_blocks=npb),
        grid=(nb,),
        in_specs=[
            pl.BlockSpec((tm, d), p_row), pl.BlockSpec((tm, d), s_row),
            pl.BlockSpec((d, d), const2),
            pl.BlockSpec((tm, d), p_row), pl.BlockSpec((tm, d), s_row),
            p_mod, s_mod,
            pl.BlockSpec((1, d), const2),
            p_mod, s_mod, p_mod, s_mod,
            pl.BlockSpec((d, LANES), const2),
            pl.BlockSpec((1, LANES), const2),
        ],
        out_specs=[
            pl.BlockSpec((tm, d), row), pl.BlockSpec((tm, d), row), pl.BlockSpec((tm, LANES), row),
            pl.BlockSpec((None, SUBLANES, tm), lambda i: (i, 0, 0)),
            pl.BlockSpec((None, 1, LANES), lambda i: (i, 0, 0)),
            pl.BlockSpec((1, LANES), const2),
        ],
        out_shape=[
            jax.ShapeDtypeStruct((n, d), F32), jax.ShapeDtypeStruct((n, d), BF16),
            jax.ShapeDtypeStruct((n, LANES), F32),
            jax.ShapeDtypeStruct((nb, SUBLANES, tm), F32),
            jax.ShapeDtypeStruct((nb, 1, LANES), F32),
            jax.ShapeDtypeStruct((1, LANES), F32),
        ],
        scratch_shapes=[pltpu.VMEM((SUBLANES, LANES), F32)],
        compiler_params=_cparams("arbitrary"),
        name="post_attn_router",
    )(o_p, o_s, wo, x_p, x_s, gate_p, gate_s, g, shift_p, shift_s, scale_p, scale_s, wr_pad, br_pad)


def _sorted_pos(idx, rank, start_ref, n_experts):
    pos = rank
    for e in range(n_experts):
        pos = pos + jnp.where(idx == float(e), start_ref[e].astype(F32), 0.0)
    return pos


def _dispatch_kernel(lo_ref, hi_ref, start_ref, mt_ref, h_ref, xs_ref, gs_ref, acc_scr, gacc_scr,
                     *, tb, n_experts):
    j = pl.program_id(0)
    sub = xs_ref.shape[0]
    nb = mt_ref.shape[0]
    lo = lo_ref[j]
    hi = hi_ref[j]

    @pl.when(hi <= lo)
    def _():
        xs_ref[...] = jnp.zeros_like(xs_ref)
        gs_ref[...] = jnp.zeros_like(gs_ref)

    def hits(b, width):
        dest = (j * sub + lax.broadcasted_iota(jnp.int32, (sub, width * tb), 0)).astype(F32)
        mts = [mt_ref[b + w] for w in range(width)]
        field = lambda r: jnp.concatenate([m[r:r + 1, :] for m in mts], axis=1)
        hit0 = _sorted_pos(field(0), field(4), start_ref, n_experts) == dest
        hit1 = _sorted_pos(field(1), field(5), start_ref, n_experts) == dest
        gate = jnp.where(hit0, field(2), 0.0) + jnp.where(hit1, field(3), 0.0)
        return _ones_where(hit0 | hit1), gate

    n_even = nb - nb % 2

    def pair(m, carry):
        b = 2 * m
        oh, gate = hits(b, 2)
        rows = pl.ds(pl.multiple_of(b * tb, 2 * tb), 2 * tb)
        acc_scr[...] += _dot(oh, h_ref[rows, :])
        gacc_scr[...] += gate
        return carry

    @pl.when(hi > lo)
    def _():
        acc_scr[...] = jnp.zeros_like(acc_scr)
        gacc_scr[...] = jnp.zeros_like(gacc_scr)
        lax.fori_loop(lo // 2, (jnp.minimum(hi, n_even) + 1) // 2, pair, 0)

        if nb % 2:
            @pl.when(hi == nb)
            def _():
                oh, gate = hits(nb - 1, 1)
                acc_scr[...] += _dot(oh, h_ref[(nb - 1) * tb:nb * tb, :])
                gacc_scr[:, 0:tb] += gate

        xs_ref[...] = acc_scr[...].astype(BF16)
        gs_ref[...] = jnp.broadcast_to(jnp.sum(gacc_scr[...], axis=1, keepdims=True), gs_ref.shape)


def _dispatch_call(tile_lo, tile_hi, start, meta_t, h, *, n_sorted, n_experts):
    n, d = h.shape
    nb, _, tb = meta_t.shape
    whole = lambda nd: (lambda j, lo, hi, st: (0,) * nd)
    once = pl.Buffered(1)
    return pl.pallas_call(
        functools.partial(_dispatch_kernel, tb=tb, n_experts=n_experts),
        grid_spec=pltpu.PrefetchScalarGridSpec(
            num_scalar_prefetch=3,
            grid=(n_sorted // SUB,),
            in_specs=[
                pl.BlockSpec((nb, SUBLANES, tb), whole(3), pipeline_mode=once),
                pl.BlockSpec((n, d), whole(2), pipeline_mode=once),
            ],
            out_specs=[pl.BlockSpec((SUB, d), lambda j, lo, hi, st: (j, 0)),
                       pl.BlockSpec((SUB, LANES), lambda j, lo, hi, st: (j, 0))],
            scratch_shapes=[pltpu.VMEM((SUB, d), F32), pltpu.VMEM((SUB, 2 * tb), F32)],
        ),
        out_shape=[jax.ShapeDtypeStruct((n_sorted, d), BF16), jax.ShapeDtypeStruct((n_sorted, LANES), F32)],
        compiler_params=_cparams("arbitrary"),
        name="moe_dispatch",
    )(tile_lo, tile_hi, start, meta_t, h)


def _moe_kernel(ex_ref, nsub_ref, x_ref, gs_ref, wg_ref, wu_ref, wd_ref, y_ref, acc_scr):
    s = pl.program_id(0)
    j = pl.program_id(1)
    nsub = nsub_ref[s]
    last = j == pl.num_programs(1) - 1

    n_full = (nsub * SUB) // FFN_ROWS
    odd = nsub * SUB - n_full * FFN_ROWS > 0

    def tile_rows(t):
        return pl.ds(pl.multiple_of(t * SUB, SUB), SUB)

    def step_rows(t):
        return pl.ds(pl.multiple_of(t * FFN_ROWS, FFN_ROWS), FFN_ROWS)

    @pl.when(j == 0)
    def _():
        def zero(t, carry):
            acc_scr[tile_rows(t), :] = jnp.zeros((SUB, acc_scr.shape[1]), F32)
            return carry
        lax.fori_loop(0, nsub, zero, 0)

    def ffn(rows):
        x = x_ref[rows, :]
        a = _silu(_dot(x, wg_ref[...].astype(BF16))) * _dot(x, wu_ref[...].astype(BF16))
        acc_scr[rows, :] += _dot(a.astype(BF16), wd_ref[...].astype(BF16))

    def body(t, carry):
        ffn(step_rows(2 * t))
        ffn(step_rows(2 * t + 1))
        return carry

    lax.fori_loop(0, n_full // 2, body, 0)

    @pl.when(n_full % 2 == 1)
    def _():
        ffn(step_rows(n_full - 1))

    @pl.when(odd)
    def _():
        ffn(tile_rows(nsub - 1))

    @pl.when(last)
    def _():
        def emit(t, carry):
            rows = tile_rows(t)
            y_ref[rows, :] = (acc_scr[rows, :] * gs_ref[rows, 0:1]).astype(BF16)
            return carry
        lax.fori_loop(0, nsub, emit, 0)

        def zero(t, carry):
            y_ref[tile_rows(t), :] = jnp.zeros((SUB, y_ref.shape[1]), BF16)
            return carry
        lax.fori_loop(nsub, y_ref.shape[0] // SUB, zero, 0)


def _moe_call(sup_expert, sup_nsub, xs, gs, w_gu, w_down, *, ts):
    n_sorted, d = xs.shape
    f = w_down.shape[1]
    tf = min(TF_FFN, f)
    nf = f // tf
    n_super = n_sorted // ts

    def jj(s, j, nsub):
        return jnp.where(nsub[s] > 0, j, nf - 1)

    return pl.pallas_call(
        _moe_kernel,
        grid_spec=pltpu.PrefetchScalarGridSpec(
            num_scalar_prefetch=2,
            grid=(n_super, nf),
            in_specs=[
                pl.BlockSpec((ts, d), lambda s, j, ex, ns: (s, 0)),
                pl.BlockSpec((ts, LANES), lambda s, j, ex, ns: (s, 0)),
                pl.BlockSpec((None, d, tf), lambda s, j, ex, ns: (ex[s], 0, jj(s, j, ns))),
                pl.BlockSpec((None, d, tf), lambda s, j, ex, ns: (ex[s], 0, nf + jj(s, j, ns))),
                pl.BlockSpec((None, tf, d), lambda s, j, ex, ns: (ex[s], jj(s, j, ns), 0)),
            ],
            out_specs=pl.BlockSpec((ts, d), lambda s, j, ex, ns: (s, 0)),
            scratch_shapes=[pltpu.VMEM((ts, d), F32)],
        ),
        out_shape=jax.ShapeDtypeStruct((n_sorted, d), BF16),
        compiler_params=_cparams("arbitrary", "arbitrary"),
        name="moe_ffn",
    )(sup_expert, sup_nsub, xs, gs, w_gu, w_gu, w_down)


def _combine_kernel(tile_ref, need2_ref, start_ref, meta_ref, x_ref, gtp_ref, gts_ref, gf_ref, *rest,
                    n_experts, n_prompt_blocks):
    y_refs = rest[0:2 * n_experts]
    op_ref, os_ref, acc_scr = rest[2 * n_experts:]
    b = pl.program_id(0)
    tb = acc_scr.shape[0]
    meta = meta_ref[...]
    col = lax.broadcasted_iota(jnp.int32, (tb, SUB), 1)

    def window(e, w):
        first = start_ref[e].astype(F32)
        p0 = jnp.where(meta[:, 0:1] == float(e), meta[:, 4:5] + first, -1.0)
        p1 = jnp.where(meta[:, 1:2] == float(e), meta[:, 5:6] + first, -1.0)
        src = ((tile_ref[b * n_experts + e] + w) * SUB + col).astype(F32)
        return _ones_where((p0 == src) | (p1 == src))

    total = _dot(window(0, 0), y_refs[0][...])
    for e in range(1, n_experts):
        total = total + _dot(window(e, 0), y_refs[2 * e][...])
    acc_scr[...] = total
    for e in range(n_experts):
        @pl.when(need2_ref[b * n_experts + e] > 0)
        def _(e=e):
            acc_scr[...] += _dot(window(e, 1), y_refs[2 * e + 1][...])

    is_p = b < n_prompt_blocks
    x4 = x_ref[...] + jnp.where(is_p, gtp_ref[...], gts_ref[...]) * acc_scr[...]
    ms = jnp.mean(x4 * x4, axis=-1, keepdims=True)
    y = x4 * lax.rsqrt(ms + EPS) * gf_ref[...]

    @pl.when(is_p)
    def _():
        op_ref[...] = y

    @pl.when(jnp.logical_not(is_p))
    def _():
        os_ref[...] = y


def _combine_call(win_tile, win_need2, start, meta, x, gate_p, gate_s, g_final, ys, *, n_p, tb, blocks_per_seq,
                  n_experts):
    n, d = x.shape
    n_s = n - n_p
    npb = n_p // tb
    nb = n // tb

    def pb(b):
        return jnp.minimum(b, npb - 1)

    def sb(b):
        return jnp.maximum(b - npb, 0)

    def win_map(e, w):
        if w == 0:
            return lambda b, tile, need, st: (tile[b * n_experts + e], 0)
        return lambda b, tile, need, st: (jnp.where(need[b * n_experts + e] > 0, tile[b * n_experts + e] + 1, 0), 0)

    in_specs = [
        pl.BlockSpec((tb, LANES), lambda b, tile, need, st: (b, 0)),
        pl.BlockSpec((tb, d), lambda b, tile, need, st: (b, 0)),
        pl.BlockSpec((None, 1, d), lambda b, tile, need, st: (pb(b) // blocks_per_seq, 0, 0)),
        pl.BlockSpec((None, tb, d), lambda b, tile, need, st: (sb(b), 0, 0)),
        pl.BlockSpec((1, d), lambda b, tile, need, st: (0, 0)),
    ]
    in_specs += [pl.BlockSpec((SUB, d), win_map(e, w)) for e in range(n_experts) for w in range(2)]
    return pl.pallas_call(
        functools.partial(_combine_kernel, n_experts=n_experts, n_prompt_blocks=npb),
        grid_spec=pltpu.PrefetchScalarGridSpec(
            num_scalar_prefetch=3,
            grid=(nb,),
            in_specs=in_specs,
            out_specs=[pl.BlockSpec((tb, d), lambda b, tile, need, st: (pb(b), 0)),
                       pl.BlockSpec((tb, d), lambda b, tile, need, st: (sb(b), 0))],
            scratch_shapes=[pltpu.VMEM((tb, d), F32)],
        ),
        out_shape=[jax.ShapeDtypeStruct((n_p, d), F32), jax.ShapeDtypeStruct((n_s, d), F32)],
        compiler_params=_cparams("arbitrary"),
        name="moe_combine",
    )(win_tile, win_need2, start, meta, x, gate_p, gate_s, g_final, *([ys] * (2 * n_experts)))


def _routing_tables(blk_cnt, counts, *, n_experts, ts, n_super):
    nb = blk_cnt.shape[0]
    cnt = counts[0, :n_experts].astype(jnp.int32)
    n_sup_e = (cnt + ts - 1) // ts
    sup_start = jnp.cumsum(n_sup_e) - n_sup_e
    start = sup_start * ts

    s_ids = jnp.arange(n_super, dtype=jnp.int32)
    used = jnp.sum(n_sup_e)
    sup_end = sup_start + n_sup_e
    sup_e = jnp.sum((s_ids[:, None] >= sup_end[None, :]).astype(jnp.int32), axis=1)
    sup_e = jnp.clip(sup_e, 0, n_experts - 1)
    last_e = jnp.max(jnp.where(cnt > 0, jnp.arange(n_experts, dtype=jnp.int32), 0))
    sup_e = jnp.where(s_ids < used, sup_e, last_e)
    rows_in = jnp.clip(cnt[sup_e] - (s_ids - sup_start[sup_e]) * ts, 0, ts)
    sup_nsub = jnp.where(s_ids < used, (rows_in + SUB - 1) // SUB, 0).astype(jnp.int32)

    blk = blk_cnt[:, 0, :n_experts].astype(jnp.int32)
    blk_end = jnp.concatenate([blk[1:], cnt[None, :]], axis=0)
    n_tiles = n_super * (ts // SUB)
    t_ids = jnp.arange(n_tiles, dtype=jnp.int32)
    t_e = sup_e[t_ids // (ts // SUB)]
    t_r0 = t_ids * SUB - start[t_e]
    t_active = ((t_ids // (ts // SUB)) < used) & (t_r0 < cnt[t_e]) & (t_r0 >= 0)
    t_r1 = jnp.minimum(t_r0 + SUB, cnt[t_e])
    be = blk_end[:, t_e]
    bs = blk[:, t_e]
    overlap = (be > t_r0[None, :]) & (bs < t_r1[None, :])
    b_ids = jnp.arange(nb, dtype=jnp.int32)[:, None]
    lo = jnp.min(jnp.where(overlap, b_ids, nb), axis=0)
    hi = jnp.max(jnp.where(overlap, b_ids + 1, 0), axis=0)
    tile_lo = jnp.where(t_active, lo, 0).astype(jnp.int32)
    tile_hi = jnp.where(t_active, hi, 0).astype(jnp.int32)

    first = start[None, :] + blk
    n_be = blk_end - blk
    win_tile = (first // SUB).astype(jnp.int32)
    win_need2 = ((first + n_be) > (win_tile + 1) * SUB).astype(jnp.int32)

    return dict(start=start.astype(jnp.int32), sup_e=sup_e, sup_nsub=sup_nsub, tile_lo=tile_lo,
                tile_hi=tile_hi, win_tile=win_tile.reshape(-1), win_need2=win_need2.reshape(-1))


def kernel(x_prompt, x_sample, c_prompt, c_sample, cache_k, cache_v, cache_logf, page_table, norm_mix_g, norm_ffn_g, final_norm_g, ada_w, ada_b, gmlp_w_in, gmlp_v_g, gmlp_v_b, gmlp_w_s, gmlp_b_s, gmlp_w_out, fox_w_in, fox_b_f, fox_w_out, ffn_w_gu, ffn_w_down, moe_w_r, moe_b_r, moe_w_gu, moe_w_down):
    n_seq_p, t_len, d = x_prompt.shape
    n_seq_s, t_new, _ = x_sample.shape
    n_p = n_seq_p * t_len
    n_s = n_seq_s * t_new
    heads = N_HEADS
    hd = d // heads
    n_experts = moe_w_r.shape[-1]
    da = gmlp_w_out.shape[1]

    xp = x_prompt.reshape(n_p, d)
    xs = x_sample.reshape(n_s, d)

    mod = _ada_call(jnp.concatenate([c_prompt, c_sample], axis=0), ada_w, ada_b)

    def mods(layer):
        mp = [mod[layer, :n_seq_p, c * d:(c + 1) * d].reshape(n_seq_p, 1, d) for c in range(6)]
        ms = [jnp.repeat(mod[layer, n_seq_p:, c * d:(c + 1) * d], t_new, axis=0).reshape(1, n_s, d)
              for c in range(6)]
        return mp, ms

    row = lambda a: a.reshape(1, -1)

    mp, ms = mods(0)
    w_in = gmlp_w_in[0].astype(BF16)
    wu, wv = w_in[:, :da], w_in[:, da:]
    wo = gmlp_w_out[0].astype(BF16)
    vg, vb = row(gmlp_v_g[0]), row(gmlp_v_b[0])
    lc = min(GMLP_CHUNK, t_len)
    mp1, ms1 = mods(1)
    w_t = fox_w_in[0].T.astype(BF16)
    wqkv_t, wf_t = w_t[:3 * d], w_t[3 * d:]
    b_f = fox_b_f[0].reshape(heads, 1)

    reps = n_s // t_new
    ws_s = jnp.zeros((gmlp_w_s.shape[1], LANES, LANES), F32).at[:, :t_new, :t_new].set(
        gmlp_w_s[0][:, :t_new, :t_new])
    bs_s = jnp.tile(gmlp_b_s[0][:, :t_new], (1, reps)).T
    xs, v_rows = _gmlp_call(xs, row(norm_mix_g[0]), ms[0], ms[1], ms[2], wu, wv, vg, vb, ws_s, bs_s, wo,
                            tm=n_s, tiles_per_seq=1, period=t_new, emit_v=True)
    xs = _ffn_call(xs, row(norm_ffn_g[0]), ms[3], ms[4], ms[5], ffn_w_gu[0], ffn_w_down[0],
                   tm=n_s, tiles_per_seq=1)
    q_s, k_s, v_s, lft_s, cumt_s = _foxproj_call(
        xs, row(norm_mix_g[1]), ms1[0], ms1[1], wqkv_t, wf_t, b_f,
        tm=n_s, tiles_per_seq=1, transposed=False, period=t_new)

    page_size = cache_k.shape[2]
    n_phys = cache_k.shape[1]
    kc_t = jnp.transpose(cache_k[0], (0, 2, 3, 1)).reshape(n_phys, d, page_size)
    vc_t = jnp.transpose(cache_v[0], (0, 2, 3, 1)).reshape(n_phys, d, page_size)
    lfc_t = jnp.transpose(cache_logf[0], (0, 2, 1))
    cn = cumt_s[0].reshape(heads, n_seq_s, t_new).transpose(1, 0, 2)
    cn_pad = jnp.zeros((n_seq_s, heads, LANES), F32).at[:, :, :t_new].set(cn)
    xp, o_s = _gmlp_attn_call(xp, row(norm_mix_g[0]), mp[0], mp[1], mp[2], wu, wv, vg, vb,
                              gmlp_w_s[0][:, :lc, :lc], gmlp_b_s[0][:, :lc].T, wo,
                              page_table, q_s, cn_pad, k_s, v_s, kc_t, vc_t, lfc_t,
                              t_len=t_len, tn=t_new, heads=heads)
    tm = min(TM_FFN, t_len)
    xp = _ffn_call(xp, row(norm_ffn_g[0]), mp[3], mp[4], mp[5], ffn_w_gu[0], ffn_w_down[0],
                   tm=tm, tiles_per_seq=t_len // tm)

    mp, ms = mp1, ms1
    tm = min(TM_PROJ, t_len)
    qt_p, kt_p, kb_p, vt_p, vtb_p, lft_p, cumt_p = _foxproj_call(
        xp, row(norm_mix_g[1]), mp[0], mp[1], wqkv_t, wf_t, b_f,
        tm=tm, tiles_per_seq=t_len // tm, transposed=True, period=tm)
    nq, tq = qt_p.shape[1], qt_p.shape[3]
    o_p = _attn_call(qt_p, kb_p, vtb_p, cumt_p.reshape(n_seq_p, heads // 2, 2, nq, tq), t_len=t_len, heads=heads)

    wo_f = fox_w_out[0].astype(BF16)
    wr_pad = jnp.zeros((d, LANES), F32).at[:, :n_experts].set(moe_w_r[0])
    br_pad = jnp.zeros((1, LANES), F32).at[0, :n_experts].set(moe_b_r[0])
    tb = min(TM_POST, t_len)
    ms_blk = [m.reshape(n_s // tb, tb, d) for m in ms]
    x3, h_all, meta, meta_t, blk_cnt, counts = _post_call(
        o_p, o_s, wo_f, xp, xs, mp[2], ms_blk[2], row(norm_ffn_g[1]), mp[3], ms_blk[3], mp[4], ms_blk[4],
        wr_pad, br_pad, tm=tb, blocks_per_seq=t_len // tb, n_experts=n_experts)

    n_tok = n_p + n_s
    ts = TS_MOE
    n_super = (TOP_K * n_tok) // ts + n_experts + 1
    rt = _routing_tables(blk_cnt, counts, n_experts=n_experts, ts=ts, n_super=n_super)
    xs_sorted, g_sorted = _dispatch_call(rt["tile_lo"], rt["tile_hi"], rt["start"], meta_t, h_all,
                                         n_sorted=n_super * ts, n_experts=n_experts)
    ys_sorted = _moe_call(rt["sup_e"], rt["sup_nsub"], xs_sorted, g_sorted, moe_w_gu[0], moe_w_down[0], ts=ts)
    y_p, y_s = _combine_call(rt["win_tile"], rt["win_need2"], rt["start"], meta, x3, mp[5], ms_blk[5],
                             row(final_norm_g), ys_sorted,
                             n_p=n_p, tb=tb, blocks_per_seq=t_len // tb, n_experts=n_experts)

    y_prompt = y_p.reshape(n_seq_p, t_len, d)
    y_sample = y_s.reshape(n_seq_s, t_new, d)
    state_a_v_sample = v_rows.reshape(1, n_seq_s, t_new, da)
    k_prompt = kt_p.reshape(1, n_seq_p, heads, hd, t_len).transpose(0, 1, 4, 2, 3)
    v_prompt = vt_p.reshape(1, n_seq_p, heads, hd, t_len).transpose(0, 1, 4, 2, 3)
    logf_prompt = lft_p.transpose(0, 2, 1)[None]
    k_sample = k_s.reshape(1, n_seq_s, t_new, heads, hd)
    v_sample = v_s.reshape(1, n_seq_s, t_new, heads, hd)
    logf_sample = lft_s[0].T.reshape(1, n_seq_s, t_new, heads)
    return (y_prompt, y_sample, state_a_v_sample, k_prompt, v_prompt, logf_prompt, k_sample, v_sample, logf_sample)
```

```python
import functools

import jax
import jax.numpy as jnp
from jax import lax
from jax.experimental import pallas as pl
from jax.experimental.pallas import tpu as pltpu

F32 = jnp.float32
BF16 = jnp.bfloat16

N_HEADS = 16
GMLP_GROUPS = 8
GMLP_CHUNK = 128
TOP_K = 2
EPS = 1e-6
NEG = -1e30
LOG2E = 1.4426950408889634

LANES = 128
SUBLANES = 8
MXU_TILE = 256
VMEM_LIMIT = 56 * 1024 * 1024

TM_FFN = 1024
TF_FFN = 512
TM_PROJ = 512
TQ_ATTN = 512
TM_POST = 256
SUB = 256
FFN_ROWS = 2 * SUB
TS_MOE = 2560
PAGES_PER_STEP = 16


def _cparams(*sem):
    return pltpu.CompilerParams(dimension_semantics=sem, vmem_limit_bytes=VMEM_LIMIT)


def _dot(a, b):
    return jnp.dot(a, b, preferred_element_type=F32)


def _dot_nt(a, b):
    return lax.dot_general(a, b, (((1,), (1,)), ((), ())), preferred_element_type=F32)


def _norm_mod(x, g, shift, scale):
    ms = jnp.mean(x * x, axis=-1, keepdims=True)
    y = x * lax.rsqrt(ms + EPS) * g
    return y * (1.0 + scale) + shift


def _gelu(x):
    return 0.5 * x * (1.0 + lax.erf(x * (2.0 ** -0.5)))


def _silu(x):
    return x * jax.nn.sigmoid(x)


def _ones_where(cond):
    return jnp.where(cond, 1.0, 0.0).astype(BF16)


def _shift_div(x, c):
    assert c & (c - 1) == 0
    return lax.shift_right_logical(x, c.bit_length() - 1)


def _split3(x):
    hi = x.astype(BF16)
    r = x - hi.astype(F32)
    mid = r.astype(BF16)
    lo = (r - mid.astype(F32)).astype(BF16)
    return hi, mid, lo


def _ada_kernel(c_ref, w_ref, b_ref, o_ref):
    s = _silu(c_ref[...]).astype(BF16)
    o_ref[...] = _dot(s, w_ref[...].astype(BF16)) + b_ref[...]


def _ada_call(c_all, ada_w, ada_b):
    n_layers, d, d6 = ada_w.shape
    r = c_all.shape[0]
    tn = min(d6, 1536)
    return pl.pallas_call(
        _ada_kernel,
        grid=(n_layers, d6 // tn),
        in_specs=[
            pl.BlockSpec((r, d), lambda l, j: (0, 0)),
            pl.BlockSpec((None, d, tn), lambda l, j: (l, 0, j)),
            pl.BlockSpec((None, 1, tn), lambda l, j: (l, 0, j)),
        ],
        out_specs=pl.BlockSpec((None, r, tn), lambda l, j: (l, 0, j)),
        out_shape=jax.ShapeDtypeStruct((n_layers, r, d6), F32),
        compiler_params=_cparams("arbitrary", "arbitrary"),
        name="ada",
    )(c_all, ada_w, ada_b.reshape(n_layers, 1, d6))


def _mod_spec(mod, tiles_per_seq):
    _, rows, d = mod.shape
    return pl.BlockSpec((None, rows, d), lambda i, *_: (i // tiles_per_seq, 0, 0))


def _gmlp_kernel(x_ref, g_ref, sh_ref, sc_ref, gt_ref, wu_ref, wv_ref, vg_ref, vb_ref, ws_ref, bs_ref,
                 wo_ref, *rest, lc, period, groups, emit_v):
    if emit_v:
        o_ref, v_ref, vn_scr, out_scr = rest
    else:
        o_ref, vn_scr, out_scr = rest
    x = x_ref[...]
    tm = x.shape[0]
    h = _norm_mod(x, g_ref[...], sh_ref[...], sc_ref[...]).astype(BF16)
    v = _gelu(_dot(h, wv_ref[...]))
    mu = jnp.mean(v, axis=-1, keepdims=True)
    vc = v - mu
    var = jnp.mean(vc * vc, axis=-1, keepdims=True)
    vn = vc * lax.rsqrt(var + EPS) * vg_ref[...] + vb_ref[...]
    if emit_v:
        v_ref[...] = vn
    vn_scr[...] = vn.astype(BF16)
    gd = vn.shape[1] // groups
    r = lax.broadcasted_iota(jnp.int32, (lc, lc), 0)
    c = lax.broadcasted_iota(jnp.int32, (lc, lc), 1)
    mask = c <= r
    if period < lc:
        blk = ~(period - 1)
        mask = mask & ((r & blk) == (c & blk))
        ri = lax.broadcasted_iota(jnp.int32, (lc, LANES), 0)
        ci = lax.broadcasted_iota(jnp.int32, (lc, LANES), 1)
        sel = _ones_where((ri & (period - 1)) == ci)
    for g in range(groups):
        u = _gelu(_dot(h, wu_ref[:, g * gd:(g + 1) * gd]))
        if period < lc:
            rows_of_block = _dot(sel, ws_ref[g].astype(BF16)).astype(BF16)
            ws_full = _dot_nt(rows_of_block, sel)
        else:
            ws_full = ws_ref[g]
        wsm = jnp.where(mask, ws_full, 0.0).astype(BF16)
        bcol = bs_ref[:, g:g + 1]
        for ci in range(tm // lc):
            rows = slice(ci * lc, (ci + 1) * lc)
            mixed = _dot(wsm, vn_scr[rows, g * gd:(g + 1) * gd]) + bcol
            out_scr[rows, g * gd:(g + 1) * gd] = (u[rows] * mixed).astype(BF16)
    o_ref[...] = x + gt_ref[...] * _dot(out_scr[...], wo_ref[...])


def _gmlp_call(x, g, shift, scale, gate, wu, wv, vg, vb, ws, bs_t, wo, *, tm, tiles_per_seq, period, emit_v):
    n, d = x.shape
    da = wu.shape[1]
    groups = ws.shape[0]
    lc = bs_t.shape[0]
    const2 = lambda i: (0, 0)
    in_specs = [
        pl.BlockSpec((tm, d), lambda i: (i, 0)),
        pl.BlockSpec((1, d), const2),
        _mod_spec(shift, tiles_per_seq), _mod_spec(scale, tiles_per_seq), _mod_spec(gate, tiles_per_seq),
        pl.BlockSpec((d, da), const2), pl.BlockSpec((d, da), const2),
        pl.BlockSpec((1, da), const2), pl.BlockSpec((1, da), const2),
        pl.BlockSpec(ws.shape, lambda i: (0, 0, 0)),
        pl.BlockSpec((lc, groups), const2),
        pl.BlockSpec((da, d), const2),
    ]
    out_specs = [pl.BlockSpec((tm, d), lambda i: (i, 0))]
    out_shape = [jax.ShapeDtypeStruct((n, d), F32)]
    if emit_v:
        out_specs.append(pl.BlockSpec((tm, da), lambda i: (i, 0)))
        out_shape.append(jax.ShapeDtypeStruct((n, da), F32))
    return pl.pallas_call(
        functools.partial(_gmlp_kernel, lc=lc, period=period, groups=groups, emit_v=emit_v),
        grid=(n // tm,),
        in_specs=in_specs, out_specs=out_specs, out_shape=out_shape,
        scratch_shapes=[pltpu.VMEM((tm, da), BF16), pltpu.VMEM((tm, da), BF16)],
        compiler_params=_cparams("arbitrary"),
        name="gmlp",
    )(x, g, shift, scale, gate, wu, wv, vg, vb, ws, bs_t, wo)


def _ffn_kernel(x_ref, g_ref, sh_ref, sc_ref, gt_ref, wg_ref, wu_ref, wd_ref, o_ref, h_scr, acc_scr):
    j = pl.program_id(1)

    @pl.when(j == 0)
    def _():
        h_scr[...] = _norm_mod(x_ref[...], g_ref[...], sh_ref[...], sc_ref[...]).astype(BF16)
        acc_scr[...] = jnp.zeros_like(acc_scr)

    h = h_scr[...]
    a = _silu(_dot(h, wg_ref[...].astype(BF16))) * _dot(h, wu_ref[...].astype(BF16))
    acc_scr[...] += _dot(a.astype(BF16), wd_ref[...].astype(BF16))

    @pl.when(j == pl.num_programs(1) - 1)
    def _():
        o_ref[...] = x_ref[...] + gt_ref[...] * acc_scr[...]


def _ffn_call(x, g, shift, scale, gate, w_gu, w_down, *, tm, tiles_per_seq):
    n, d = x.shape
    f = w_down.shape[0]
    tf = min(TF_FFN, f)
    nf = f // tf
    const2 = lambda i, j: (0, 0)
    return pl.pallas_call(
        _ffn_kernel,
        grid=(n // tm, nf),
        in_specs=[
            pl.BlockSpec((tm, d), lambda i, j: (i, 0)),
            pl.BlockSpec((1, d), const2),
            _mod_spec(shift, tiles_per_seq), _mod_spec(scale, tiles_per_seq), _mod_spec(gate, tiles_per_seq),
            pl.BlockSpec((d, tf), lambda i, j: (0, j)),
            pl.BlockSpec((d, tf), lambda i, j: (0, nf + j)),
            pl.BlockSpec((tf, d), lambda i, j: (j, 0)),
        ],
        out_specs=pl.BlockSpec((tm, d), lambda i, j: (i, 0)),
        out_shape=jax.ShapeDtypeStruct((n, d), F32),
        scratch_shapes=[pltpu.VMEM((tm, d), BF16), pltpu.VMEM((tm, d), F32)],
        compiler_params=_cparams("parallel", "arbitrary"),
        name="ffn",
    )(x, g, shift, scale, gate, w_gu, w_gu, w_down)


def _log_sigmoid(x):
    return jnp.minimum(x, 0.0) - jnp.log(1.0 + jnp.exp(-jnp.abs(x)))


def _foxproj_kernel(x_ref, g_ref, sh_ref, sc_ref, w_ref, wf_ref, bf_ref, *rest, transposed, period,
                    tiles_per_seq, q_scale):
    if transposed:
        q_ref, k_ref, kb_ref, v_ref, vb_ref, lf_ref, cum_ref, carry_scr = rest
    else:
        q_ref, k_ref, v_ref, lf_ref, cum_ref, carry_scr = rest
    i = pl.program_id(0)
    x = x_ref[...]
    tm, d = x.shape
    h = _norm_mod(x, g_ref[...], sh_ref[...], sc_ref[...]).astype(BF16)
    if transposed:
        tq = q_ref.shape[-1]
        qt = (_dot_nt(w_ref[0:d, :], h) * q_scale).astype(BF16)
        for jq in range(tm // tq):
            q_ref[jq] = qt[:, jq * tq:(jq + 1) * tq]
        kt = _dot_nt(w_ref[d:2 * d, :], h)
        k_ref[...] = kt
        kb_ref[...] = kt.astype(BF16)
        vt = _dot_nt(w_ref[2 * d:3 * d, :], h)
        v_ref[...] = vt
        vb_ref[...] = vt.astype(BF16)
    else:
        q_ref[...] = _dot_nt(h, w_ref[0:d, :]) * q_scale
        k_ref[...] = _dot_nt(h, w_ref[d:2 * d, :])
        v_ref[...] = _dot_nt(h, w_ref[2 * d:3 * d, :])
    logf = _log_sigmoid(_dot_nt(wf_ref[...], h) + bf_ref[...])
    lf_ref[...] = logf
    s = lax.broadcasted_iota(jnp.int32, (tm, tm), 0)
    t = lax.broadcasted_iota(jnp.int32, (tm, tm), 1)
    upper = s <= t
    if period < tm:
        blk = ~(period - 1)
        upper = upper & ((s & blk) == (t & blk))
    upper = _ones_where(upper)
    hi, mid, lo = _split3(logf)
    cum = _dot(hi, upper) + _dot(mid, upper) + _dot(lo, upper)
    if tiles_per_seq > 1:
        @pl.when(i % tiles_per_seq == 0)
        def _():
            carry_scr[...] = jnp.zeros_like(carry_scr)
        cum = cum + carry_scr[:, 0:1]
        carry_scr[...] = jnp.broadcast_to(cum[:, tm - 1:tm], carry_scr.shape)
    cum_ref[...] = cum


def _foxproj_call(x, g, shift, scale, w_t, wf_t, b_f, *, tm, tiles_per_seq, transposed, period):
    n, d = x.shape
    heads = wf_t.shape[0]
    n_seq = n // (tm * tiles_per_seq)
    t_len = tm * tiles_per_seq
    const2 = lambda i: (0, 0)
    row = lambda i: (i, 0)
    seq_t = lambda i: (i // tiles_per_seq, 0, i % tiles_per_seq)
    in_specs = [
        pl.BlockSpec((tm, d), row),
        pl.BlockSpec((1, d), const2),
        _mod_spec(shift, tiles_per_seq), _mod_spec(scale, tiles_per_seq),
        pl.BlockSpec((3 * d, d), const2),
        pl.BlockSpec((heads, d), const2),
        pl.BlockSpec((heads, 1), const2),
    ]
    lf_spec = pl.BlockSpec((None, heads, tm), seq_t)
    lf_shape = jax.ShapeDtypeStruct((n_seq, heads, t_len), F32)
    if transposed:
        tq = min(TQ_ATTN, tm)
        kv_spec = pl.BlockSpec((None, d, tm), seq_t)
        q_spec = pl.BlockSpec((None, tm // tq, d, tq), lambda i: (i // tiles_per_seq, i % tiles_per_seq, 0, 0))
        out_specs = [q_spec, kv_spec, kv_spec, kv_spec, kv_spec, lf_spec, lf_spec]
        out_shape = [jax.ShapeDtypeStruct((n_seq, t_len // tq, d, tq), BF16),
                     jax.ShapeDtypeStruct((n_seq, d, t_len), F32), jax.ShapeDtypeStruct((n_seq, d, t_len), BF16),
                     jax.ShapeDtypeStruct((n_seq, d, t_len), F32), jax.ShapeDtypeStruct((n_seq, d, t_len), BF16),
                     lf_shape, lf_shape]
    else:
        out_specs = [pl.BlockSpec((tm, d), row)] * 3 + [lf_spec, lf_spec]
        out_shape = [jax.ShapeDtypeStruct((n, d), F32)] * 3 + [lf_shape, lf_shape]
    return pl.pallas_call(
        functools.partial(_foxproj_kernel, transposed=transposed, period=period, tiles_per_seq=tiles_per_seq,
                          q_scale=float(d // heads) ** -0.5 * (LOG2E if transposed else 1.0)),
        grid=(n // tm,),
        in_specs=in_specs, out_specs=out_specs, out_shape=out_shape,
        scratch_shapes=[pltpu.VMEM((heads, LANES), F32)],
        compiler_params=_cparams("arbitrary"),
        name="foxproj",
    )(x, g, shift, scale, w_t, wf_t, b_f)


def _attn_kernel(qt_ref, k_ref, vt_ref, cum_ref, o_ref, ck_scr, m_scr, l_scr, acc_scr, *, hd):
    nq, _, tq = qt_ref.shape
    dim = lax.broadcasted_iota(jnp.int32, (LANES, tq), 0)
    key = lax.broadcasted_iota(jnp.int32, (tq, tq), 0)
    qry = lax.broadcasted_iota(jnp.int32, (tq, tq), 1)
    eye = key == qry
    causal = key <= qry
    m_scr[...] = jnp.full_like(m_scr, NEG)
    l_scr[...] = jnp.zeros_like(l_scr)
    acc_scr[...] = jnp.zeros_like(acc_scr)

    def update(hh, qi, kh, vh, masked):
        t = lax.dot_general(kh, qt_ref[qi], (((0,), (0,)), ((), ())), preferred_element_type=F32) - ck_scr[...]
        if masked:
            t = jnp.where(causal, t, NEG)
        cq = cum_ref[hh, qi:qi + 1, :] * LOG2E
        m_old = m_scr[hh, qi]
        m_new = jnp.maximum(m_old, jnp.max(t, axis=0, keepdims=True) + cq)
        alpha = jnp.exp2(m_old - m_new)
        p = jnp.exp2(t + (cq - m_new))
        l_scr[hh, qi] = alpha * l_scr[hh, qi] + jnp.sum(p, axis=0, keepdims=True)
        acc_scr[hh, qi] = alpha * acc_scr[hh, qi] + _dot(vh, p.astype(BF16))
        m_scr[hh, qi] = m_new

    for ki in range(nq):
        keys = slice(ki * tq, (ki + 1) * tq)
        k_blk = k_ref[:, keys]
        for hh in range(2):
            kh = jnp.where((dim >= hh * hd) & (dim < (hh + 1) * hd), k_blk, jnp.zeros_like(k_blk))
            vh = vt_ref[hh * hd:(hh + 1) * hd, keys]
            col = jnp.sum(jnp.where(eye, cum_ref[hh, ki:ki + 1, :], 0.0), axis=1, keepdims=True)
            ck_scr[...] = jnp.broadcast_to(col * LOG2E, ck_scr.shape)
            update(hh, ki, kh, vh, True)
            for qi in range(ki + 1, nq):
                update(hh, qi, kh, vh, False)

    for qi in range(nq):
        ot = jnp.concatenate([acc_scr[0, qi] / l_scr[0, qi], acc_scr[1, qi] / l_scr[1, qi]], axis=0)
        o_ref[qi * tq:(qi + 1) * tq, :] = ot.T.astype(o_ref.dtype)


def _attn_call(qt, k, vt, cum5, *, t_len, heads):
    n_seq, nq, d, tq = qt.shape
    hd = d // heads
    assert 2 * hd == LANES
    return pl.pallas_call(
        functools.partial(_attn_kernel, hd=hd),
        grid=(n_seq, heads // 2),
        in_specs=[
            pl.BlockSpec((None, nq, LANES, tq), lambda b, p: (b, 0, p, 0)),
            pl.BlockSpec((None, LANES, t_len), lambda b, p: (b, p, 0)),
            pl.BlockSpec((None, LANES, t_len), lambda b, p: (b, p, 0)),
            pl.BlockSpec((None, None, 2, nq, tq), lambda b, p: (b, p, 0, 0, 0)),
        ],
        out_specs=pl.BlockSpec((t_len, LANES), lambda b, p: (b, p)),
        out_shape=jax.ShapeDtypeStruct((n_seq * t_len, d), BF16),
        scratch_shapes=[pltpu.VMEM((tq, tq), F32), pltpu.VMEM((2, nq, 1, tq), F32),
                        pltpu.VMEM((2, nq, 1, tq), F32), pltpu.VMEM((2, nq, hd, tq), F32)],
        compiler_params=_cparams("parallel", "parallel"),
        name="attn_prompt",
    )(qt, k, vt, cum5)


def _attn_sample_body(j, n_j, co_work, q_ref, cn_ref, kn_ref, vn_ref, *rest, pages, hd, page_size):
    k_refs = rest[0:pages]
    v_refs = rest[pages:2 * pages]
    lf_refs = rest[2 * pages:3 * pages]
    o_ref, qbd_scr, cn_scr, m_scr, l_scr, acc_scr, suf_scr = rest[3 * pages:]
    tn, d = q_ref.shape
    heads = d // hd
    rows = heads * tn
    row = lax.broadcasted_iota(jnp.int32, (rows, LANES), 0)
    lane = lax.broadcasted_iota(jnp.int32, (rows, LANES), 1)
    q_of_row = row & (tn - 1)

    def rep(a):
        return jnp.broadcast_to(a[:, None, :], (heads, tn, a.shape[-1])).reshape(rows, a.shape[-1])

    gw = qbd_scr.shape[1]
    gr = (gw // hd) * tn
    n_groups = d // gw
    g_rows = lambda g: slice(g * gr, (g + 1) * gr)
    g_cols = lambda g: slice(g * gw, (g + 1) * gw)
    rr = lax.broadcasted_iota(jnp.int32, (gr, gw), 0)
    cc = lax.broadcasted_iota(jnp.int32, (gr, gw), 1)
    own_head = _shift_div(rr, tn) == _shift_div(cc, hd)

    @pl.when(j == 0)
    def _():
        q = q_ref[...]
        for g in range(n_groups):
            qrep = jnp.broadcast_to(q[None, :, g_cols(g)], (gw // hd, tn, gw)).reshape(gr, gw)
            qbd_scr[g_rows(g), :] = jnp.where(own_head, qrep, 0.0).astype(BF16)
        cn_scr[...] = jnp.sum(jnp.where(lane == q_of_row, rep(cn_ref[...]), 0.0), axis=1, keepdims=True)
        m_scr[...] = jnp.full_like(m_scr, NEG)
        l_scr[...] = jnp.zeros_like(l_scr)
        acc_scr[...] = jnp.zeros_like(acc_scr)
        suf_scr[...] = jnp.zeros_like(suf_scr)

    co_work()

    s_idx = lax.broadcasted_iota(jnp.int32, (page_size, page_size), 0)
    t_idx = lax.broadcasted_iota(jnp.int32, (page_size, page_size), 1)
    later = _ones_where(s_idx > t_idx)
    lf_all = jnp.concatenate([lf_refs[i][...] for i in range(pages)], axis=0)
    hi, mid, lo = _split3(lf_all)
    suf_all = _dot(hi, later) + _dot(mid, later) + _dot(lo, later)
    tot_all = suf_all[:, 0:1] + lf_all[:, 0:1]
    carry = suf_scr[:, 0:1]
    sufs = [None] * pages
    for i in reversed(range(pages)):
        sufs[i] = suf_all[i * heads:(i + 1) * heads, :] + carry
        carry = carry + tot_all[i * heads:(i + 1) * heads, :]
    suf_scr[...] = jnp.broadcast_to(carry, suf_scr.shape)

    def page_pair(refs, i, g):
        return jnp.concatenate([refs[i][g_cols(g), :].astype(BF16), refs[i + 1][g_cols(g), :].astype(BF16)], axis=1)

    scores = []
    for i in range(0, pages, 2):
        qk = [_dot(qbd_scr[g_rows(g), :], page_pair(k_refs, i, g)) for g in range(n_groups)]
        scores.append(jnp.concatenate(qk, axis=0) + rep(jnp.concatenate([sufs[i], sufs[i + 1]], axis=1)))
    s = jnp.concatenate(scores, axis=1) + cn_scr[...]
    m_old = m_scr[...]
    m_new = jnp.maximum(m_old, jnp.max(s, axis=1, keepdims=True))
    alpha = jnp.exp(m_old - m_new)
    p = jnp.exp(s - m_new)
    l_scr[...] = alpha * l_scr[...] + jnp.sum(p, axis=1, keepdims=True)
    pb = p.astype(BF16)
    for g in range(n_groups):
        pv = _dot_nt(pb[g_rows(g), 0:2 * page_size], page_pair(v_refs, 0, g))
        for i in range(2, pages, 2):
            pv = pv + _dot_nt(pb[g_rows(g), i * page_size:(i + 2) * page_size], page_pair(v_refs, i, g))
        acc_scr[g_rows(g), :] = alpha[g_rows(g), :] * acc_scr[g_rows(g), :] + pv
    m_scr[...] = m_new

    @pl.when(j == n_j - 1)
    def _():
        pad = jnp.zeros((LANES - tn, d), BF16)
        k_new = jnp.concatenate([kn_ref[...].astype(BF16), pad], axis=0)
        v_new = jnp.concatenate([vn_ref[...].astype(BF16), pad], axis=0)
        qk = [_dot_nt(qbd_scr[g_rows(g), :], k_new[:, g_cols(g)]) for g in range(n_groups)]
        s = jnp.concatenate(qk, axis=0) + cn_scr[...] - rep(cn_ref[...])
        s = jnp.where(lane <= q_of_row, s, NEG)
        m_old = m_scr[...]
        m_new = jnp.maximum(m_old, jnp.max(s, axis=1, keepdims=True))
        alpha = jnp.exp(m_old - m_new)
        pb = jnp.exp(s - m_new)
        l_new = alpha * l_scr[...] + jnp.sum(pb, axis=1, keepdims=True)
        pb = pb.astype(BF16)
        for g in range(n_groups):
            o = alpha[g_rows(g), :] * acc_scr[g_rows(g), :] + _dot(pb[g_rows(g), :], v_new[:, g_cols(g)])
            o = jnp.where(own_head, o / l_new[g_rows(g), :], 0.0)
            o_ref[:, g_cols(g)] = jnp.sum(o.reshape(gw // hd, tn, gw), axis=0)


N_GMLP_IN = 12


def _gmlp_attn_kernel(pt_ref, *refs, n_steps, pages, hd, page_size, lc, groups):
    n_attn_in = 4 + 3 * pages
    gmlp_in = refs[:N_GMLP_IN]
    attn_in = refs[N_GMLP_IN:N_GMLP_IN + n_attn_in]
    x1_ref, o_ref, vn_scr, out_scr, *attn_scr = refs[N_GMLP_IN + n_attn_in:]
    i = pl.program_id(0)
    mixer = functools.partial(_gmlp_kernel, *gmlp_in, x1_ref, vn_scr, out_scr, lc=lc, period=lc, groups=groups,
                              emit_v=False)
    _attn_sample_body(i % n_steps, n_steps, mixer, *attn_in, o_ref, *attn_scr,
                      pages=pages, hd=hd, page_size=page_size)


def _gmlp_attn_call(x, g, shift, scale, gate, wu, wv, vg, vb, ws, bs_t, wo,
                    page_table, q, cn_pad, k_new, v_new, kc_t, vc_t, lfc_t, *, t_len, tn, heads):
    n, d = x.shape
    da = wu.shape[1]
    groups = ws.shape[0]
    lc = bs_t.shape[0]
    n_s = q.shape[0]
    n_seq, n_pages = page_table.shape
    page_size = kc_t.shape[-1]
    pages = min(PAGES_PER_STEP, n_pages)
    n_steps = n_pages // pages
    hd = d // heads
    n_grid = n_seq * n_steps
    tm = n // n_grid
    assert tm * n_grid == n and tm % lc == 0 and t_len % tm == 0, (n, n_grid, lc)
    tiles_per_seq = t_len // tm
    once = pl.Buffered(1)
    const2 = lambda i, pt: (0, 0)
    in_specs = [
        pl.BlockSpec((tm, d), lambda i, pt: (i, 0)),
        pl.BlockSpec((1, d), const2),
        _mod_spec(shift, tiles_per_seq), _mod_spec(scale, tiles_per_seq), _mod_spec(gate, tiles_per_seq),
        pl.BlockSpec((d, da), const2, pipeline_mode=once), pl.BlockSpec((d, da), const2, pipeline_mode=once),
        pl.BlockSpec((1, da), const2), pl.BlockSpec((1, da), const2),
        pl.BlockSpec(ws.shape, lambda i, pt: (0, 0, 0)),
        pl.BlockSpec((lc, groups), const2),
        pl.BlockSpec((da, d), const2, pipeline_mode=once),
    ]
    assert len(in_specs) == N_GMLP_IN

    def page_map(ip):
        def index(i, pt):
            return (pt[(i // n_steps) * n_pages + (n_steps - 1 - i % n_steps) * pages + ip], 0, 0)
        return index

    seq_row = lambda i, pt: (i // n_steps, 0)
    in_specs += [
        pl.BlockSpec((tn, d), seq_row),
        pl.BlockSpec((None, heads, LANES), lambda i, pt: (i // n_steps, 0, 0)),
        pl.BlockSpec((tn, d), seq_row), pl.BlockSpec((tn, d), seq_row),
    ]
    in_specs += [pl.BlockSpec((None, d, page_size), page_map(ip)) for ip in range(pages)]
    in_specs += [pl.BlockSpec((None, d, page_size), page_map(ip)) for ip in range(pages)]
    in_specs += [pl.BlockSpec((None, heads, page_size), page_map(ip)) for ip in range(pages)]
    rows = heads * tn
    return pl.pallas_call(
        functools.partial(_gmlp_attn_kernel, n_steps=n_steps, pages=pages, hd=hd, page_size=page_size,
                          lc=lc, groups=groups),
        grid_spec=pltpu.PrefetchScalarGridSpec(
            num_scalar_prefetch=1,
            grid=(n_grid,),
            in_specs=in_specs,
            out_specs=[pl.BlockSpec((tm, d), lambda i, pt: (i, 0)), pl.BlockSpec((tn, d), seq_row)],
            scratch_shapes=[pltpu.VMEM((tm, da), BF16), pltpu.VMEM((tm, da), BF16),
                            pltpu.VMEM((rows, MXU_TILE), BF16), pltpu.VMEM((rows, 1), F32),
                            pltpu.VMEM((rows, 1), F32), pltpu.VMEM((rows, 1), F32),
                            pltpu.VMEM((rows, MXU_TILE), F32), pltpu.VMEM((heads, LANES), F32)],
        ),
        out_shape=[jax.ShapeDtypeStruct((n, d), F32), jax.ShapeDtypeStruct((n_s, d), F32)],
        compiler_params=_cparams("arbitrary"),
        name="gmlp_attn_sample",
    )(page_table.reshape(-1), x, g, shift, scale, gate, wu, wv, vg, vb, ws, bs_t, wo,
      q, cn_pad, k_new, v_new, *([kc_t] * pages), *([vc_t] * pages), *([lfc_t] * pages))


def _post_kernel(op_ref, os_ref, wo_ref, xp_ref, xs_ref, gtp_ref, gts_ref, g_ref, shp_ref, shs_ref, scp_ref,
                 scs_ref, wr_ref, br_ref, x3_ref, h_ref, meta_ref, metat_ref, blk_ref, cout_ref, carry_scr,
                 *, n_experts, n_prompt_blocks):
    i = pl.program_id(0)
    tm = xp_ref.shape[0]
    is_p = i < n_prompt_blocks

    @pl.when(i == 0)
    def _():
        carry_scr[...] = jnp.zeros_like(carry_scr)

    o = jnp.where(is_p, op_ref[...], os_ref[...].astype(BF16))
    x = jnp.where(is_p, xp_ref[...], xs_ref[...])
    gate = jnp.where(is_p, gtp_ref[...], gts_ref[...])
    shift = jnp.where(is_p, shp_ref[...], shs_ref[...])
    scale = jnp.where(is_p, scp_ref[...], scs_ref[...])
    x3 = x + gate * _dot(o, wo_ref[...])
    x3_ref[...] = x3
    h = _norm_mod(x3, g_ref[...], shift, scale)
    h_hi = h.astype(BF16)
    h_ref[...] = h_hi
    h_lo = (h - h_hi.astype(F32)).astype(BF16)
    wr = wr_ref[...]
    w_hi = wr.astype(BF16)
    w_lo = (wr - w_hi.astype(F32)).astype(BF16)
    logits = _dot(h_hi, w_hi) + _dot(h_hi, w_lo) + _dot(h_lo, w_hi) + br_ref[...]
    lane_i = lax.broadcasted_iota(jnp.int32, (tm, LANES), 1)
    lane = lane_i.astype(F32)
    logits = jnp.where(lane_i < n_experts, logits, NEG)
    l1 = jnp.max(logits, axis=1, keepdims=True)
    i1 = jnp.min(jnp.where(logits == l1, lane, float(LANES)), axis=1, keepdims=True)
    rest = jnp.where(lane == i1, NEG, logits)
    l2 = jnp.max(rest, axis=1, keepdims=True)
    i2 = jnp.min(jnp.where(rest == l2, lane, float(LANES)), axis=1, keepdims=True)
    e = jnp.exp(l2 - l1)
    g1 = 1.0 / (1.0 + e)
    g2 = e / (1.0 + e)
    onehot = jnp.where((lane == i1) | (lane == i2), 1.0, 0.0)
    r = lax.broadcasted_iota(jnp.int32, (tm, tm), 0)
    c = lax.broadcasted_iota(jnp.int32, (tm, tm), 1)
    before = _ones_where(c < r)
    carry = carry_scr[0:1, :]
    blk_ref[...] = carry
    prefix = _dot(before, onehot.astype(BF16)) + carry
    r1 = jnp.sum(jnp.where(lane == i1, prefix, 0.0), axis=1, keepdims=True)
    r2 = jnp.sum(jnp.where(lane == i2, prefix, 0.0), axis=1, keepdims=True)
    cols = (i1, i2, g1, g2, r1, r2)
    meta = jnp.zeros((tm, LANES), F32)
    for k, col in enumerate(cols):
        meta = jnp.where(lane_i == k, col, meta)
    meta_ref[...] = meta
    metat_ref[...] = meta.T[0:SUBLANES, :]
    carry = carry + jnp.sum(onehot, axis=0, keepdims=True)
    carry_scr[...] = jnp.broadcast_to(carry, carry_scr.shape)
    cout_ref[...] = carry


def _post_call(o_p, o_s, wo, x_p, x_s, gate_p, gate_s, g, shift_p, shift_s, scale_p, scale_s, wr_pad, br_pad,
               *, tm, blocks_per_seq, n_experts):
    n_p, d = x_p.shape
    n_s = x_s.shape[0]
    npb = n_p // tm
    nb = npb + n_s // tm
    n = n_p + n_s
    const2 = lambda i: (0, 0)
    row = lambda i: (i, 0)
    p_row = lambda i: (jnp.minimum(i, npb - 1), 0)
    s_row = lambda i: (jnp.maximum(i - npb, 0), 0)
    p_mod = pl.BlockSpec((None, 1, d), lambda i: (jnp.minimum(i, npb - 1) // blocks_per_seq, 0, 0))
    s_mod = pl.BlockSpec((None, tm, d), lambda i: (jnp.maximum(i - npb, 0), 0, 0))
    return pl.pallas_call(
        functools.partial(_post_kernel, n_experts=n_experts, n_prompt_blocks=npb),
        grid=(nb,),
        in_specs=[
            pl.BlockSpec((tm, d), p_row), pl.BlockSpec((tm, d), s_row),
            pl.BlockSpec((d, d), const2),
            pl.BlockSpec((tm, d), p_row), pl.BlockSpec((tm, d), s_row),
            p_mod, s_mod,
            pl.BlockSpec((1, d), const2),
            p_mod, s_mod, p_mod, s_mod,
            pl.BlockSpec((d, LANES), const2),
            pl.BlockSpec((1, LANES), const2),
        ],
        out_specs=[
            pl.BlockSpec((tm, d), row), pl.BlockSpec((tm, d), row), pl.BlockSpec((tm, LANES), row),
            pl.BlockSpec((None, SUBLANES, tm), lambda i: (i, 0, 0)),
            pl.BlockSpec((None, 1, LANES), lambda i: (i, 0, 0)),
            pl.BlockSpec((1, LANES), const2),
        ],
        out_shape=[
            jax.ShapeDtypeStruct((n, d), F32), jax.ShapeDtypeStruct((n, d), BF16),
            jax.ShapeDtypeStruct((n, LANES), F32),
            jax.ShapeDtypeStruct((nb, SUBLANES, tm), F32),
            jax.ShapeDtypeStruct((nb, 1, LANES), F32),
            jax.ShapeDtypeStruct((1, LANES), F32),
        ],
        scratch_shapes=[pltpu.VMEM((SUBLANES, LANES), F32)],
        compiler_params=_cparams("arbitrary"),
        name="post_attn_router",
    )(o_p, o_s, wo, x_p, x_s, gate_p, gate_s, g, shift_p, shift_s, scale_p, scale_s, wr_pad, br_pad)


def _sorted_pos(idx, rank, start_ref, n_experts):
    pos = rank
    for e in range(n_experts):
        pos = pos + jnp.where(idx == float(e), start_ref[e].astype(F32), 0.0)
    return pos


def _dispatch_kernel(lo_ref, hi_ref, start_ref, mt_ref, h_ref, xs_ref, gs_ref, acc_scr, gacc_scr,
                     *, tb, n_experts):
    j = pl.program_id(0)
    sub = xs_ref.shape[0]
    nb = mt_ref.shape[0]
    lo = lo_ref[j]
    hi = hi_ref[j]

    @pl.when(hi <= lo)
    def _():
        xs_ref[...] = jnp.zeros_like(xs_ref)
        gs_ref[...] = jnp.zeros_like(gs_ref)

    def hits(b, width):
        dest = (j * sub + lax.broadcasted_iota(jnp.int32, (sub, width * tb), 0)).astype(F32)
        mts = [mt_ref[b + w] for w in range(width)]
        field = lambda r: jnp.concatenate([m[r:r + 1, :] for m in mts], axis=1)
        hit0 = _sorted_pos(field(0), field(4), start_ref, n_experts) == dest
        hit1 = _sorted_pos(field(1), field(5), start_ref, n_experts) == dest
        gate = jnp.where(hit0, field(2), 0.0) + jnp.where(hit1, field(3), 0.0)
        return _ones_where(hit0 | hit1), gate

    n_even = nb - nb % 2

    def pair(m, carry):
        b = 2 * m
        oh, gate = hits(b, 2)
        rows = pl.ds(pl.multiple_of(b * tb, 2 * tb), 2 * tb)
        acc_scr[...] += _dot(oh, h_ref[rows, :])
        gacc_scr[...] += gate
        return carry

    @pl.when(hi > lo)
    def _():
        acc_scr[...] = jnp.zeros_like(acc_scr)
        gacc_scr[...] = jnp.zeros_like(gacc_scr)
        lax.fori_loop(lo // 2, (jnp.minimum(hi, n_even) + 1) // 2, pair, 0)

        if nb % 2:
            @pl.when(hi == nb)
            def _():
                oh, gate = hits(nb - 1, 1)
                acc_scr[...] += _dot(oh, h_ref[(nb - 1) * tb:nb * tb, :])
                gacc_scr[:, 0:tb] += gate

        xs_ref[...] = acc_scr[...].astype(BF16)
        gs_ref[...] = jnp.broadcast_to(jnp.sum(gacc_scr[...], axis=1, keepdims=True), gs_ref.shape)


def _dispatch_call(tile_lo, tile_hi, start, meta_t, h, *, n_sorted, n_experts):
    n, d = h.shape
    nb, _, tb = meta_t.shape
    whole = lambda nd: (lambda j, lo, hi, st: (0,) * nd)
    once = pl.Buffered(1)
    return pl.pallas_call(
        functools.partial(_dispatch_kernel, tb=tb, n_experts=n_experts),
        grid_spec=pltpu.PrefetchScalarGridSpec(
            num_scalar_prefetch=3,
            grid=(n_sorted // SUB,),
            in_specs=[
                pl.BlockSpec((nb, SUBLANES, tb), whole(3), pipeline_mode=once),
                pl.BlockSpec((n, d), whole(2), pipeline_mode=once),
            ],
            out_specs=[pl.BlockSpec((SUB, d), lambda j, lo, hi, st: (j, 0)),
                       pl.BlockSpec((SUB, LANES), lambda j, lo, hi, st: (j, 0))],
            scratch_shapes=[pltpu.VMEM((SUB, d), F32), pltpu.VMEM((SUB, 2 * tb), F32)],
        ),
        out_shape=[jax.ShapeDtypeStruct((n_sorted, d), BF16), jax.ShapeDtypeStruct((n_sorted, LANES), F32)],
        compiler_params=_cparams("parallel"),
        name="moe_dispatch",
    )(tile_lo, tile_hi, start, meta_t, h)


def _moe_kernel(ex_ref, nsub_ref, x_ref, gs_ref, wg_ref, wu_ref, wd_ref, y_ref, acc_scr):
    s = pl.program_id(0)
    j = pl.program_id(1)
    nsub = nsub_ref[s]
    last = j == pl.num_programs(1) - 1

    n_full = (nsub * SUB) // FFN_ROWS
    odd = nsub * SUB - n_full * FFN_ROWS > 0

    def tile_rows(t):
        return pl.ds(pl.multiple_of(t * SUB, SUB), SUB)

    def step_rows(t):
        return pl.ds(pl.multiple_of(t * FFN_ROWS, FFN_ROWS), FFN_ROWS)

    @pl.when(j == 0)
    def _():
        def zero(t, carry):
            acc_scr[tile_rows(t), :] = jnp.zeros((SUB, acc_scr.shape[1]), F32)
            return carry
        lax.fori_loop(0, nsub, zero, 0)

    def ffn(rows):
        x = x_ref[rows, :]
        a = _silu(_dot(x, wg_ref[...].astype(BF16))) * _dot(x, wu_ref[...].astype(BF16))
        acc_scr[rows, :] += _dot(a.astype(BF16), wd_ref[...].astype(BF16))

    def body(t, carry):
        ffn(step_rows(2 * t))
        ffn(step_rows(2 * t + 1))
        return carry

    lax.fori_loop(0, n_full // 2, body, 0)

    @pl.when(n_full % 2 == 1)
    def _():
        ffn(step_rows(n_full - 1))

    @pl.when(odd)
    def _():
        ffn(tile_rows(nsub - 1))

    @pl.when(last)
    def _():
        def emit(t, carry):
            rows = tile_rows(t)
            y_ref[rows, :] = (acc_scr[rows, :] * gs_ref[rows, 0:1]).astype(BF16)
            return carry
        lax.fori_loop(0, nsub, emit, 0)

        def zero(t, carry):
            y_ref[tile_rows(t), :] = jnp.zeros((SUB, y_ref.shape[1]), BF16)
            return carry
        lax.fori_loop(nsub, y_ref.shape[0] // SUB, zero, 0)


def _moe_call(sup_expert, sup_nsub, xs, gs, w_gu, w_down, *, ts):
    n_sorted, d = xs.shape
    f = w_down.shape[1]
    tf = min(TF_FFN, f)
    nf = f // tf
    n_super = n_sorted // ts

    def jj(s, j, nsub):
        return jnp.where(nsub[s] > 0, j, nf - 1)

    return pl.pallas_call(
        _moe_kernel,
        grid_spec=pltpu.PrefetchScalarGridSpec(
            num_scalar_prefetch=2,
            grid=(n_super, nf),
            in_specs=[
                pl.BlockSpec((ts, d), lambda s, j, ex, ns: (s, 0)),
                pl.BlockSpec((ts, LANES), lambda s, j, ex, ns: (s, 0)),
                pl.BlockSpec((None, d, tf), lambda s, j, ex, ns: (ex[s], 0, jj(s, j, ns))),
                pl.BlockSpec((None, d, tf), lambda s, j, ex, ns: (ex[s], 0, nf + jj(s, j, ns))),
                pl.BlockSpec((None, tf, d), lambda s, j, ex, ns: (ex[s], jj(s, j, ns), 0)),
            ],
            out_specs=pl.BlockSpec((ts, d), lambda s, j, ex, ns: (s, 0)),
            scratch_shapes=[pltpu.VMEM((ts, d), F32)],
        ),
        out_shape=jax.ShapeDtypeStruct((n_sorted, d), BF16),
        compiler_params=_cparams("parallel", "arbitrary"),
        name="moe_ffn",
    )(sup_expert, sup_nsub, xs, gs, w_gu, w_gu, w_down)


def _combine_kernel(tile_ref, need2_ref, start_ref, meta_ref, x_ref, gtp_ref, gts_ref, gf_ref, *rest,
                    n_experts, n_prompt_blocks):
    y_refs = rest[0:2 * n_experts]
    op_ref, os_ref, acc_scr = rest[2 * n_experts:]
    b = pl.program_id(0)
    tb = acc_scr.shape[0]
    meta = meta_ref[...]
    col = lax.broadcasted_iota(jnp.int32, (tb, SUB), 1)

    def window(e, w):
        first = start_ref[e].astype(F32)
        p0 = jnp.where(meta[:, 0:1] == float(e), meta[:, 4:5] + first, -1.0)
        p1 = jnp.where(meta[:, 1:2] == float(e), meta[:, 5:6] + first, -1.0)
        src = ((tile_ref[b * n_experts + e] + w) * SUB + col).astype(F32)
        return _ones_where((p0 == src) | (p1 == src))

    total = _dot(window(0, 0), y_refs[0][...])
    for e in range(1, n_experts):
        total = total + _dot(window(e, 0), y_refs[2 * e][...])
    acc_scr[...] = total
    for e in range(n_experts):
        @pl.when(need2_ref[b * n_experts + e] > 0)
        def _(e=e):
            acc_scr[...] += _dot(window(e, 1), y_refs[2 * e + 1][...])

    is_p = b < n_prompt_blocks
    x4 = x_ref[...] + jnp.where(is_p, gtp_ref[...], gts_ref[...]) * acc_scr[...]
    ms = jnp.mean(x4 * x4, axis=-1, keepdims=True)
    y = x4 * lax.rsqrt(ms + EPS) * gf_ref[...]

    @pl.when(is_p)
    def _():
        op_ref[...] = y

    @pl.when(jnp.logical_not(is_p))
    def _():
        os_ref[...] = y


def _combine_call(win_tile, win_need2, start, meta, x, gate_p, gate_s, g_final, ys, *, n_p, tb, blocks_per_seq,
                  n_experts):
    n, d = x.shape
    n_s = n - n_p
    npb = n_p // tb
    nb = n // tb

    def pb(b):
        return jnp.minimum(b, npb - 1)

    def sb(b):
        return jnp.maximum(b - npb, 0)

    def win_map(e, w):
        if w == 0:
            return lambda b, tile, need, st: (tile[b * n_experts + e], 0)
        return lambda b, tile, need, st: (jnp.where(need[b * n_experts + e] > 0, tile[b * n_experts + e] + 1, 0), 0)

    in_specs = [
        pl.BlockSpec((tb, LANES), lambda b, tile, need, st: (b, 0)),
        pl.BlockSpec((tb, d), lambda b, tile, need, st: (b, 0)),
        pl.BlockSpec((None, 1, d), lambda b, tile, need, st: (pb(b) // blocks_per_seq, 0, 0)),
        pl.BlockSpec((None, tb, d), lambda b, tile, need, st: (sb(b), 0, 0)),
        pl.BlockSpec((1, d), lambda b, tile, need, st: (0, 0)),
    ]
    in_specs += [pl.BlockSpec((SUB, d), win_map(e, w)) for e in range(n_experts) for w in range(2)]
    return pl.pallas_call(
        functools.partial(_combine_kernel, n_experts=n_experts, n_prompt_blocks=npb),
        grid_spec=pltpu.PrefetchScalarGridSpec(
            num_scalar_prefetch=3,
            grid=(nb,),
            in_specs=in_specs,
            out_specs=[pl.BlockSpec((tb, d), lambda b, tile, need, st: (pb(b), 0)),
                       pl.BlockSpec((tb, d), lambda b, tile, need, st: (sb(b), 0))],
            scratch_shapes=[pltpu.VMEM((tb, d), F32)],
        ),
        out_shape=[jax.ShapeDtypeStruct((n_p, d), F32), jax.ShapeDtypeStruct((n_s, d), F32)],
        compiler_params=_cparams("arbitrary"),
        name="moe_combine",
    )(win_tile, win_need2, start, meta, x, gate_p, gate_s, g_final, *([ys] * (2 * n_experts)))


def _routing_tables(blk_cnt, counts, *, n_experts, ts, n_super):
    nb = blk_cnt.shape[0]
    cnt = counts[0, :n_experts].astype(jnp.int32)
    n_sup_e = (cnt + ts - 1) // ts
    sup_start = jnp.cumsum(n_sup_e) - n_sup_e
    start = sup_start * ts

    s_ids = jnp.arange(n_super, dtype=jnp.int32)
    used = jnp.sum(n_sup_e)
    sup_end = sup_start + n_sup_e
    sup_e = jnp.sum((s_ids[:, None] >= sup_end[None, :]).astype(jnp.int32), axis=1)
    sup_e = jnp.clip(sup_e, 0, n_experts - 1)
    last_e = jnp.max(jnp.where(cnt > 0, jnp.arange(n_experts, dtype=jnp.int32), 0))
    sup_e = jnp.where(s_ids < used, sup_e, last_e)
    rows_in = jnp.clip(cnt[sup_e] - (s_ids - sup_start[sup_e]) * ts, 0, ts)
    sup_nsub = jnp.where(s_ids < used, (rows_in + SUB - 1) // SUB, 0).astype(jnp.int32)

    blk = blk_cnt[:, 0, :n_experts].astype(jnp.int32)
    blk_end = jnp.concatenate([blk[1:], cnt[None, :]], axis=0)
    n_tiles = n_super * (ts // SUB)
    t_ids = jnp.arange(n_tiles, dtype=jnp.int32)
    t_e = sup_e[t_ids // (ts // SUB)]
    t_r0 = t_ids * SUB - start[t_e]
    t_active = ((t_ids // (ts // SUB)) < used) & (t_r0 < cnt[t_e]) & (t_r0 >= 0)
    t_r1 = jnp.minimum(t_r0 + SUB, cnt[t_e])
    be = blk_end[:, t_e]
    bs = blk[:, t_e]
    overlap = (be > t_r0[None, :]) & (bs < t_r1[None, :])
    b_ids = jnp.arange(nb, dtype=jnp.int32)[:, None]
    lo = jnp.min(jnp.where(overlap, b_ids, nb), axis=0)
    hi = jnp.max(jnp.where(overlap, b_ids + 1, 0), axis=0)
    tile_lo = jnp.where(t_active, lo, 0).astype(jnp.int32)
    tile_hi = jnp.where(t_active, hi, 0).astype(jnp.int32)

    first = start[None, :] + blk
    n_be = blk_end - blk
    win_tile = (first // SUB).astype(jnp.int32)
    win_need2 = ((first + n_be) > (win_tile + 1) * SUB).astype(jnp.int32)

    return dict(start=start.astype(jnp.int32), sup_e=sup_e, sup_nsub=sup_nsub, tile_lo=tile_lo,
                tile_hi=tile_hi, win_tile=win_tile.reshape(-1), win_need2=win_need2.reshape(-1))


def kernel(x_prompt, x_sample, c_prompt, c_sample, cache_k, cache_v, cache_logf, page_table, norm_mix_g, norm_ffn_g, final_norm_g, ada_w, ada_b, gmlp_w_in, gmlp_v_g, gmlp_v_b, gmlp_w_s, gmlp_b_s, gmlp_w_out, fox_w_in, fox_b_f, fox_w_out, ffn_w_gu, ffn_w_down, moe_w_r, moe_b_r, moe_w_gu, moe_w_down):
    n_seq_p, t_len, d = x_prompt.shape
    n_seq_s, t_new, _ = x_sample.shape
    n_p = n_seq_p * t_len
    n_s = n_seq_s * t_new
    heads = N_HEADS
    hd = d // heads
    n_experts = moe_w_r.shape[-1]
    da = gmlp_w_out.shape[1]

    xp = x_prompt.reshape(n_p, d)
    xs = x_sample.reshape(n_s, d)

    mod = _ada_call(jnp.concatenate([c_prompt, c_sample], axis=0), ada_w, ada_b)

    def mods(layer):
        mp = [mod[layer, :n_seq_p, c * d:(c + 1) * d].reshape(n_seq_p, 1, d) for c in range(6)]
        ms = [jnp.repeat(mod[layer, n_seq_p:, c * d:(c + 1) * d], t_new, axis=0).reshape(1, n_s, d)
              for c in range(6)]
        return mp, ms

    row = lambda a: a.reshape(1, -1)

    mp, ms = mods(0)
    w_in = gmlp_w_in[0].astype(BF16)
    wu, wv = w_in[:, :da], w_in[:, da:]
    wo = gmlp_w_out[0].astype(BF16)
    vg, vb = row(gmlp_v_g[0]), row(gmlp_v_b[0])
    lc = min(GMLP_CHUNK, t_len)
    mp1, ms1 = mods(1)
    w_t = fox_w_in[0].T.astype(BF16)
    wqkv_t, wf_t = w_t[:3 * d], w_t[3 * d:]
    b_f = fox_b_f[0].reshape(heads, 1)

    reps = n_s // t_new
    ws_s = jnp.zeros((gmlp_w_s.shape[1], LANES, LANES), F32).at[:, :t_new, :t_new].set(
        gmlp_w_s[0][:, :t_new, :t_new])
    bs_s = jnp.tile(gmlp_b_s[0][:, :t_new], (1, reps)).T
    xs, v_rows = _gmlp_call(xs, row(norm_mix_g[0]), ms[0], ms[1], ms[2], wu, wv, vg, vb, ws_s, bs_s, wo,
                            tm=n_s, tiles_per_seq=1, period=t_new, emit_v=True)
    xs = _ffn_call(xs, row(norm_ffn_g[0]), ms[3], ms[4], ms[5], ffn_w_gu[0], ffn_w_down[0],
                   tm=n_s, tiles_per_seq=1)
    q_s, k_s, v_s, lft_s, cumt_s = _foxproj_call(
        xs, row(norm_mix_g[1]), ms1[0], ms1[1], wqkv_t, wf_t, b_f,
        tm=n_s, tiles_per_seq=1, transposed=False, period=t_new)

    page_size = cache_k.shape[2]
    n_phys = cache_k.shape[1]
    kc_t = jnp.transpose(cache_k[0], (0, 2, 3, 1)).reshape(n_phys, d, page_size)
    vc_t = jnp.transpose(cache_v[0], (0, 2, 3, 1)).reshape(n_phys, d, page_size)
    lfc_t = jnp.transpose(cache_logf[0], (0, 2, 1))
    cn = cumt_s[0].reshape(heads, n_seq_s, t_new).transpose(1, 0, 2)
    cn_pad = jnp.zeros((n_seq_s, heads, LANES), F32).at[:, :, :t_new].set(cn)
    xp, o_s = _gmlp_attn_call(xp, row(norm_mix_g[0]), mp[0], mp[1], mp[2], wu, wv, vg, vb,
                              gmlp_w_s[0][:, :lc, :lc], gmlp_b_s[0][:, :lc].T, wo,
                              page_table, q_s, cn_pad, k_s, v_s, kc_t, vc_t, lfc_t,
                              t_len=t_len, tn=t_new, heads=heads)
    tm = min(TM_FFN, t_len)
    xp = _ffn_call(xp, row(norm_ffn_g[0]), mp[3], mp[4], mp[5], ffn_w_gu[0], ffn_w_down[0],
                   tm=tm, tiles_per_seq=t_len // tm)

    mp, ms = mp1, ms1
    tm = min(TM_PROJ, t_len)
    qt_p, kt_p, kb_p, vt_p, vtb_p, lft_p, cumt_p = _foxproj_call(
        xp, row(norm_mix_g[1]), mp[0], mp[1], wqkv_t, wf_t, b_f,
        tm=tm, tiles_per_seq=t_len // tm, transposed=True, period=tm)
    nq, tq = qt_p.shape[1], qt_p.shape[3]
    o_p = _attn_call(qt_p, kb_p, vtb_p, cumt_p.reshape(n_seq_p, heads // 2, 2, nq, tq), t_len=t_len, heads=heads)

    wo_f = fox_w_out[0].astype(BF16)
    wr_pad = jnp.zeros((d, LANES), F32).at[:, :n_experts].set(moe_w_r[0])
    br_pad = jnp.zeros((1, LANES), F32).at[0, :n_experts].set(moe_b_r[0])
    tb = min(TM_POST, t_len)
    ms_blk = [m.reshape(n_s // tb, tb, d) for m in ms]
    x3, h_all, meta, meta_t, blk_cnt, counts = _post_call(
        o_p, o_s, wo_f, xp, xs, mp[2], ms_blk[2], row(norm_ffn_g[1]), mp[3], ms_blk[3], mp[4], ms_blk[4],
        wr_pad, br_pad, tm=tb, blocks_per_seq=t_len // tb, n_experts=n_experts)

    n_tok = n_p + n_s
    ts = TS_MOE
    n_super = (TOP_K * n_tok) // ts + n_experts + 1
    rt = _routing_tables(blk_cnt, counts, n_experts=n_experts, ts=ts, n_super=n_super)
    xs_sorted, g_sorted = _dispatch_call(rt["tile_lo"], rt["tile_hi"], rt["start"], meta_t, h_all,
                                         n_sorted=n_super * ts, n_experts=n_experts)
    ys_sorted = _moe_call(rt["sup_e"], rt["sup_nsub"], xs_sorted, g_sorted, moe_w_gu[0], moe_w_down[0], ts=ts)
    y_p, y_s = _combine_call(rt["win_tile"], rt["win_need2"], rt["start"], meta, x3, mp[5], ms_blk[5],
                             row(final_norm_g), ys_sorted,
                             n_p=n_p, tb=tb, blocks_per_seq=t_len // tb, n_experts=n_experts)

    y_prompt = y_p.reshape(n_seq_p, t_len, d)
    y_sample = y_s.reshape(n_seq_s, t_new, d)
    state_a_v_sample = v_rows.reshape(1, n_seq_s, t_new, da)
    k_prompt = kt_p.reshape(1, n_seq_p, heads, hd, t_len).transpose(0, 1, 4, 2, 3)
    v_prompt = vt_p.reshape(1, n_seq_p, heads, hd, t_len).transpose(0, 1, 4, 2, 3)
    logf_prompt = lft_p.transpose(0, 2, 1)[None]
    k_sample = k_s.reshape(1, n_seq_s, t_new, heads, hd)
    v_sample = v_s.reshape(1, n_seq_s, t_new, heads, hd)
    logf_sample = lft_s[0].T.reshape(1, n_seq_s, t_new, heads)
    return (y_prompt, y_sample, state_a_v_sample, k_prompt, v_prompt, logf_prompt, k_sample, v_sample, logf_sample)
```
